```python
import jax, jax.numpy as jnp
from jax import lax
import numpy as np

D_MODEL = 1024
BATCH = 32
SEQ = 2048
DEPTH = 2

D_POOL = D_MODEL // 2
POOL_WINDOWS = (2, 4, 8, 16)
POOL_GROUP = D_POOL // len(POOL_WINDOWS)
D_GLA_V = D_MODEL // 2
GLA_HEADS = 4
GLA_DK = (D_GLA_V // 2) // GLA_HEADS
GLA_DV = D_GLA_V // GLA_HEADS
GLA_RANK = 16
GATE_NORM = 16.0
GLA_CHUNK = 64
D_IN = D_POOL + 2 * GLA_HEADS * GLA_DK + 2 * D_GLA_V + 2 * GLA_RANK

N_EXPERTS = 32
N_GROUPS = 4
EXPERTS_PER_GROUP = N_EXPERTS // N_GROUPS
TOP_K = 2
D_FF = 512
MOE_BLOCK = 512
EPS = 1e-6

kernel_name = "hybrid_pool_gla_groupmoe_adaln_encoder"


def rms_norm(x):
    xf = x.astype(jnp.float32)
    y = xf * lax.rsqrt(jnp.mean(xf * xf, axis=-1, keepdims=True) + EPS)
    return y.astype(x.dtype)


def pool_mixer(u, w_pool, pool_scale):
    bn, sn, _ = u.shape
    uf = u.astype(jnp.float32)
    pos = jnp.arange(sn)
    outs = []
    for gi, w in enumerate(POOL_WINDOWS):
        ug = uf[..., gi * POOL_GROUP:(gi + 1) * POOL_GROUP]
        csum = jnp.concatenate([jnp.zeros((bn, 1, POOL_GROUP), jnp.float32), jnp.cumsum(ug, axis=1)], axis=1)
        lo = jnp.clip(pos - w // 2, 0, sn - 1)
        hi = jnp.clip(pos - w // 2 + w - 1, 0, sn - 1)
        cnt = (hi - lo + 1).astype(jnp.float32)[None, :, None]
        outs.append((csum[:, hi + 1] - csum[:, lo]) / cnt - ug)
    pooled = jnp.stack(outs, axis=2).astype(u.dtype)
    y = jnp.einsum('bsgc,gcd->bsgd', pooled, w_pool)
    return y.reshape(bn, sn, D_POOL) * pool_scale


def gla_chunked(q, k, v, loga, strict):
    bn, hn, sn, dk = q.shape
    dv = v.shape[-1]
    nc = sn // GLA_CHUNK
    q = q.reshape(bn, hn, nc, GLA_CHUNK, dk)
    k = k.reshape(bn, hn, nc, GLA_CHUNK, dk)
    v = v.reshape(bn, hn, nc, GLA_CHUNK, dv)
    b = jnp.cumsum(loga.reshape(bn, hn, nc, GLA_CHUNK, dk), axis=3)
    qe = q * jnp.exp(b)
    ke = k * jnp.exp(-b)
    mask = jnp.tril(jnp.ones((GLA_CHUNK, GLA_CHUNK), bool), -1 if strict else 0)
    att = jnp.where(mask, jnp.einsum('bhnid,bhnjd->bhnij', qe, ke), 0.0)
    o = jnp.einsum('bhnij,bhnjv->bhniv', att, v)
    b_last = b[..., -1:, :]
    upd = jnp.einsum('bhncd,bhncv->bhndv', k * jnp.exp(b_last - b), v)
    decay = jnp.exp(b_last[..., 0, :])

    def step(state, inp):
        u_n, g_n = inp
        return g_n[..., None] * state + u_n, state

    _, s_before = lax.scan(step, jnp.zeros((bn, hn, dk, dv), jnp.float32),
                           (jnp.moveaxis(upd, 2, 0), jnp.moveaxis(decay, 2, 0)))
    s_before = jnp.moveaxis(s_before, 0, 2)
    o = o + jnp.einsum('bhncd,bhndv->bhncv', qe, s_before)
    return o.reshape(bn, hn, sn, dv)


def gla_mixer(q, k, v, g, r_f, r_b, w_gk_up, b_gk, norm_g):
    bn, sn, _ = q.shape

    def heads(t, d):
        return t.reshape(bn, sn, GLA_HEADS, d).transpose(0, 2, 1, 3).astype(jnp.float32)

    qh = heads(q, GLA_DK) * (GLA_DK ** -0.5)
    kh = heads(k, GLA_DK)
    vh = heads(v, GLA_DV)
    la_f = heads(jax.nn.log_sigmoid((r_f @ w_gk_up[0] + b_gk[0]).astype(jnp.float32)) / GATE_NORM, GLA_DK)
    la_b = heads(jax.nn.log_sigmoid((r_b @ w_gk_up[1] + b_gk[1]).astype(jnp.float32)) / GATE_NORM, GLA_DK)
    flip = lambda t: jnp.flip(t, axis=2)
    o = gla_chunked(qh, kh, vh, la_f, False) + flip(gla_chunked(flip(qh), flip(kh), flip(vh), flip(la_b), True))
    o = o * lax.rsqrt(jnp.mean(o * o, axis=-1, keepdims=True) + EPS) * norm_g.astype(jnp.float32)
    o = o.transpose(0, 2, 1, 3).reshape(bn, sn, D_GLA_V).astype(q.dtype)
    return o * jax.nn.silu(g)


def moe_ffn(h, w_router, b_router, w1, w3, w2):
    bn, sn, d = h.shape
    t = bn * sn
    ht = h.reshape(t, d)
    s = jax.nn.sigmoid(jnp.dot(ht.astype(jnp.float32), w_router.astype(jnp.float32)))
    sel = (s + b_router.astype(jnp.float32)).reshape(t, N_GROUPS, EXPERTS_PER_GROUP)
    g_idx = jnp.argmax(lax.top_k(sel, TOP_K)[0].sum(-1), axis=-1)
    in_grp = jnp.take_along_axis(sel, g_idx[:, None, None], axis=1)[:, 0]
    _, loc = lax.top_k(in_grp, TOP_K)
    e_idx = (g_idx[:, None] * EXPERTS_PER_GROUP + loc).astype(jnp.int32)
    gw = jnp.take_along_axis(s, e_idx, axis=1)
    gw = gw / jnp.sum(gw, axis=-1, keepdims=True)

    n_assign = t * TOP_K
    flat_e = e_idx.reshape(n_assign)
    flat_w = gw.reshape(n_assign).astype(h.dtype)
    flat_tok = jnp.arange(n_assign, dtype=jnp.int32) // TOP_K
    order = jnp.argsort(flat_e)
    sorted_e = flat_e[order]
    counts = jnp.bincount(flat_e, length=N_EXPERTS).astype(jnp.int32)
    padded = (counts + MOE_BLOCK - 1) // MOE_BLOCK * MOE_BLOCK
    pad_end = jnp.cumsum(padded)
    pad_start = pad_end - padded
    start = jnp.cumsum(counts) - counts
    dest = pad_start[sorted_e] + jnp.arange(n_assign, dtype=jnp.int32) - start[sorted_e]
    n_blocks = -(-n_assign // MOE_BLOCK) + N_EXPERTS
    rows = n_blocks * MOE_BLOCK
    row_tok = jnp.full((rows,), t, jnp.int32).at[dest].set(flat_tok[order])
    row_w = jnp.zeros((rows,), h.dtype).at[dest].set(flat_w[order])
    blk_e = jnp.minimum(jnp.searchsorted(pad_end, jnp.arange(n_blocks, dtype=jnp.int32) * MOE_BLOCK, side='right'),
                        N_EXPERTS - 1)
    x_pad = jnp.concatenate([ht, jnp.zeros((1, d), h.dtype)], axis=0)

    def block_ffn(args):
        e, tok, wt = args
        xb = x_pad[tok]
        hid = jax.nn.silu(xb @ w1[e]) * (xb @ w3[e])
        return (hid @ w2[e]) * wt[:, None]

    ys = lax.map(block_ffn, (blk_e, row_tok.reshape(n_blocks, MOE_BLOCK), row_w.reshape(n_blocks, MOE_BLOCK)))
    out = jnp.zeros((t + 1, d), h.dtype).at[row_tok].add(ys.reshape(rows, d))
    return out[:t].reshape(bn, sn, d)


def setup_inputs(seed: int = 0) -> dict:
    key = jax.random.key(seed)
    ks = jax.random.split(key, 20)
    f32 = jnp.float32
    nrm = lambda k, shape, scale: jax.random.normal(k, shape, f32) * scale
    L, D = DEPTH, D_MODEL
    return {
        "x": nrm(ks[0], (BATCH, SEQ, D), 1.0),
        "c": nrm(ks[1], (BATCH, D), 1.0),
        "w_mod": nrm(ks[2], (L, D, 6 * D), 0.5 * D ** -0.5),
        "b_mod": nrm(ks[3], (L, 6 * D), 0.02),
        "w_in": nrm(ks[4], (L, D, D_IN), D ** -0.5),
        "w_pool": nrm(ks[5], (L, len(POOL_WINDOWS), POOL_GROUP, POOL_GROUP), POOL_GROUP ** -0.5),
        "pool_scale": 1.0 + nrm(ks[6], (L, D_POOL), 0.02),
        "w_gk_up": nrm(ks[7], (L, 2, GLA_RANK, GLA_HEADS * GLA_DK), GLA_RANK ** -0.5),
        "b_gk": nrm(ks[8], (L, 2, GLA_HEADS * GLA_DK), 0.1),
        "gla_norm_g": 1.0 + nrm(ks[9], (L, GLA_DV), 0.02),
        "w_out": nrm(ks[10], (L, D, D), D ** -0.5),
        "w_router": nrm(ks[11], (D, N_EXPERTS), D ** -0.5),
        "b_router": nrm(ks[12], (N_EXPERTS,), 0.01),
        "w1": nrm(ks[13], (L, N_EXPERTS, D, D_FF), D ** -0.5),
        "w3": nrm(ks[14], (L, N_EXPERTS, D, D_FF), D ** -0.5),
        "w2": nrm(ks[15], (L, N_EXPERTS, D_FF, D), D_FF ** -0.5),
        "final_g": 1.0 + nrm(ks[16], (D,), 0.02),
    }


def reference(x, c, w_mod, b_mod, w_in, w_pool, pool_scale, w_gk_up, b_gk, gla_norm_g, w_out,
              w_router, b_router, w1, w3, w2, final_g):
    split_at = np.cumsum([D_POOL, GLA_HEADS * GLA_DK, GLA_HEADS * GLA_DK, D_GLA_V, D_GLA_V, GLA_RANK]).tolist()
    c_act = jax.nn.silu(c)
    for l in range(DEPTH):
        mod = (c_act @ w_mod[l] + b_mod[l])[:, None, :]
        sh1, sc1, g1, sh2, sc2, g2 = jnp.split(mod, 6, axis=-1)
        h = rms_norm(x) * (1.0 + sc1) + sh1
        z = h @ w_in[l]
        u_pool, q, k, v, g, r_f, r_b = jnp.split(z, split_at, axis=-1)
        y_pool = pool_mixer(u_pool, w_pool[l], pool_scale[l])
        y_gla = gla_mixer(q, k, v, g, r_f, r_b, w_gk_up[l], b_gk[l], gla_norm_g[l])
        mix = jnp.concatenate([y_pool, y_gla], axis=-1) @ w_out[l]
        x = x + g1 * mix
        h = rms_norm(x) * (1.0 + sc2) + sh2
        x = x + g2 * moe_ffn(h, w_router, b_router, w1[l], w3[l], w2[l])
    return rms_norm(x) * final_g
```

```python
import functools

import jax
import jax.numpy as jnp
from jax import lax
from jax.experimental import pallas as pl
from jax.experimental.pallas import tpu as pltpu

F32 = jnp.float32
BF16 = jnp.bfloat16
I32 = jnp.int32

D_MODEL = 1024
DEPTH = 2
D_POOL = 512
POOL_WINDOWS = (2, 4, 8, 16)
POOL_GROUP = 128
POOL_HALO = 16
D_GLA_V = 512
GLA_HEADS = 4
GLA_DK = 64
GLA_DV = 128
D_QK = GLA_HEADS * GLA_DK
GLA_RANK = 16
GATE_NORM = 16.0
GLA_CHUNK = 64
D_MAIN = D_POOL + 2 * D_QK + 2 * D_GLA_V
N_EXPERTS = 32
N_GROUPS = 4
EXPERTS_PER_GROUP = 8
D_FF = 512
EPS = 1e-6

SEQ_TILE = 512
CUMSUM_GROUP = 256
MOE_BLOCK = 512
ROW_TILE = 256
VMEM_LIMIT = 56 * 1024 * 1024

NT_DIMS = (((1,), (1,)), ((), ()))
TN_DIMS = (((0,), (0,)), ((), ()))


def _dot(a, b):
    return jnp.dot(a, b, preferred_element_type=F32)


def _split_bf16(a):
    hi = a.astype(BF16)
    lo = (a - hi.astype(F32)).astype(BF16)
    return hi, lo


def _mod_kernel(c_ref, w_ref, b_ref, o_ref):
    c = c_ref[...]
    c_act = (c * jax.nn.sigmoid(c)).astype(BF16)
    o_ref[0] = _dot(c_act, w_ref[0].astype(BF16)) + b_ref[0]


def _modulation(c, w_mod, b_mod):
    n_layers, d, n_out = w_mod.shape
    bn = c.shape[0]
    tn = 1536
    return pl.pallas_call(
        _mod_kernel,
        grid=(n_layers, n_out // tn),
        in_specs=[
            pl.BlockSpec((bn, d), lambda l, j: (0, 0)),
            pl.BlockSpec((1, d, tn), lambda l, j: (l, 0, j)),
            pl.BlockSpec((1, 1, tn), lambda l, j: (l, 0, j)),
        ],
        out_specs=pl.BlockSpec((1, bn, tn), lambda l, j: (l, 0, j)),
        out_shape=jax.ShapeDtypeStruct((n_layers, bn, n_out), F32),
        compiler_params=pltpu.CompilerParams(
            dimension_semantics=("arbitrary", "arbitrary"), vmem_limit_bytes=VMEM_LIMIT),
        name="modulation",
    )(c, w_mod, b_mod.reshape(n_layers, 1, n_out))


def _log_sigmoid(x):
    return jnp.minimum(x, 0.0) - jnp.log1p(jnp.exp(-jnp.abs(x)))


def _inproj_kernel(x_ref, sc_ref, sh_ref, wmain_ref, wr_ref, wgk_ref, bgk_ref,
                   u_ref, g_ref, v_ref, qf_ref, kf_ref, df_ref, qb_ref, kb_ref, db_ref,
                   decf_ref, decb_ref):
    x = x_ref[0]
    ms = jnp.mean(x * x, axis=-1, keepdims=True)
    h = x * lax.rsqrt(ms + EPS) * (1.0 + sc_ref[0]) + sh_ref[0]
    hb = h.astype(BF16)
    z = _dot(hb, wmain_ref[...])
    r = _dot(hb, wr_ref[...]).astype(BF16)
    u_ref[0] = z[:, 0:D_POOL]
    q = z[:, D_POOL:D_POOL + D_QK] * (GLA_DK ** -0.5)
    k = z[:, D_POOL + D_QK:D_POOL + 2 * D_QK]
    v_ref[0] = z[:, D_POOL + 2 * D_QK:D_POOL + 2 * D_QK + D_GLA_V].astype(BF16)
    g_ref[0] = z[:, D_POOL + 2 * D_QK + D_GLA_V:D_MAIN]

    tile = x.shape[0]
    n_chunks = tile // GLA_CHUNK
    row = lax.broadcasted_iota(I32, (CUMSUM_GROUP, CUMSUM_GROUP), 0)
    col = lax.broadcasted_iota(I32, (CUMSUM_GROUP, CUMSUM_GROUP), 1)
    same_chunk = (row // GLA_CHUNK) == (col // GLA_CHUNK)
    prefix = jnp.where(same_chunk & (col <= row), 1.0, 0.0).astype(BF16)
    suffix = jnp.where(same_chunk & (col >= row), 1.0, 0.0).astype(BF16)

    def direction(idx, tri, last_row, q_out, k_out, d_out, dec_out):
        gate = _dot(r, wgk_ref[idx]) + bgk_ref[idx]
        loga = _log_sigmoid(gate) / GATE_NORM
        parts = []
        for s in range(tile // CUMSUM_GROUP):
            hi, lo = _split_bf16(loga[s * CUMSUM_GROUP:(s + 1) * CUMSUM_GROUP])
            parts.append(_dot(tri, hi) + _dot(tri, lo))
        b = jnp.concatenate(parts, axis=0)
        b3 = b.reshape(n_chunks, GLA_CHUNK, D_QK)
        total = b3[:, last_row:last_row + 1, :]
        q_out[0] = (q * jnp.exp(b)).astype(BF16)
        k_out[0] = (k * jnp.exp(-b)).astype(BF16)
        d_out[0] = (k * jnp.exp(total - b3).reshape(tile, D_QK)).astype(BF16)
        dec_out[0] = jnp.exp(total).reshape(n_chunks, D_QK)

    direction(0, prefix, GLA_CHUNK - 1, qf_ref, kf_ref, df_ref, decf_ref)
    direction(1, suffix, 0, qb_ref, kb_ref, db_ref, decb_ref)


def _inproj(x, sc, sh, w_main, w_r, w_gk, b_gk):
    bn, sn, d = x.shape
    ts = SEQ_TILE
    n_chunks = ts // GLA_CHUNK
    tok = lambda width: pl.BlockSpec((1, ts, width), lambda b, j: (b, j, 0))
    per_batch = pl.BlockSpec((1, 1, d), lambda b, j: (b, 0, 0))
    full = lambda a: pl.BlockSpec(a.shape, lambda b, j: (0,) * a.ndim)
    dec_spec = pl.BlockSpec((1, n_chunks, D_QK), lambda b, j: (b, j, 0))
    tok_shape = lambda width, dt: jax.ShapeDtypeStruct((bn, sn, width), dt)
    dec_shape = jax.ShapeDtypeStruct((bn, sn // GLA_CHUNK, D_QK), F32)
    return pl.pallas_call(
        _inproj_kernel,
        grid=(bn, sn // ts),
        in_specs=[tok(d), per_batch, per_batch, full(w_main), full(w_r), full(w_gk), full(b_gk)],
        out_specs=[tok(D_POOL), tok(D_GLA_V), tok(D_GLA_V)] + [tok(D_QK)] * 6 + [dec_spec, dec_spec],
        out_shape=[tok_shape(D_POOL, F32), tok_shape(D_GLA_V, F32), tok_shape(D_GLA_V, BF16)]
        + [tok_shape(D_QK, BF16)] * 6 + [dec_shape, dec_shape],
        compiler_params=pltpu.CompilerParams(
            dimension_semantics=("arbitrary", "arbitrary"), vmem_limit_bytes=VMEM_LIMIT),
        name="inproj",
    )(x, sc, sh, w_main, w_r, w_gk, b_gk)


def _gla_kernel(qf_ref, kf_ref, df_ref, qb_ref, kb_ref, db_ref, v_ref, decf_ref, decb_ref,
                g_ref, ng_ref, y_ref, o_acc, state):
    sn = v_ref.shape[1]
    n_chunks = sn // GLA_CHUNK
    lane_head = lax.broadcasted_iota(I32, (GLA_CHUNK, D_QK), 1) // GLA_DK
    head_masks = [lane_head == hd for hd in range(GLA_HEADS)]
    state_blocks = (lax.broadcasted_iota(I32, (D_GLA_V, D_QK), 0) // GLA_DV
                    == lax.broadcasted_iota(I32, (D_GLA_V, D_QK), 1) // GLA_DK)
    ri = lax.broadcasted_iota(I32, (GLA_CHUNK, GLA_CHUNK), 0)
    ci = lax.broadcasted_iota(I32, (GLA_CHUNK, GLA_CHUNK), 1)
    causal = ci <= ri
    anti = ci > ri

    def chunk_out(n, q_ref, k_ref, d_ref, dec_ref, mask):
        rows = pl.ds(pl.multiple_of(n * GLA_CHUNK, GLA_CHUNK), GLA_CHUNK)
        qe = q_ref[0, rows, :]
        ke = k_ref[0, rows, :]
        kd = d_ref[0, rows, :]
        vv = v_ref[0, rows, :]
        q_heads = jnp.concatenate([jnp.where(m, qe, jnp.zeros_like(qe)) for m in head_masks], axis=0)
        att_all = lax.dot_general(q_heads, ke, NT_DIMS, preferred_element_type=F32)
        outs = []
        for hd in range(GLA_HEADS):
            att = jnp.where(mask, att_all[hd * GLA_CHUNK:(hd + 1) * GLA_CHUNK], 0.0).astype(BF16)
            outs.append(_dot(att, vv[:, hd * GLA_DV:(hd + 1) * GLA_DV]))
        st = state[...]
        st_heads = jnp.where(state_blocks, st, 0.0).astype(BF16)
        inter = lax.dot_general(qe, st_heads, NT_DIMS, preferred_element_type=F32)
        upd = lax.dot_general(vv, kd, TN_DIMS, preferred_element_type=F32)
        state[...] = st * dec_ref[0, pl.ds(n, 1), :] + upd
        return jnp.concatenate(outs, axis=1) + inter, rows

    state[...] = jnp.zeros_like(state)

    def fwd_body(n, carry):
        o, rows = chunk_out(n, qf_ref, kf_ref, df_ref, decf_ref, causal)
        o_acc[rows, :] = o
        return carry

    lax.fori_loop(0, n_chunks, fwd_body, 0)

    state[...] = jnp.zeros_like(state)
    norm_g = ng_ref[...]

    def bwd_body(i, carry):
        n = n_chunks - 1 - i
        o, rows = chunk_out(n, qb_ref, kb_ref, db_ref, decb_ref, anti)
        o = o + o_acc[rows, :]
        gate = g_ref[0, rows, :]
        gate = gate * jax.nn.sigmoid(gate)
        outs = []
        for hd in range(GLA_HEADS):
            oh = o[:, hd * GLA_DV:(hd + 1) * GLA_DV]
            oh = oh * lax.rsqrt(jnp.mean(oh * oh, axis=-1, keepdims=True) + EPS) * norm_g
            outs.append(oh)
        y_ref[0, rows, :] = (jnp.concatenate(outs, axis=1) * gate).astype(BF16)
        return carry

    lax.fori_loop(0, n_chunks, bwd_body, 0)


def _gla(qf, kf, df, qb, kb, db, v, decf, decb, g, norm_g):
    bn, sn, _ = v.shape
    seq = lambda a: pl.BlockSpec((1,) + a.shape[1:], lambda b: (b, 0, 0))
    args = (qf, kf, df, qb, kb, db, v, decf, decb, g)
    return pl.pallas_call(
        _gla_kernel,
        grid=(bn,),
        in_specs=[seq(a) for a in args] + [pl.BlockSpec((1, GLA_DV), lambda b: (0, 0))],
        out_specs=pl.BlockSpec((1, sn, D_GLA_V), lambda b: (b, 0, 0)),
        out_shape=jax.ShapeDtypeStruct((bn, sn, D_GLA_V), BF16),
        scratch_shapes=[pltpu.VMEM((sn, D_GLA_V), F32), pltpu.VMEM((D_GLA_V, D_QK), F32)],
        compiler_params=pltpu.CompilerParams(
            dimension_semantics=("arbitrary",), vmem_limit_bytes=VMEM_LIMIT),
        name="gla",
    )(*args, norm_g.reshape(1, GLA_DV))


def _mix_kernel(u_ref, up_ref, un_ref, yg_ref, x_ref, wp_ref, ps_ref, wo_ref, g1_ref, sc_ref, sh_ref,
                wrh_ref, wrl_ref, br_ref,
                xo_ref, h_ref, ri_ref, rw_ref, cnt_ref, carry, *, seq_len):
    b = pl.program_id(0)
    j = pl.program_id(1)
    ts = u_ref.shape[1]

    @pl.when((b == 0) & (j == 0))
    def _():
        carry[...] = jnp.zeros_like(carry)

    prev_ok = jnp.where(j > 0, 1.0, 0.0)
    next_ok = jnp.where(j < pl.num_programs(1) - 1, 1.0, 0.0)
    ext = jnp.concatenate([up_ref[0, 0] * prev_ok, u_ref[0], un_ref[0, 0] * next_ok], axis=0)
    pos = j * ts + lax.broadcasted_iota(I32, (ts, 1), 0)
    pooled = []
    for gi, w in enumerate(POOL_WINDOWS):
        e = ext[:, gi * POOL_GROUP:(gi + 1) * POOL_GROUP]
        acc = e[POOL_HALO - w // 2:POOL_HALO - w // 2 + ts]
        for dlt in range(-(w // 2) + 1, w // 2):
            acc = acc + e[POOL_HALO + dlt:POOL_HALO + dlt + ts]
        lo = jnp.clip(pos - w // 2, 0, seq_len - 1)
        hi = jnp.clip(pos - w // 2 + w - 1, 0, seq_len - 1)
        cnt = (hi - lo + 1).astype(F32)
        pooled.append(acc / cnt - e[POOL_HALO:POOL_HALO + ts])
    y_pool = [(_dot(pooled[gi].astype(BF16), wp_ref[gi])
               * ps_ref[:, gi * POOL_GROUP:(gi + 1) * POOL_GROUP]).astype(BF16)
              for gi in range(len(POOL_WINDOWS))]
    y = jnp.concatenate(y_pool + [yg_ref[0]], axis=1)
    x = x_ref[0] + g1_ref[0] * _dot(y, wo_ref[...])
    xo_ref[0] = x

    ms = jnp.mean(x * x, axis=-1, keepdims=True)
    h = x * lax.rsqrt(ms + EPS) * (1.0 + sc_ref[0]) + sh_ref[0]
    h_ref[0] = h

    h_hi, h_lo = _split_bf16(h)
    logits = (lax.dot_general(wrh_ref[...], h_hi, NT_DIMS, preferred_element_type=F32)
              + lax.dot_general(wrh_ref[...], h_lo, NT_DIMS, preferred_element_type=F32)
              + lax.dot_general(wrl_ref[...], h_hi, NT_DIMS, preferred_element_type=F32))
    score = jax.nn.sigmoid(logits)
    sel = (score + br_ref[...]).reshape(N_GROUPS, EXPERTS_PER_GROUP, ts)
    local = lax.broadcasted_iota(I32, sel.shape, 1)
    big = EXPERTS_PER_GROUP
    m1 = jnp.max(sel, axis=1, keepdims=True)
    i1 = jnp.min(jnp.where(sel == m1, local, big), axis=1, keepdims=True)
    rest = jnp.where(local == i1, -jnp.inf, sel)
    m2 = jnp.max(rest, axis=1, keepdims=True)
    i2 = jnp.min(jnp.where(rest == m2, local, big), axis=1, keepdims=True)
    gscore = m1 + m2
    gid = lax.broadcasted_iota(I32, gscore.shape, 0)
    gbest = jnp.max(gscore, axis=0, keepdims=True)
    gsel = jnp.min(jnp.where(gscore == gbest, gid, N_GROUPS), axis=0, keepdims=True)
    in_group = gid == gsel
    pick0 = (in_group & (local == i1)).reshape(N_EXPERTS, ts)
    pick1 = (in_group & (local == i2)).reshape(N_EXPERTS, ts)
    eid = lax.broadcasted_iota(I32, (N_EXPERTS, ts), 0)
    e0 = jnp.sum(jnp.where(pick0, eid, 0), axis=0, keepdims=True)
    e1 = jnp.sum(jnp.where(pick1, eid, 0), axis=0, keepdims=True)
    s0 = jnp.sum(jnp.where(pick0, score, 0.0), axis=0, keepdims=True)
    s1 = jnp.sum(jnp.where(pick1, score, 0.0), axis=0, keepdims=True)
    rw_ref[0:1, :] = s0 / (s0 + s1)
    rw_ref[1:2, :] = s1 / (s0 + s1)

    onehot = jnp.where(pick0 | pick1, 1.0, 0.0)
    tr = lax.broadcasted_iota(I32, (ts, ts), 0)
    tc = lax.broadcasted_iota(I32, (ts, ts), 1)
    earlier = jnp.where(tr < tc, 1.0, 0.0).astype(BF16)
    before = _dot(onehot.astype(BF16), earlier) + carry[...]
    r0 = jnp.sum(jnp.where(pick0, before, 0.0), axis=0, keepdims=True)
    r1 = jnp.sum(jnp.where(pick1, before, 0.0), axis=0, keepdims=True)
    ri_ref[0:1, :] = e0
    ri_ref[1:2, :] = e1
    ri_ref[2:3, :] = r0.astype(I32)
    ri_ref[3:4, :] = r1.astype(I32)
    carry[...] = carry[...] + jnp.sum(onehot, axis=1, keepdims=True)
    cnt_ref[...] = carry[...]


def _mix(u, y_gla, x, w_pool, pool_scale, w_out, g1, sc2, sh2, wr_hi, wr_lo, b_router):
    bn, sn, d = x.shape
    ts = SEQ_TILE
    nt = sn // ts
    hpt = ts // POOL_HALO
    n_halo = sn // POOL_HALO
    u_halo = u.reshape(bn, n_halo, POOL_HALO, D_POOL)
    tok = lambda width: pl.BlockSpec((1, ts, width), lambda b, j: (b, j, 0))
    per_batch = pl.BlockSpec((1, 1, d), lambda b, j: (b, 0, 0))
    full = lambda a: pl.BlockSpec(a.shape, lambda b, j: (0,) * a.ndim)
    prev_halo = pl.BlockSpec((1, 1, POOL_HALO, D_POOL),
                             lambda b, j: (b, jnp.maximum(j * hpt - 1, 0), 0, 0))
    next_halo = pl.BlockSpec((1, 1, POOL_HALO, D_POOL),
                             lambda b, j: (b, jnp.minimum((j + 1) * hpt, n_halo - 1), 0, 0))
    lane_tok = lambda rows: pl.BlockSpec((rows, ts), lambda b, j: (0, b * nt + j))
    t = bn * sn
    return pl.pallas_call(
        functools.partial(_mix_kernel, seq_len=sn),
        grid=(bn, nt),
        in_specs=[tok(D_POOL), prev_halo, next_halo, tok(D_GLA_V), tok(d), full(w_pool), full(pool_scale),
                  full(w_out), per_batch, per_batch, per_batch, full(wr_hi), full(wr_lo), full(b_router)],
        out_specs=[tok(d), tok(d), lane_tok(4), lane_tok(2),
                   pl.BlockSpec((N_EXPERTS, 1), lambda b, j: (0, 0))],
        out_shape=[jax.ShapeDtypeStruct((bn, sn, d), F32), jax.ShapeDtypeStruct((bn, sn, d), F32),
                   jax.ShapeDtypeStruct((4, t), I32), jax.ShapeDtypeStruct((2, t), F32),
                   jax.ShapeDtypeStruct((N_EXPERTS, 1), F32)],
        scratch_shapes=[pltpu.VMEM((N_EXPERTS, 1), F32)],
        compiler_params=pltpu.CompilerParams(
            dimension_semantics=("arbitrary", "arbitrary"), vmem_limit_bytes=VMEM_LIMIT),
        name="mix_route",
    )(u, u_halo, u_halo, y_gla, x, w_pool, pool_scale, w_out, g1, sc2, sh2, wr_hi, wr_lo, b_router)


def _dispatch_kernel(dest_ref, h_ref, zeros_ref, xs_ref, sem):
    del zeros_ref
    n_rows = h_ref.shape[0]

    def copy(a):
        return pltpu.make_async_copy(h_ref.at[pl.ds(a // 2, 1), :],
                                     xs_ref.at[pl.ds(dest_ref[0, 0, a], 1), :], sem)

    def start(a, carry):
        copy(a).start()
        return carry

    def wait(a, carry):
        copy(a).wait()
        return carry

    lax.fori_loop(0, 2 * n_rows, start, 0)
    lax.fori_loop(0, 2 * n_rows, wait, 0)


def _dispatch(dest, h, n_rows_sorted):
    t, d = h.shape
    tr = ROW_TILE
    zeros = jnp.zeros((n_rows_sorted, d), F32)
    return pl.pallas_call(
        _dispatch_kernel,
        grid=(t // tr,),
        in_specs=[pl.BlockSpec((1, 1, 2 * tr), lambda i: (i, 0, 0), memory_space=pltpu.SMEM),
                  pl.BlockSpec((tr, d), lambda i: (i, 0)),
                  pl.BlockSpec(memory_space=pl.ANY)],
        out_specs=pl.BlockSpec(memory_space=pl.ANY),
        out_shape=jax.ShapeDtypeStruct((n_rows_sorted, d), F32),
        input_output_aliases={2: 0},
        scratch_shapes=[pltpu.SemaphoreType.DMA(())],
        compiler_params=pltpu.CompilerParams(
            dimension_semantics=("arbitrary",), vmem_limit_bytes=VMEM_LIMIT),
        name="dispatch",
    )(dest.reshape(t // tr, 1, 2 * tr), h, zeros)


def _ffn_kernel(blk_e_ref, n_used_ref, xs_ref, w1_ref, w3_ref, w2_ref, ys_ref):
    del blk_e_ref

    @pl.when(pl.program_id(0) < n_used_ref[0])
    def _():
        xb = xs_ref[...].astype(BF16)
        a = _dot(xb, w1_ref[0, 0])
        hid = (a * jax.nn.sigmoid(a)) * _dot(xb, w3_ref[0, 0])
        ys_ref[...] = _dot(hid.astype(BF16), w2_ref[0, 0])


def _ffn(layer, blk_e, n_used, xs, w1, w3, w2):
    rows, d = xs.shape
    n_blocks = rows // MOE_BLOCK
    last = lambda i, n_used_ref: jnp.minimum(i, n_used_ref[0] - 1)
    row_spec = pl.BlockSpec((MOE_BLOCK, d), lambda i, be, nu: (last(i, nu), 0))
    w_spec = lambda a: pl.BlockSpec((1, 1) + a.shape[2:], lambda i, be, nu: (layer, be[last(i, nu)], 0, 0))
    return pl.pallas_call(
        _ffn_kernel,
        grid_spec=pltpu.PrefetchScalarGridSpec(
            num_scalar_prefetch=2,
            grid=(n_blocks,),
            in_specs=[row_spec, w_spec(w1), w_spec(w3), w_spec(w2)],
            out_specs=row_spec,
        ),
        out_shape=jax.ShapeDtypeStruct((rows, d), F32),
        compiler_params=pltpu.CompilerParams(
            dimension_semantics=("arbitrary",), vmem_limit_bytes=VMEM_LIMIT),
        name="expert_ffn",
    )(blk_e, n_used, xs, w1, w3, w2)


def _combine_kernel(dest_ref, ys_ref, x_ref, w_ref, g2_ref, fg_ref, o_ref, rows_buf, sem, *, final_norm):
    n_rows = x_ref.shape[0]

    def copy(a):
        return pltpu.make_async_copy(ys_ref.at[pl.ds(dest_ref[0, 0, a], 1), :],
                                     rows_buf.at[a % 2, pl.ds(a // 2, 1), :], sem)

    def start(a, carry):
        copy(a).start()
        return carry

    def wait(a, carry):
        copy(a).wait()
        return carry

    lax.fori_loop(0, 2 * n_rows, start, 0)
    lax.fori_loop(0, 2 * n_rows, wait, 0)
    w = w_ref[...]
    moe = rows_buf[0] * w[:, 0:1] + rows_buf[1] * w[:, 1:2]
    x = x_ref[...] + g2_ref[0] * moe
    if final_norm:
        ms = jnp.mean(x * x, axis=-1, keepdims=True)
        x = x * lax.rsqrt(ms + EPS) * fg_ref[...]
    o_ref[...] = x


def _combine(dest, ys, x, w, g2, final_g, seq_len, final_norm):
    t, d = x.shape
    tr = ROW_TILE
    tiles_per_seq = seq_len // tr
    return pl.pallas_call(
        functools.partial(_combine_kernel, final_norm=final_norm),
        grid=(t // tr,),
        in_specs=[pl.BlockSpec((1, 1, 2 * tr), lambda i: (i, 0, 0), memory_space=pltpu.SMEM),
                  pl.BlockSpec(memory_space=pl.ANY),
                  pl.BlockSpec((tr, d), lambda i: (i, 0)),
                  pl.BlockSpec((tr, 2), lambda i: (i, 0)),
                  pl.BlockSpec((1, 1, d), lambda i: (i // tiles_per_seq, 0, 0)),
                  pl.BlockSpec((1, d), lambda i: (0, 0))],
        out_specs=pl.BlockSpec((tr, d), lambda i: (i, 0)),
        out_shape=jax.ShapeDtypeStruct((t, d), F32),
        scratch_shapes=[pltpu.VMEM((2, tr, d), F32), pltpu.SemaphoreType.DMA(())],
        compiler_params=pltpu.CompilerParams(
            dimension_semantics=("arbitrary",), vmem_limit_bytes=VMEM_LIMIT),
        name="combine",
    )(dest.reshape(t // tr, 1, 2 * tr), ys, x, w, g2, final_g.reshape(1, d))


def _dispatch_plan(route_i, counts, n_tokens):
    counts = counts.astype(I32)
    padded = (counts + MOE_BLOCK - 1) // MOE_BLOCK * MOE_BLOCK
    pad_end = jnp.cumsum(padded)
    pad_start = pad_end - padded
    dest = jnp.stack([pad_start[route_i[0]] + route_i[2], pad_start[route_i[1]] + route_i[3]], axis=1)
    n_blocks = (n_tokens * 2) // MOE_BLOCK + N_EXPERTS
    blk_start = jnp.arange(n_blocks, dtype=I32) * MOE_BLOCK
    blk_e = jnp.minimum(jnp.searchsorted(pad_end, blk_start, side="right"), N_EXPERTS - 1).astype(I32)
    n_used = (pad_end[-1] // MOE_BLOCK).astype(I32).reshape(1)
    return dest.reshape(-1).astype(I32), blk_e, n_used, n_blocks * MOE_BLOCK


def kernel(x, c, w_mod, b_mod, w_in, w_pool, pool_scale, w_gk_up, b_gk, gla_norm_g, w_out,
           w_router, b_router, w1, w3, w2, final_g):
    bn, sn, d = x.shape
    t = bn * sn
    mod = _modulation(c, w_mod, b_mod).reshape(DEPTH, bn, 6, 1, d)

    w_main = w_in[:, :, :D_MAIN].astype(BF16)
    w_r = w_in[:, :, D_MAIN:].astype(BF16)
    zero_rank = jnp.zeros_like(w_gk_up[:, 0])
    w_gk = jnp.stack([jnp.concatenate([w_gk_up[:, 0], zero_rank], axis=1),
                      jnp.concatenate([zero_rank, w_gk_up[:, 1]], axis=1)], axis=1).astype(BF16)
    b_gk3 = b_gk.reshape(DEPTH, 2, 1, D_QK)
    w_pool_b = w_pool.astype(BF16)
    w_out_b = w_out.astype(BF16)
    wr_t = w_router.T
    wr_hi = wr_t.astype(BF16)
    wr_lo = (wr_t - wr_hi.astype(F32)).astype(BF16)
    br = b_router.reshape(N_EXPERTS, 1)
    w1_b, w3_b, w2_b = w1.astype(BF16), w3.astype(BF16), w2.astype(BF16)

    for l in range(DEPTH):
        sh1, sc1, g1, sh2, sc2, g2 = (mod[l, :, i] for i in range(6))
        u, g, v, qf, kf, df, qb, kb, db, decf, decb = _inproj(
            x, sc1, sh1, w_main[l], w_r[l], w_gk[l], b_gk3[l])
        y_gla = _gla(qf, kf, df, qb, kb, db, v, decf, decb, g, gla_norm_g[l])
        x, h, route_i, route_w, counts = _mix(
            u, y_gla, x, w_pool_b[l], pool_scale[l].reshape(1, D_POOL), w_out_b[l], g1, sc2, sh2,
            wr_hi, wr_lo, br)
        dest, blk_e, n_used, n_rows_sorted = _dispatch_plan(route_i, counts[:, 0], t)
        xs = _dispatch(dest, h.reshape(t, d), n_rows_sorted)
        ys = _ffn(l, blk_e, n_used, xs, w1_b, w3_b, w2_b)
        x = _combine(dest, ys, x.reshape(t, d), route_w.T, g2, final_g, sn, l == DEPTH - 1)
        x = x.reshape(bn, sn, d)
    return x
```

```python
import functools

import jax
import jax.numpy as jnp
from jax import lax
from jax.experimental import pallas as pl
from jax.experimental.pallas import tpu as pltpu

F32 = jnp.float32
BF16 = jnp.bfloat16
I32 = jnp.int32

D_MODEL = 1024
DEPTH = 2
D_POOL = 512
POOL_WINDOWS = (2, 4, 8, 16)
POOL_GROUP = 128
POOL_HALO = 16
D_GLA_V = 512
GLA_HEADS = 4
GLA_DK = 64
GLA_DV = 128
D_QK = GLA_HEADS * GLA_DK
GLA_RANK = 16
GATE_NORM = 16.0
GLA_CHUNK = 64
D_MAIN = D_POOL + 2 * D_QK + 2 * D_GLA_V
N_EXPERTS = 32
N_GROUPS = 4
EXPERTS_PER_GROUP = 8
D_FF = 512
EPS = 1e-6

SEQ_TILE = 512
CUMSUM_GROUP = 256
MOE_BLOCK = 512
LANE = 128
ROW_SUBLANES = D_MODEL // LANE
META_PER_TILE = 3 * N_EXPERTS
VMEM_LIMIT = 56 * 1024 * 1024

NT_DIMS = (((1,), (1,)), ((), ()))
TN_DIMS = (((0,), (0,)), ((), ()))


def _dot(a, b):
    return jnp.dot(a, b, preferred_element_type=F32)


def _split_bf16(a):
    hi = a.astype(BF16)
    lo = (a - hi.astype(F32)).astype(BF16)
    return hi, lo


def _mod_kernel(c_ref, w_ref, b_ref, o_ref):
    c = c_ref[...]
    c_act = (c * jax.nn.sigmoid(c)).astype(BF16)
    o_ref[0] = _dot(c_act, w_ref[0].astype(BF16)) + b_ref[0]


def _modulation(c, w_mod, b_mod):
    n_layers, d, n_out = w_mod.shape
    bn = c.shape[0]
    tn = 1536
    return pl.pallas_call(
        _mod_kernel,
        grid=(n_layers, n_out // tn),
        in_specs=[
            pl.BlockSpec((bn, d), lambda l, j: (0, 0)),
            pl.BlockSpec((1, d, tn), lambda l, j: (l, 0, j)),
            pl.BlockSpec((1, 1, tn), lambda l, j: (l, 0, j)),
        ],
        out_specs=pl.BlockSpec((1, bn, tn), lambda l, j: (l, 0, j)),
        out_shape=jax.ShapeDtypeStruct((n_layers, bn, n_out), F32),
        compiler_params=pltpu.CompilerParams(
            dimension_semantics=("arbitrary", "arbitrary"), vmem_limit_bytes=VMEM_LIMIT),
        name="modulation",
    )(c, w_mod, b_mod.reshape(n_layers, 1, n_out))


def _log_sigmoid(x):
    return jnp.minimum(x, 0.0) - jnp.log1p(jnp.exp(-jnp.abs(x)))


def _inproj_kernel(x_ref, sc_ref, sh_ref, wmain_ref, wr_ref, wgk_ref, bgk_ref,
                   u_ref, g_ref, v_ref, qf_ref, kf_ref, df_ref, qb_ref, kb_ref, db_ref,
                   decf_ref, decb_ref):
    x = x_ref[0]
    ms = jnp.mean(x * x, axis=-1, keepdims=True)
    h = x * lax.rsqrt(ms + EPS) * (1.0 + sc_ref[0]) + sh_ref[0]
    hb = h.astype(BF16)
    z = _dot(hb, wmain_ref[...])
    r = _dot(hb, wr_ref[...]).astype(BF16)
    u_ref[0] = z[:, 0:D_POOL]
    q = z[:, D_POOL:D_POOL + D_QK] * (GLA_DK ** -0.5)
    k = z[:, D_POOL + D_QK:D_POOL + 2 * D_QK]
    v_ref[0] = z[:, D_POOL + 2 * D_QK:D_POOL + 2 * D_QK + D_GLA_V].astype(BF16)
    g_ref[0] = z[:, D_POOL + 2 * D_QK + D_GLA_V:D_MAIN]

    tile = x.shape[0]
    n_chunks = tile // GLA_CHUNK
    row = lax.broadcasted_iota(I32, (CUMSUM_GROUP, CUMSUM_GROUP), 0)
    col = lax.broadcasted_iota(I32, (CUMSUM_GROUP, CUMSUM_GROUP), 1)
    same_chunk = (row // GLA_CHUNK) == (col // GLA_CHUNK)
    prefix = jnp.where(same_chunk & (col <= row), 1.0, 0.0).astype(BF16)
    suffix = jnp.where(same_chunk & (col >= row), 1.0, 0.0).astype(BF16)

    def direction(idx, tri, last_row, q_out, k_out, d_out, dec_out):
        gate = _dot(r, wgk_ref[idx]) + bgk_ref[idx]
        loga = _log_sigmoid(gate) / GATE_NORM
        parts = []
        for s in range(tile // CUMSUM_GROUP):
            hi, lo = _split_bf16(loga[s * CUMSUM_GROUP:(s + 1) * CUMSUM_GROUP])
            parts.append(_dot(tri, hi) + _dot(tri, lo))
        b = jnp.concatenate(parts, axis=0)
        b3 = b.reshape(n_chunks, GLA_CHUNK, D_QK)
        total = b3[:, last_row:last_row + 1, :]
        q_out[0] = (q * jnp.exp(b)).astype(BF16)
        k_out[0] = (k * jnp.exp(-b)).astype(BF16)
        d_out[0] = (k * jnp.exp(total - b3).reshape(tile, D_QK)).astype(BF16)
        dec_out[0] = jnp.exp(total).reshape(n_chunks, D_QK)

    direction(0, prefix, GLA_CHUNK - 1, qf_ref, kf_ref, df_ref, decf_ref)
    direction(1, suffix, 0, qb_ref, kb_ref, db_ref, decb_ref)


def _inproj(x, sc, sh, w_main, w_r, w_gk, b_gk):
    bn, sn, d = x.shape
    ts = SEQ_TILE
    n_chunks = ts // GLA_CHUNK
    tok = lambda width: pl.BlockSpec((1, ts, width), lambda b, j: (b, j, 0))
    per_batch = pl.BlockSpec((1, 1, d), lambda b, j: (b, 0, 0))
    full = lambda a: pl.BlockSpec(a.shape, lambda b, j: (0,) * a.ndim)
    dec_spec = pl.BlockSpec((1, n_chunks, D_QK), lambda b, j: (b, j, 0))
    tok_shape = lambda width, dt: jax.ShapeDtypeStruct((bn, sn, width), dt)
    dec_shape = jax.ShapeDtypeStruct((bn, sn // GLA_CHUNK, D_QK), F32)
    return pl.pallas_call(
        _inproj_kernel,
        grid=(bn, sn // ts),
        in_specs=[tok(d), per_batch, per_batch, full(w_main), full(w_r), full(w_gk), full(b_gk)],
        out_specs=[tok(D_POOL), tok(D_GLA_V), tok(D_GLA_V)] + [tok(D_QK)] * 6 + [dec_spec, dec_spec],
        out_shape=[tok_shape(D_POOL, F32), tok_shape(D_GLA_V, F32), tok_shape(D_GLA_V, BF16)]
        + [tok_shape(D_QK, BF16)] * 6 + [dec_shape, dec_shape],
        compiler_params=pltpu.CompilerParams(
            dimension_semantics=("arbitrary", "arbitrary"), vmem_limit_bytes=VMEM_LIMIT),
        name="inproj",
    )(x, sc, sh, w_main, w_r, w_gk, b_gk)


def _gla_kernel(qf_ref, kf_ref, df_ref, qb_ref, kb_ref, db_ref, v_ref, decf_ref, decb_ref,
                g_ref, ng_ref, y_ref, o_acc, state):
    sn = v_ref.shape[1]
    n_chunks = sn // GLA_CHUNK
    lane_head = lax.broadcasted_iota(I32, (GLA_CHUNK, D_QK), 1) // GLA_DK
    head_masks = [lane_head == hd for hd in range(GLA_HEADS)]
    state_blocks = (lax.broadcasted_iota(I32, (D_GLA_V, D_QK), 0) // GLA_DV
                    == lax.broadcasted_iota(I32, (D_GLA_V, D_QK), 1) // GLA_DK)
    ri = lax.broadcasted_iota(I32, (GLA_CHUNK, GLA_CHUNK), 0)
    ci = lax.broadcasted_iota(I32, (GLA_CHUNK, GLA_CHUNK), 1)
    causal = ci <= ri
    anti = ci > ri

    def chunk_out(n, q_ref, k_ref, d_ref, dec_ref, mask):
        rows = pl.ds(pl.multiple_of(n * GLA_CHUNK, GLA_CHUNK), GLA_CHUNK)
        qe = q_ref[0, rows, :]
        ke = k_ref[0, rows, :]
        kd = d_ref[0, rows, :]
        vv = v_ref[0, rows, :]
        q_heads = jnp.concatenate([jnp.where(m, qe, jnp.zeros_like(qe)) for m in head_masks], axis=0)
        att_all = lax.dot_general(q_heads, ke, NT_DIMS, preferred_element_type=F32)
        outs = []
        for hd in range(GLA_HEADS):
            att = jnp.where(mask, att_all[hd * GLA_CHUNK:(hd + 1) * GLA_CHUNK], 0.0).astype(BF16)
            outs.append(_dot(att, vv[:, hd * GLA_DV:(hd + 1) * GLA_DV]))
        st = state[...]
        st_heads = jnp.where(state_blocks, st, 0.0).astype(BF16)
        inter = lax.dot_general(qe, st_heads, NT_DIMS, preferred_element_type=F32)
        upd = lax.dot_general(vv, kd, TN_DIMS, preferred_element_type=F32)
        state[...] = st * dec_ref[0, pl.ds(n, 1), :] + upd
        return jnp.concatenate(outs, axis=1) + inter, rows

    state[...] = jnp.zeros_like(state)

    def fwd_body(n, carry):
        o, rows = chunk_out(n, qf_ref, kf_ref, df_ref, decf_ref, causal)
        o_acc[rows, :] = o
        return carry

    lax.fori_loop(0, n_chunks, fwd_body, 0)

    state[...] = jnp.zeros_like(state)
    norm_g = ng_ref[...]

    def bwd_body(i, carry):
        n = n_chunks - 1 - i
        o, rows = chunk_out(n, qb_ref, kb_ref, db_ref, decb_ref, anti)
        o = o + o_acc[rows, :]
        gate = g_ref[0, rows, :]
        gate = gate * jax.nn.sigmoid(gate)
        outs = []
        for hd in range(GLA_HEADS):
            oh = o[:, hd * GLA_DV:(hd + 1) * GLA_DV]
            oh = oh * lax.rsqrt(jnp.mean(oh * oh, axis=-1, keepdims=True) + EPS) * norm_g
            outs.append(oh)
        y_ref[0, rows, :] = (jnp.concatenate(outs, axis=1) * gate).astype(BF16)
        return carry

    lax.fori_loop(0, n_chunks, bwd_body, 0)


def _gla(qf, kf, df, qb, kb, db, v, decf, decb, g, norm_g):
    bn, sn, _ = v.shape
    seq = lambda a: pl.BlockSpec((1,) + a.shape[1:], lambda b: (b, 0, 0))
    args = (qf, kf, df, qb, kb, db, v, decf, decb, g)
    return pl.pallas_call(
        _gla_kernel,
        grid=(bn,),
        in_specs=[seq(a) for a in args] + [pl.BlockSpec((1, GLA_DV), lambda b: (0, 0))],
        out_specs=pl.BlockSpec((1, sn, D_GLA_V), lambda b: (b, 0, 0)),
        out_shape=jax.ShapeDtypeStruct((bn, sn, D_GLA_V), BF16),
        scratch_shapes=[pltpu.VMEM((sn, D_GLA_V), F32), pltpu.VMEM((D_GLA_V, D_QK), F32)],
        compiler_params=pltpu.CompilerParams(
            dimension_semantics=("arbitrary",), vmem_limit_bytes=VMEM_LIMIT),
        name="gla",
    )(*args, norm_g.reshape(1, GLA_DV))


def _mix_kernel(u_ref, up_ref, un_ref, yg_ref, x_ref, wp_ref, ps_ref, wo_ref, g1_ref, sc_ref, sh_ref,
                wrh_ref, wrl_ref, br_ref,
                xo_ref, h_ref, pos_ref, rw_ref, cnt_ref, *, seq_len):
    j = pl.program_id(1)
    ts = u_ref.shape[1]

    prev_ok = jnp.where(j > 0, 1.0, 0.0)
    next_ok = jnp.where(j < pl.num_programs(1) - 1, 1.0, 0.0)
    ext = jnp.concatenate([up_ref[0, 0] * prev_ok, u_ref[0], un_ref[0, 0] * next_ok], axis=0)
    pos = j * ts + lax.broadcasted_iota(I32, (ts, 1), 0)
    pooled = []
    for gi, w in enumerate(POOL_WINDOWS):
        e = ext[:, gi * POOL_GROUP:(gi + 1) * POOL_GROUP]
        acc = e[POOL_HALO - w // 2:POOL_HALO - w // 2 + ts]
        for dlt in range(-(w // 2) + 1, w // 2):
            acc = acc + e[POOL_HALO + dlt:POOL_HALO + dlt + ts]
        lo = jnp.clip(pos - w // 2, 0, seq_len - 1)
        hi = jnp.clip(pos - w // 2 + w - 1, 0, seq_len - 1)
        cnt = (hi - lo + 1).astype(F32)
        pooled.append(acc / cnt - e[POOL_HALO:POOL_HALO + ts])
    y_pool = [(_dot(pooled[gi].astype(BF16), wp_ref[gi])
               * ps_ref[:, gi * POOL_GROUP:(gi + 1) * POOL_GROUP]).astype(BF16)
              for gi in range(len(POOL_WINDOWS))]
    y = jnp.concatenate(y_pool + [yg_ref[0]], axis=1)
    x = x_ref[0] + g1_ref[0] * _dot(y, wo_ref[...])
    xo_ref[0] = x

    ms = jnp.mean(x * x, axis=-1, keepdims=True)
    h = x * lax.rsqrt(ms + EPS) * (1.0 + sc_ref[0]) + sh_ref[0]
    h_ref[0] = h.astype(BF16)

    h_hi, h_lo = _split_bf16(h)
    logits = (lax.dot_general(wrh_ref[...], h_hi, NT_DIMS, preferred_element_type=F32)
              + lax.dot_general(wrh_ref[...], h_lo, NT_DIMS, preferred_element_type=F32)
              + lax.dot_general(wrl_ref[...], h_hi, NT_DIMS, preferred_element_type=F32))
    score = jax.nn.sigmoid(logits)
    sel = (score + br_ref[...]).reshape(N_GROUPS, EXPERTS_PER_GROUP, ts)
    local = lax.broadcasted_iota(I32, sel.shape, 1)
    big = EXPERTS_PER_GROUP
    m1 = jnp.max(sel, axis=1, keepdims=True)
    i1 = jnp.min(jnp.where(sel == m1, local, big), axis=1, keepdims=True)
    rest = jnp.where(local == i1, -jnp.inf, sel)
    m2 = jnp.max(rest, axis=1, keepdims=True)
    i2 = jnp.min(jnp.where(rest == m2, local, big), axis=1, keepdims=True)
    gscore = m1 + m2
    gid = lax.broadcasted_iota(I32, gscore.shape, 0)
    gbest = jnp.max(gscore, axis=0, keepdims=True)
    gsel = jnp.min(jnp.where(gscore == gbest, gid, N_GROUPS), axis=0, keepdims=True)
    in_group = gid == gsel
    pick0 = (in_group & (local == i1)).reshape(N_EXPERTS, ts)
    pick1 = (in_group & (local == i2)).reshape(N_EXPERTS, ts)
    s0 = jnp.sum(jnp.where(pick0, score, 0.0), axis=0, keepdims=True)
    s1 = jnp.sum(jnp.where(pick1, score, 0.0), axis=0, keepdims=True)
    rw_ref[0:1, :] = s0 / (s0 + s1)
    rw_ref[1:2, :] = s1 / (s0 + s1)

    onehot = jnp.where(pick0 | pick1, 1.0, 0.0).astype(BF16)
    tr = lax.broadcasted_iota(I32, (ts, ts), 0)
    tc = lax.broadcasted_iota(I32, (ts, ts), 1)
    earlier_tok = jnp.where(tr < tc, 1.0, 0.0).astype(BF16)
    er = lax.broadcasted_iota(I32, (N_EXPERTS, N_EXPERTS), 0)
    ec = lax.broadcasted_iota(I32, (N_EXPERTS, N_EXPERTS), 1)
    lower_exp = jnp.where(ec < er, 1.0, 0.0).astype(BF16)
    same_before = _dot(onehot, earlier_tok)
    lower_total = jnp.sum(_dot(lower_exp, onehot), axis=1, keepdims=True)
    slot = same_before + lower_total
    p0 = jnp.sum(jnp.where(pick0, slot, 0.0), axis=0, keepdims=True)
    p1 = jnp.sum(jnp.where(pick1, slot, 0.0), axis=0, keepdims=True)
    pos_ref[0:1, :] = p0.astype(I32)
    pos_ref[1:2, :] = p1.astype(I32)
    cnt_ref[0] = jnp.sum(onehot.astype(F32), axis=1, keepdims=True)


def _mix(u, y_gla, x, w_pool, pool_scale, w_out, g1, sc2, sh2, wr_hi, wr_lo, b_router):
    bn, sn, d = x.shape
    ts = SEQ_TILE
    nt = sn // ts
    hpt = ts // POOL_HALO
    n_halo = sn // POOL_HALO
    u_halo = u.reshape(bn, n_halo, POOL_HALO, D_POOL)
    tok = lambda width: pl.BlockSpec((1, ts, width), lambda b, j: (b, j, 0))
    per_batch = pl.BlockSpec((1, 1, d), lambda b, j: (b, 0, 0))
    full = lambda a: pl.BlockSpec(a.shape, lambda b, j: (0,) * a.ndim)
    prev_halo = pl.BlockSpec((1, 1, POOL_HALO, D_POOL),
                             lambda b, j: (b, jnp.maximum(j * hpt - 1, 0), 0, 0))
    next_halo = pl.BlockSpec((1, 1, POOL_HALO, D_POOL),
                             lambda b, j: (b, jnp.minimum((j + 1) * hpt, n_halo - 1), 0, 0))
    lane_tok = lambda rows: pl.BlockSpec((rows, ts), lambda b, j: (0, b * nt + j))
    t = bn * sn
    return pl.pallas_call(
        functools.partial(_mix_kernel, seq_len=sn),
        grid=(bn, nt),
        in_specs=[tok(D_POOL), prev_halo, next_halo, tok(D_GLA_V), tok(d), full(w_pool), full(pool_scale),
                  full(w_out), per_batch, per_batch, per_batch, full(wr_hi), full(wr_lo), full(b_router)],
        out_specs=[tok(d), tok(d), lane_tok(2), lane_tok(2),
                   pl.BlockSpec((1, N_EXPERTS, 1), lambda b, j: (b * nt + j, 0, 0))],
        out_shape=[jax.ShapeDtypeStruct((bn, sn, d), F32), jax.ShapeDtypeStruct((bn, sn, d), BF16),
                   jax.ShapeDtypeStruct((2, t), I32), jax.ShapeDtypeStruct((2, t), F32),
                   jax.ShapeDtypeStruct((bn * nt, N_EXPERTS, 1), F32)],
        compiler_params=pltpu.CompilerParams(
            dimension_semantics=("arbitrary", "arbitrary"), vmem_limit_bytes=VMEM_LIMIT),
        name="mix_route",
    )(u, u_halo, u_halo, y_gla, x, w_pool, pool_scale, w_out, g1, sc2, sh2, wr_hi, wr_lo, b_router)


def _segment_copies(meta_ref, tile, make_copy, action):
    base = tile * META_PER_TILE
    for e in range(N_EXPERTS):
        n = meta_ref[base + e]
        first_slot = meta_ref[base + N_EXPERTS + e]
        first_row = meta_ref[base + 2 * N_EXPERTS + e]

        @pl.when(n > 0)
        def _():
            action(make_copy(first_slot, first_row, n))


def _start(copy):
    copy.start()


def _wait(copy):
    copy.wait()


def _dispatch_kernel(meta_ref, h_ref, pos_ref, zeros_ref, xs_ref, buf, sem):
    del zeros_ref
    i = pl.program_id(0)
    ts = h_ref.shape[0]

    def copy(first_slot, first_row, n):
        return pltpu.make_async_copy(buf.at[pl.ds(first_slot, n)], xs_ref.at[pl.ds(first_row, n)], sem)

    slot = lax.broadcasted_iota(I32, (2 * ts, ts), 0)
    perm = jnp.where((slot == pos_ref[0:1, :]) | (slot == pos_ref[1:2, :]), 1.0, 0.0).astype(BF16)
    rows = _dot(perm, h_ref[...])

    @pl.when(i > 0)
    def _():
        _segment_copies(meta_ref, i - 1, copy, _wait)

    for c in range(ROW_SUBLANES):
        buf[:, c, :] = rows[:, c * LANE:(c + 1) * LANE]
    _segment_copies(meta_ref, i, copy, _start)

    @pl.when(i == pl.num_programs(0) - 1)
    def _():
        _segment_copies(meta_ref, i, copy, _wait)


def _dispatch(meta, h, pos, n_rows_sorted):
    t, d = h.shape
    ts = SEQ_TILE
    zeros = jnp.zeros((n_rows_sorted, ROW_SUBLANES, LANE), F32)
    return pl.pallas_call(
        _dispatch_kernel,
        grid_spec=pltpu.PrefetchScalarGridSpec(
            num_scalar_prefetch=1,
            grid=(t // ts,),
            in_specs=[pl.BlockSpec((ts, d), lambda i, m: (i, 0)),
                      pl.BlockSpec((2, ts), lambda i, m: (0, i)),
                      pl.BlockSpec(memory_space=pl.ANY)],
            out_specs=pl.BlockSpec(memory_space=pl.ANY),
            scratch_shapes=[pltpu.VMEM((2 * ts, ROW_SUBLANES, LANE), F32), pltpu.SemaphoreType.DMA(())],
        ),
        out_shape=jax.ShapeDtypeStruct((n_rows_sorted, ROW_SUBLANES, LANE), F32),
        input_output_aliases={3: 0},
        compiler_params=pltpu.CompilerParams(
            dimension_semantics=("arbitrary",), vmem_limit_bytes=VMEM_LIMIT),
        name="dispatch",
    )(meta, h, pos, zeros)


def _ffn_kernel(blk_e_ref, n_used_ref, xs_ref, w1_ref, w3_ref, w2_ref, ys_ref):
    del blk_e_ref

    @pl.when(pl.program_id(0) < n_used_ref[0])
    def _():
        xb = jnp.concatenate([xs_ref[:, c, :] for c in range(ROW_SUBLANES)], axis=1).astype(BF16)
        a = _dot(xb, w1_ref[0, 0])
        hid = (a * jax.nn.sigmoid(a)) * _dot(xb, w3_ref[0, 0])
        y = _dot(hid.astype(BF16), w2_ref[0, 0])
        for c in range(ROW_SUBLANES):
            ys_ref[:, c, :] = y[:, c * LANE:(c + 1) * LANE]

    @pl.when(pl.program_id(0) >= n_used_ref[0])
    def _():
        ys_ref[...] = jnp.zeros_like(ys_ref)


def _ffn(layer, blk_e, n_used, xs, w1, w3, w2):
    rows = xs.shape[0]
    n_blocks = rows // MOE_BLOCK
    last = lambda i, n_used_ref: jnp.minimum(i, n_used_ref[0] - 1)
    row_spec = pl.BlockSpec((MOE_BLOCK, ROW_SUBLANES, LANE), lambda i, be, nu: (last(i, nu), 0, 0))
    w_spec = lambda a: pl.BlockSpec((1, 1) + a.shape[2:], lambda i, be, nu: (layer, be[last(i, nu)], 0, 0))
    return pl.pallas_call(
        _ffn_kernel,
        grid_spec=pltpu.PrefetchScalarGridSpec(
            num_scalar_prefetch=2,
            grid=(n_blocks,),
            in_specs=[row_spec, w_spec(w1), w_spec(w3), w_spec(w2)],
            out_specs=pl.BlockSpec((MOE_BLOCK, ROW_SUBLANES, LANE), lambda i, be, nu: (i, 0, 0)),
        ),
        out_shape=jax.ShapeDtypeStruct(xs.shape, F32),
        compiler_params=pltpu.CompilerParams(
            dimension_semantics=("arbitrary",), vmem_limit_bytes=VMEM_LIMIT),
        name="expert_ffn",
    )(blk_e, n_used, xs, w1, w3, w2)


def _combine_kernel(meta_ref, ys_ref, x_ref, pos_ref, rw_ref, posc_ref, g2_ref, fg_ref, o_ref, buf, sem,
                    *, final_norm):
    i = pl.program_id(0)
    ts = x_ref.shape[0]

    def fetch(tile, action):
        half = tile % 2

        def copy(first_slot, first_row, n):
            return pltpu.make_async_copy(ys_ref.at[pl.ds(first_row, n)],
                                         buf.at[half, pl.ds(first_slot, n)], sem.at[half])

        _segment_copies(meta_ref, tile, copy, action)

    @pl.when(i == 0)
    def _():
        fetch(i, _start)

    @pl.when(i + 1 < pl.num_programs(0))
    def _():
        fetch(i + 1, _start)

    fetch(i, _wait)
    cur = buf.at[i % 2]
    rows = jnp.concatenate([cur[:, c, :] for c in range(ROW_SUBLANES)], axis=1)
    slot = lax.broadcasted_iota(I32, (2 * ts, ts), 0)
    w_rows = jnp.sum(jnp.where(slot == pos_ref[0:1, :], rw_ref[0:1, :], 0.0)
                     + jnp.where(slot == pos_ref[1:2, :], rw_ref[1:2, :], 0.0),
                     axis=1, keepdims=True)
    weighted = (rows * w_rows).astype(BF16)
    slot_lane = lax.broadcasted_iota(I32, (ts, 2 * ts), 1)
    posc = posc_ref[...]
    unperm = jnp.where((slot_lane == posc[:, 0:1]) | (slot_lane == posc[:, 1:2]), 1.0, 0.0).astype(BF16)
    x = x_ref[...] + g2_ref[0] * _dot(unperm, weighted)
    if final_norm:
        ms = jnp.mean(x * x, axis=-1, keepdims=True)
        x = x * lax.rsqrt(ms + EPS) * fg_ref[...]
    o_ref[...] = x


def _combine(meta, ys, x, pos, route_w, g2, final_g, seq_len, final_norm):
    t, d = x.shape
    ts = SEQ_TILE
    tiles_per_seq = seq_len // ts
    lane_tok = pl.BlockSpec((2, ts), lambda i, m: (0, i))
    return pl.pallas_call(
        functools.partial(_combine_kernel, final_norm=final_norm),
        grid_spec=pltpu.PrefetchScalarGridSpec(
            num_scalar_prefetch=1,
            grid=(t // ts,),
            in_specs=[pl.BlockSpec(memory_space=pl.ANY),
                      pl.BlockSpec((ts, d), lambda i, m: (i, 0)),
                      lane_tok, lane_tok,
                      pl.BlockSpec((ts, 2), lambda i, m: (i, 0)),
                      pl.BlockSpec((1, 1, d), lambda i, m: (i // tiles_per_seq, 0, 0)),
                      pl.BlockSpec((1, d), lambda i, m: (0, 0))],
            out_specs=pl.BlockSpec((ts, d), lambda i, m: (i, 0)),
            scratch_shapes=[pltpu.VMEM((2, 2 * ts, ROW_SUBLANES, LANE), F32), pltpu.SemaphoreType.DMA((2,))],
        ),
        out_shape=jax.ShapeDtypeStruct((t, d), F32),
        compiler_params=pltpu.CompilerParams(
            dimension_semantics=("arbitrary",), vmem_limit_bytes=VMEM_LIMIT),
        name="combine",
    )(meta, ys, x, pos, route_w, pos.T, g2, final_g.reshape(1, d))


def _dispatch_plan(tile_counts, n_assign):
    cnt = tile_counts.astype(I32)
    totals = jnp.sum(cnt, axis=0)
    padded = (totals + MOE_BLOCK - 1) // MOE_BLOCK * MOE_BLOCK
    pad_end = jnp.cumsum(padded)
    first_row = (pad_end - padded)[None, :] + jnp.cumsum(cnt, axis=0) - cnt
    first_slot = jnp.cumsum(cnt, axis=1) - cnt
    meta = jnp.stack([cnt, first_slot, first_row], axis=1).reshape(-1)
    n_blocks = n_assign // MOE_BLOCK + N_EXPERTS
    blk_start = jnp.arange(n_blocks, dtype=I32) * MOE_BLOCK
    blk_e = jnp.minimum(jnp.sum((pad_end[None, :] <= blk_start[:, None]).astype(I32), axis=1), N_EXPERTS - 1)
    n_used = (pad_end[-1] // MOE_BLOCK).astype(I32).reshape(1)
    return meta.astype(I32), blk_e.astype(I32), n_used, n_blocks * MOE_BLOCK


def kernel(x, c, w_mod, b_mod, w_in, w_pool, pool_scale, w_gk_up, b_gk, gla_norm_g, w_out,
           w_router, b_router, w1, w3, w2, final_g):
    bn, sn, d = x.shape
    t = bn * sn
    mod = _modulation(c, w_mod, b_mod).reshape(DEPTH, bn, 6, 1, d)

    w_main = w_in[:, :, :D_MAIN].astype(BF16)
    w_r = w_in[:, :, D_MAIN:].astype(BF16)
    zero_rank = jnp.zeros_like(w_gk_up[:, 0])
    w_gk = jnp.stack([jnp.concatenate([w_gk_up[:, 0], zero_rank], axis=1),
                      jnp.concatenate([zero_rank, w_gk_up[:, 1]], axis=1)], axis=1).astype(BF16)
    b_gk3 = b_gk.reshape(DEPTH, 2, 1, D_QK)
    w_pool_b = w_pool.astype(BF16)
    w_out_b = w_out.astype(BF16)
    wr_t = w_router.T
    wr_hi = wr_t.astype(BF16)
    wr_lo = (wr_t - wr_hi.astype(F32)).astype(BF16)
    br = b_router.reshape(N_EXPERTS, 1)
    w1_b, w3_b, w2_b = w1.astype(BF16), w3.astype(BF16), w2.astype(BF16)

    for l in range(DEPTH):
        sh1, sc1, g1, sh2, sc2, g2 = (mod[l, :, i] for i in range(6))
        u, g, v, qf, kf, df, qb, kb, db, decf, decb = _inproj(
            x, sc1, sh1, w_main[l], w_r[l], w_gk[l], b_gk3[l])
        y_gla = _gla(qf, kf, df, qb, kb, db, v, decf, decb, g, gla_norm_g[l])
        x, h, pos, route_w, tile_counts = _mix(
            u, y_gla, x, w_pool_b[l], pool_scale[l].reshape(1, D_POOL), w_out_b[l], g1, sc2, sh2,
            wr_hi, wr_lo, br)
        meta, blk_e, n_used, n_rows_sorted = _dispatch_plan(tile_counts[:, :, 0], 2 * t)
        xs = _dispatch(meta, h.reshape(t, d), pos, n_rows_sorted)
        ys = _ffn(l, blk_e, n_used, xs, w1_b, w3_b, w2_b)
        x = _combine(meta, ys, x.reshape(t, d), pos, route_w, g2, final_g, sn, l == DEPTH - 1)
        x = x.reshape(bn, sn, d)
    return x
```

```python
import functools

import jax
import jax.numpy as jnp
from jax import lax
from jax.experimental import pallas as pl
from jax.experimental.pallas import tpu as pltpu

F32 = jnp.float32
BF16 = jnp.bfloat16
I32 = jnp.int32
U32 = jnp.uint32

D_MODEL = 1024
DEPTH = 2
D_POOL = 512
POOL_WINDOWS = (2, 4, 8, 16)
POOL_GROUP = 128
POOL_HALO = 16
D_GLA_V = 512
GLA_HEADS = 4
GLA_DK = 64
GLA_DV = 128
D_QK = GLA_HEADS * GLA_DK
GLA_RANK = 16
GATE_NORM = 16.0
GLA_CHUNK = 64
GLA_OUT_ROWS = 256
D_MAIN = D_POOL + 2 * D_QK + 2 * D_GLA_V
N_EXPERTS = 32
N_GROUPS = 4
EXPERTS_PER_GROUP = 8
D_FF = 512
EPS = 1e-6

SEQ_TILE = 512
CUMSUM_GROUP = 256
MOE_BLOCK = 512
LANE = 128
SEG_ALIGN = 8
TILE_SLOTS = 2 * SEQ_TILE + 256
D_PACK = D_MODEL // 2
META_PER_TILE = 3 * N_EXPERTS
assert TILE_SLOTS >= 2 * SEQ_TILE + N_EXPERTS * (SEG_ALIGN - 1) and 2 * SEQ_TILE // SEG_ALIGN <= 256
VMEM_LIMIT = 56 * 1024 * 1024

NT_DIMS = (((1,), (1,)), ((), ()))
TN_DIMS = (((0,), (0,)), ((), ()))


def _dot(a, b):
    return jnp.dot(a, b, preferred_element_type=F32)


def _split_bf16(a):
    hi = a.astype(BF16)
    lo = (a - hi.astype(F32)).astype(BF16)
    return hi, lo


def _mod_kernel(c_ref, w_ref, b_ref, o_ref):
    c = c_ref[...]
    c_act = (c * jax.nn.sigmoid(c)).astype(BF16)
    o_ref[0] = _dot(c_act, w_ref[0].astype(BF16)) + b_ref[0]


def _modulation(c, w_mod, b_mod):
    n_layers, d, n_out = w_mod.shape
    bn = c.shape[0]
    tn = 1536
    return pl.pallas_call(
        _mod_kernel,
        grid=(n_layers, n_out // tn),
        in_specs=[
            pl.BlockSpec((bn, d), lambda l, j: (0, 0)),
            pl.BlockSpec((1, d, tn), lambda l, j: (l, 0, j)),
            pl.BlockSpec((1, 1, tn), lambda l, j: (l, 0, j)),
        ],
        out_specs=pl.BlockSpec((1, bn, tn), lambda l, j: (l, 0, j)),
        out_shape=jax.ShapeDtypeStruct((n_layers, bn, n_out), F32),
        compiler_params=pltpu.CompilerParams(
            dimension_semantics=("arbitrary", "arbitrary"), vmem_limit_bytes=VMEM_LIMIT),
        name="modulation",
    )(c, w_mod, b_mod.reshape(n_layers, 1, n_out))


def _log_sigmoid(x):
    return jnp.minimum(x, 0.0) - jnp.log1p(jnp.exp(-jnp.abs(x)))


def _inproj_kernel(x_ref, sc_ref, sh_ref, wmain_ref, wr_ref, wgk_ref, bgk_ref,
                   u_ref, g_ref, v_ref, qf_ref, kf_ref, df_ref, qb_ref, kb_ref, db_ref,
                   decf_ref, decb_ref):
    x = x_ref[0]
    ms = jnp.mean(x * x, axis=-1, keepdims=True)
    h = x * lax.rsqrt(ms + EPS) * (1.0 + sc_ref[0]) + sh_ref[0]
    hb = h.astype(BF16)
    z = _dot(hb, wmain_ref[...])
    r = _dot(hb, wr_ref[...]).astype(BF16)
    u_ref[0] = z[:, 0:D_POOL]
    q = z[:, D_POOL:D_POOL + D_QK] * (GLA_DK ** -0.5)
    k = z[:, D_POOL + D_QK:D_POOL + 2 * D_QK]
    v_ref[0] = z[:, D_POOL + 2 * D_QK:D_POOL + 2 * D_QK + D_GLA_V].astype(BF16)
    g_ref[0] = z[:, D_POOL + 2 * D_QK + D_GLA_V:D_MAIN]

    tile = x.shape[0]
    n_chunks = tile // GLA_CHUNK
    row = lax.broadcasted_iota(I32, (CUMSUM_GROUP, CUMSUM_GROUP), 0)
    col = lax.broadcasted_iota(I32, (CUMSUM_GROUP, CUMSUM_GROUP), 1)
    same_chunk = (row // GLA_CHUNK) == (col // GLA_CHUNK)
    prefix = jnp.where(same_chunk & (col <= row), 1.0, 0.0).astype(BF16)
    suffix = jnp.where(same_chunk & (col >= row), 1.0, 0.0).astype(BF16)

    def direction(idx, tri, last_row, q_out, k_out, d_out, dec_out):
        gate = _dot(r, wgk_ref[idx]) + bgk_ref[idx]
        loga = _log_sigmoid(gate) / GATE_NORM
        parts = []
        for s in range(tile // CUMSUM_GROUP):
            hi, lo = _split_bf16(loga[s * CUMSUM_GROUP:(s + 1) * CUMSUM_GROUP])
            parts.append(_dot(tri, hi) + _dot(tri, lo))
        b = jnp.concatenate(parts, axis=0)
        b3 = b.reshape(n_chunks, GLA_CHUNK, D_QK)
        total = b3[:, last_row:last_row + 1, :]
        q_out[0] = (q * jnp.exp(b)).astype(BF16)
        k_out[0] = (k * jnp.exp(-b)).astype(BF16)
        d_out[0] = (k * jnp.exp(total - b3).reshape(tile, D_QK)).astype(BF16)
        dec_out[0] = jnp.exp(total).reshape(n_chunks, D_QK)

    direction(0, prefix, GLA_CHUNK - 1, qf_ref, kf_ref, df_ref, decf_ref)
    direction(1, suffix, 0, qb_ref, kb_ref, db_ref, decb_ref)


def _inproj(x, sc, sh, w_main, w_r, w_gk, b_gk):
    bn, sn, d = x.shape
    ts = SEQ_TILE
    n_chunks = ts // GLA_CHUNK
    tok = lambda width: pl.BlockSpec((1, ts, width), lambda b, j: (b, j, 0))
    per_batch = pl.BlockSpec((1, 1, d), lambda b, j: (b, 0, 0))
    full = lambda a: pl.BlockSpec(a.shape, lambda b, j: (0,) * a.ndim)
    dec_spec = pl.BlockSpec((1, n_chunks, D_QK), lambda b, j: (b, j, 0))
    tok_shape = lambda width, dt: jax.ShapeDtypeStruct((bn, sn, width), dt)
    dec_shape = jax.ShapeDtypeStruct((bn, sn // GLA_CHUNK, D_QK), F32)
    return pl.pallas_call(
        _inproj_kernel,
        grid=(bn, sn // ts),
        in_specs=[tok(d), per_batch, per_batch, full(w_main), full(w_r), full(w_gk), full(b_gk)],
        out_specs=[tok(D_POOL), tok(D_GLA_V), tok(D_GLA_V)] + [tok(D_QK)] * 6 + [dec_spec, dec_spec],
        out_shape=[tok_shape(D_POOL, F32), tok_shape(D_GLA_V, F32), tok_shape(D_GLA_V, BF16)]
        + [tok_shape(D_QK, BF16)] * 6 + [dec_shape, dec_shape],
        compiler_params=pltpu.CompilerParams(
            dimension_semantics=("arbitrary", "arbitrary"), vmem_limit_bytes=VMEM_LIMIT),
        name="inproj",
    )(x, sc, sh, w_main, w_r, w_gk, b_gk)


def _gla_kernel(qf_ref, kf_ref, df_ref, qb_ref, kb_ref, db_ref, v_ref, decf_ref, decb_ref,
                g_ref, ng_ref, y_ref, of_acc, ob_acc, state_f, state_b):
    sn = v_ref.shape[1]
    n_chunks = sn // GLA_CHUNK
    lane_head = lax.broadcasted_iota(I32, (GLA_CHUNK, D_QK), 1) // GLA_DK
    head_masks = [lane_head == hd for hd in range(GLA_HEADS)]
    state_blocks = (lax.broadcasted_iota(I32, (D_GLA_V, D_QK), 0) // GLA_DV
                    == lax.broadcasted_iota(I32, (D_GLA_V, D_QK), 1) // GLA_DK)
    ri = lax.broadcasted_iota(I32, (GLA_CHUNK, GLA_CHUNK), 0)
    ci = lax.broadcasted_iota(I32, (GLA_CHUNK, GLA_CHUNK), 1)
    causal = ci <= ri
    anti = ci > ri

    def chunk_out(n, q_ref, k_ref, d_ref, dec_ref, state, mask, o_acc):
        rows = pl.ds(pl.multiple_of(n * GLA_CHUNK, GLA_CHUNK), GLA_CHUNK)
        qe = q_ref[0, rows, :]
        ke = k_ref[0, rows, :]
        kd = d_ref[0, rows, :]
        vv = v_ref[0, rows, :]
        q_heads = jnp.concatenate([jnp.where(m, qe, jnp.zeros_like(qe)) for m in head_masks], axis=0)
        att_all = lax.dot_general(q_heads, ke, NT_DIMS, preferred_element_type=F32)
        outs = []
        for hd in range(GLA_HEADS):
            att = jnp.where(mask, att_all[hd * GLA_CHUNK:(hd + 1) * GLA_CHUNK], 0.0).astype(BF16)
            outs.append(_dot(att, vv[:, hd * GLA_DV:(hd + 1) * GLA_DV]))
        st = state[...]
        st_heads = jnp.where(state_blocks, st, 0.0).astype(BF16)
        inter = lax.dot_general(qe, st_heads, NT_DIMS, preferred_element_type=F32)
        upd = lax.dot_general(vv, kd, TN_DIMS, preferred_element_type=F32)
        state[...] = st * dec_ref[0, pl.ds(n, 1), :] + upd
        o_acc[rows, :] = jnp.concatenate(outs, axis=1) + inter

    state_f[...] = jnp.zeros_like(state_f)
    state_b[...] = jnp.zeros_like(state_b)

    def scan_body(i, carry):
        chunk_out(i, qf_ref, kf_ref, df_ref, decf_ref, state_f, causal, of_acc)
        chunk_out(n_chunks - 1 - i, qb_ref, kb_ref, db_ref, decb_ref, state_b, anti, ob_acc)
        return carry

    lax.fori_loop(0, n_chunks, scan_body, 0, unroll=2)

    norm_g = ng_ref[...]

    def out_body(s, carry):
        rows = pl.ds(pl.multiple_of(s * GLA_OUT_ROWS, GLA_OUT_ROWS), GLA_OUT_ROWS)
        o = of_acc[rows, :] + ob_acc[rows, :]
        gate = g_ref[0, rows, :]
        gate = gate * jax.nn.sigmoid(gate)
        outs = []
        for hd in range(GLA_HEADS):
            oh = o[:, hd * GLA_DV:(hd + 1) * GLA_DV]
            oh = oh * lax.rsqrt(jnp.mean(oh * oh, axis=-1, keepdims=True) + EPS) * norm_g
            outs.append(oh)
        y_ref[0, rows, :] = (jnp.concatenate(outs, axis=1) * gate).astype(BF16)
        return carry

    lax.fori_loop(0, sn // GLA_OUT_ROWS, out_body, 0)


def _gla(qf, kf, df, qb, kb, db, v, decf, decb, g, norm_g):
    bn, sn, _ = v.shape
    seq = lambda a: pl.BlockSpec((1,) + a.shape[1:], lambda b: (b, 0, 0))
    args = (qf, kf, df, qb, kb, db, v, decf, decb, g)
    return pl.pallas_call(
        _gla_kernel,
        grid=(bn,),
        in_specs=[seq(a) for a in args] + [pl.BlockSpec((1, GLA_DV), lambda b: (0, 0))],
        out_specs=pl.BlockSpec((1, sn, D_GLA_V), lambda b: (b, 0, 0)),
        out_shape=jax.ShapeDtypeStruct((bn, sn, D_GLA_V), BF16),
        scratch_shapes=[pltpu.VMEM((sn, D_GLA_V), F32), pltpu.VMEM((sn, D_GLA_V), F32),
                        pltpu.VMEM((D_GLA_V, D_QK), F32), pltpu.VMEM((D_GLA_V, D_QK), F32)],
        compiler_params=pltpu.CompilerParams(
            dimension_semantics=("arbitrary",), vmem_limit_bytes=VMEM_LIMIT),
        name="gla",
    )(*args, norm_g.reshape(1, GLA_DV))


def _mix_kernel(u_ref, up_ref, un_ref, yg_ref, x_ref, wp_ref, ps_ref, wo_ref, g1_ref, sc_ref, sh_ref,
                wrh_ref, wrl_ref, br_ref,
                xo_ref, h_ref, pos_ref, rw_ref, cnt_ref, *, seq_len):
    j = pl.program_id(1)
    ts = u_ref.shape[1]

    prev_ok = jnp.where(j > 0, 1.0, 0.0)
    next_ok = jnp.where(j < pl.num_programs(1) - 1, 1.0, 0.0)
    ext = jnp.concatenate([up_ref[0, 0] * prev_ok, u_ref[0], un_ref[0, 0] * next_ok], axis=0)
    pos = j * ts + lax.broadcasted_iota(I32, (ts, 1), 0)
    pooled = []
    for gi, w in enumerate(POOL_WINDOWS):
        e = ext[:, gi * POOL_GROUP:(gi + 1) * POOL_GROUP]
        acc = e[POOL_HALO - w // 2:POOL_HALO - w // 2 + ts]
        for dlt in range(-(w // 2) + 1, w // 2):
            acc = acc + e[POOL_HALO + dlt:POOL_HALO + dlt + ts]
        lo = jnp.clip(pos - w // 2, 0, seq_len - 1)
        hi = jnp.clip(pos - w // 2 + w - 1, 0, seq_len - 1)
        cnt = (hi - lo + 1).astype(F32)
        pooled.append(acc / cnt - e[POOL_HALO:POOL_HALO + ts])
    y_pool = [(_dot(pooled[gi].astype(BF16), wp_ref[gi])
               * ps_ref[:, gi * POOL_GROUP:(gi + 1) * POOL_GROUP]).astype(BF16)
              for gi in range(len(POOL_WINDOWS))]
    y = jnp.concatenate(y_pool + [yg_ref[0]], axis=1)
    x = x_ref[0] + g1_ref[0] * _dot(y, wo_ref[...])
    xo_ref[0] = x

    ms = jnp.mean(x * x, axis=-1, keepdims=True)
    h = x * lax.rsqrt(ms + EPS) * (1.0 + sc_ref[0]) + sh_ref[0]
    h_ref[0] = h.astype(BF16)

    h_hi, h_lo = _split_bf16(h)
    logits = (lax.dot_general(wrh_ref[...], h_hi, NT_DIMS, preferred_element_type=F32)
              + lax.dot_general(wrh_ref[...], h_lo, NT_DIMS, preferred_element_type=F32)
              + lax.dot_general(wrl_ref[...], h_hi, NT_DIMS, preferred_element_type=F32))
    score = jax.nn.sigmoid(logits)
    sel = (score + br_ref[...]).reshape(N_GROUPS, EXPERTS_PER_GROUP, ts)
    local = lax.broadcasted_iota(I32, sel.shape, 1)
    big = EXPERTS_PER_GROUP
    m1 = jnp.max(sel, axis=1, keepdims=True)
    i1 = jnp.min(jnp.where(sel == m1, local, big), axis=1, keepdims=True)
    rest = jnp.where(local == i1, -jnp.inf, sel)
    m2 = jnp.max(rest, axis=1, keepdims=True)
    i2 = jnp.min(jnp.where(rest == m2, local, big), axis=1, keepdims=True)
    gscore = m1 + m2
    gid = lax.broadcasted_iota(I32, gscore.shape, 0)
    gbest = jnp.max(gscore, axis=0, keepdims=True)
    gsel = jnp.min(jnp.where(gscore == gbest, gid, N_GROUPS), axis=0, keepdims=True)
    in_group = gid == gsel
    pick0 = (in_group & (local == i1)).reshape(N_EXPERTS, ts)
    pick1 = (in_group & (local == i2)).reshape(N_EXPERTS, ts)
    s0 = jnp.sum(jnp.where(pick0, score, 0.0), axis=0, keepdims=True)
    s1 = jnp.sum(jnp.where(pick1, score, 0.0), axis=0, keepdims=True)
    rw_ref[0:1, :] = s0 / (s0 + s1)
    rw_ref[1:2, :] = s1 / (s0 + s1)

    onehot = jnp.where(pick0 | pick1, 1.0, 0.0).astype(BF16)
    tr = lax.broadcasted_iota(I32, (ts, ts), 0)
    tc = lax.broadcasted_iota(I32, (ts, ts), 1)
    earlier_tok = jnp.where(tr < tc, 1.0, 0.0).astype(BF16)
    er = lax.broadcasted_iota(I32, (N_EXPERTS, N_EXPERTS), 0)
    ec = lax.broadcasted_iota(I32, (N_EXPERTS, N_EXPERTS), 1)
    lower_exp = jnp.where(ec < er, 1.0, 0.0).astype(BF16)
    same_before = _dot(onehot, earlier_tok)
    cnt = jnp.sum(onehot.astype(F32), axis=1, keepdims=True)
    seg = jnp.ceil(cnt * (1.0 / SEG_ALIGN)) * SEG_ALIGN
    seg_lanes = jnp.broadcast_to(seg, (N_EXPERTS, LANE)).astype(BF16)
    first_slot = _dot(lower_exp, seg_lanes)[:, 0:1]
    slot = same_before + first_slot
    p0 = jnp.sum(jnp.where(pick0, slot, 0.0), axis=0, keepdims=True)
    p1 = jnp.sum(jnp.where(pick1, slot, 0.0), axis=0, keepdims=True)
    pos_ref[0:1, :] = p0.astype(I32)
    pos_ref[1:2, :] = p1.astype(I32)
    cnt_ref[0] = cnt


def _mix(u, y_gla, x, w_pool, pool_scale, w_out, g1, sc2, sh2, wr_hi, wr_lo, b_router):
    bn, sn, d = x.shape
    ts = SEQ_TILE
    nt = sn // ts
    hpt = ts // POOL_HALO
    n_halo = sn // POOL_HALO
    u_halo = u.reshape(bn, n_halo, POOL_HALO, D_POOL)
    tok = lambda width: pl.BlockSpec((1, ts, width), lambda b, j: (b, j, 0))
    per_batch = pl.BlockSpec((1, 1, d), lambda b, j: (b, 0, 0))
    full = lambda a: pl.BlockSpec(a.shape, lambda b, j: (0,) * a.ndim)
    prev_halo = pl.BlockSpec((1, 1, POOL_HALO, D_POOL),
                             lambda b, j: (b, jnp.maximum(j * hpt - 1, 0), 0, 0))
    next_halo = pl.BlockSpec((1, 1, POOL_HALO, D_POOL),
                             lambda b, j: (b, jnp.minimum((j + 1) * hpt, n_halo - 1), 0, 0))
    lane_tok = lambda rows: pl.BlockSpec((rows, ts), lambda b, j: (0, b * nt + j))
    t = bn * sn
    return pl.pallas_call(
        functools.partial(_mix_kernel, seq_len=sn),
        grid=(bn, nt),
        in_specs=[tok(D_POOL), prev_halo, next_halo, tok(D_GLA_V), tok(d), full(w_pool), full(pool_scale),
                  full(w_out), per_batch, per_batch, per_batch, full(wr_hi), full(wr_lo), full(b_router)],
        out_specs=[tok(d), tok(d), lane_tok(2), lane_tok(2),
                   pl.BlockSpec((1, N_EXPERTS, 1), lambda b, j: (b * nt + j, 0, 0))],
        out_shape=[jax.ShapeDtypeStruct((bn, sn, d), F32), jax.ShapeDtypeStruct((bn, sn, d), BF16),
                   jax.ShapeDtypeStruct((2, t), I32), jax.ShapeDtypeStruct((2, t), F32),
                   jax.ShapeDtypeStruct((bn * nt, N_EXPERTS, 1), F32)],
        compiler_params=pltpu.CompilerParams(
            dimension_semantics=("arbitrary", "arbitrary"), vmem_limit_bytes=VMEM_LIMIT),
        name="mix_route",
    )(u, u_halo, u_halo, y_gla, x, w_pool, pool_scale, w_out, g1, sc2, sh2, wr_hi, wr_lo, b_router)


def _segment_copies(meta_ref, tile, make_copy, action):
    base = tile * META_PER_TILE
    for e in range(N_EXPERTS):
        n = pl.multiple_of(meta_ref[base + e], SEG_ALIGN)
        first_slot = pl.multiple_of(meta_ref[base + N_EXPERTS + e], SEG_ALIGN)
        first_row = pl.multiple_of(meta_ref[base + 2 * N_EXPERTS + e], SEG_ALIGN)

        @pl.when(n > 0)
        def _():
            action(make_copy(first_slot, first_row, n))


def _start(copy):
    copy.start()


def _wait(copy):
    copy.wait()


def _pack_rows(x):
    lo = lax.bitcast_convert_type(x[:, :D_PACK], U32)
    hi = lax.bitcast_convert_type(x[:, D_PACK:], U32)
    return hi | (lo >> 16)


def _unpack_rows(u):
    lo = lax.bitcast_convert_type(u << 16, F32)
    hi = lax.bitcast_convert_type(u & jnp.uint32(0xFFFF0000), F32)
    return lo, hi


def _dispatch_kernel(meta_ref, tail_ref, h_ref, pos_ref, xs_ref, buf, zbuf, sem):
    i = pl.program_id(0)
    ts = h_ref.shape[0]

    def copy(first_slot, first_row, n):
        return pltpu.make_async_copy(buf.at[pl.ds(first_slot, n)], xs_ref.at[pl.ds(first_row, n)], sem)

    slot = lax.broadcasted_iota(I32, (TILE_SLOTS, ts), 0)
    perm = jnp.where((slot == pos_ref[0:1, :]) | (slot == pos_ref[1:2, :]), 1.0, 0.0).astype(BF16)
    rows = _dot(perm, h_ref[...])

    @pl.when(i > 0)
    def _():
        _segment_copies(meta_ref, i - 1, copy, _wait)

    buf[...] = _pack_rows(rows)
    _segment_copies(meta_ref, i, copy, _start)

    @pl.when(i == pl.num_programs(0) - 1)
    def _():
        _segment_copies(meta_ref, i, copy, _wait)
        zbuf[...] = jnp.zeros_like(zbuf)

        def tail_copies(action):
            for e in range(N_EXPERTS):
                n = pl.multiple_of(tail_ref[e], SEG_ALIGN)
                first_row = pl.multiple_of(tail_ref[N_EXPERTS + e], SEG_ALIGN)

                @pl.when(n > 0)
                def _():
                    action(pltpu.make_async_copy(zbuf.at[pl.ds(0, n)], xs_ref.at[pl.ds(first_row, n)], sem))

        tail_copies(_start)
        tail_copies(_wait)

        def zero_block(blk, carry):
            first_row = pl.multiple_of(blk * MOE_BLOCK, MOE_BLOCK)
            fill = pltpu.make_async_copy(zbuf, xs_ref.at[pl.ds(first_row, MOE_BLOCK)], sem)
            fill.start()
            fill.wait()
            return carry

        lax.fori_loop(tail_ref[2 * N_EXPERTS], xs_ref.shape[0] // MOE_BLOCK, zero_block, 0)


def _dispatch(meta, tails, h, pos, n_rows_sorted):
    t, d = h.shape
    ts = SEQ_TILE
    return pl.pallas_call(
        _dispatch_kernel,
        grid_spec=pltpu.PrefetchScalarGridSpec(
            num_scalar_prefetch=2,
            grid=(t // ts,),
            in_specs=[pl.BlockSpec((ts, d), lambda i, m, tl: (i, 0)),
                      pl.BlockSpec((2, ts), lambda i, m, tl: (0, i))],
            out_specs=pl.BlockSpec(memory_space=pl.ANY),
            scratch_shapes=[pltpu.VMEM((TILE_SLOTS, D_PACK), U32), pltpu.VMEM((MOE_BLOCK, D_PACK), U32),
                            pltpu.SemaphoreType.DMA(())],
        ),
        out_shape=jax.ShapeDtypeStruct((n_rows_sorted, D_PACK), U32),
        compiler_params=pltpu.CompilerParams(
            dimension_semantics=("arbitrary",), vmem_limit_bytes=VMEM_LIMIT),
        name="dispatch",
    )(meta, tails, h, pos)


def _ffn_kernel(blk_e_ref, n_used_ref, xs_ref, w1_ref, w3_ref, w2_ref, ys_ref):
    del blk_e_ref

    @pl.when(pl.program_id(0) < n_used_ref[0])
    def _():
        lo, hi = _unpack_rows(xs_ref[...])
        xb = jnp.concatenate([lo.astype(BF16), hi.astype(BF16)], axis=1)
        a = _dot(xb, w1_ref[0, 0].astype(BF16))
        hid = (a * jax.nn.sigmoid(a)) * _dot(xb, w3_ref[0, 0].astype(BF16))
        y = _dot(hid.astype(BF16), w2_ref[0, 0].astype(BF16))
        ys_ref[...] = _pack_rows(y.astype(BF16).astype(F32))

    @pl.when(pl.program_id(0) >= n_used_ref[0])
    def _():
        ys_ref[...] = jnp.zeros_like(ys_ref)


def _ffn(layer, blk_e, n_used, xs, w1, w3, w2):
    rows = xs.shape[0]
    n_blocks = rows // MOE_BLOCK
    last = lambda i, n_used_ref: jnp.minimum(i, n_used_ref[0] - 1)
    row_spec = pl.BlockSpec((MOE_BLOCK, D_PACK), lambda i, be, nu: (last(i, nu), 0))
    w_spec = lambda a: pl.BlockSpec((1, 1) + a.shape[2:], lambda i, be, nu: (layer, be[last(i, nu)], 0, 0))
    return pl.pallas_call(
        _ffn_kernel,
        grid_spec=pltpu.PrefetchScalarGridSpec(
            num_scalar_prefetch=2,
            grid=(n_blocks,),
            in_specs=[row_spec, w_spec(w1), w_spec(w3), w_spec(w2)],
            out_specs=pl.BlockSpec((MOE_BLOCK, D_PACK), lambda i, be, nu: (i, 0)),
        ),
        out_shape=jax.ShapeDtypeStruct(xs.shape, U32),
        compiler_params=pltpu.CompilerParams(
            dimension_semantics=("arbitrary",), vmem_limit_bytes=VMEM_LIMIT),
        name="expert_ffn",
    )(blk_e, n_used, xs, w1, w3, w2)


def _combine_kernel(meta_ref, ys_ref, x_ref, pos_ref, rw_ref, posc_ref, g2_ref, fg_ref, o_ref, buf, sem,
                    *, final_norm):
    i = pl.program_id(0)
    ts = x_ref.shape[0]

    def fetch(tile, action):
        half = tile % 2

        def copy(first_slot, first_row, n):
            return pltpu.make_async_copy(ys_ref.at[pl.ds(first_row, n)],
                                         buf.at[half, pl.ds(first_slot, n)], sem.at[half])

        _segment_copies(meta_ref, tile, copy, action)

    @pl.when(i == 0)
    def _():
        buf[...] = jnp.zeros_like(buf)
        fetch(i, _start)

    @pl.when(i + 1 < pl.num_programs(0))
    def _():
        fetch(i + 1, _start)

    fetch(i, _wait)
    lo, hi = _unpack_rows(buf[i % 2])
    slot = lax.broadcasted_iota(I32, (TILE_SLOTS, ts), 0)
    hit0 = slot == pos_ref[0:1, :]
    hit1 = slot == pos_ref[1:2, :]
    w_rows = jnp.sum(jnp.where(hit0, rw_ref[0:1, :], 0.0) + jnp.where(hit1, rw_ref[1:2, :], 0.0),
                     axis=1, keepdims=True)
    used = jnp.sum(jnp.where(hit0 | hit1, 1.0, 0.0), axis=1, keepdims=True) > 0.0
    weighted = jnp.concatenate([jnp.where(used, lo * w_rows, 0.0).astype(BF16),
                                jnp.where(used, hi * w_rows, 0.0).astype(BF16)], axis=1)
    slot_lane = lax.broadcasted_iota(I32, (ts, TILE_SLOTS), 1)
    posc = posc_ref[...]
    unperm = jnp.where((slot_lane == posc[:, 0:1]) | (slot_lane == posc[:, 1:2]), 1.0, 0.0).astype(BF16)
    x = x_ref[...] + g2_ref[0] * _dot(unperm, weighted)
    if final_norm:
        ms = jnp.mean(x * x, axis=-1, keepdims=True)
        x = x * lax.rsqrt(ms + EPS) * fg_ref[...]
    o_ref[...] = x


def _combine(meta, ys, x, pos, route_w, g2, final_g, seq_len, final_norm):
    t, d = x.shape
    ts = SEQ_TILE
    tiles_per_seq = seq_len // ts
    lane_tok = pl.BlockSpec((2, ts), lambda i, m: (0, i))
    return pl.pallas_call(
        functools.partial(_combine_kernel, final_norm=final_norm),
        grid_spec=pltpu.PrefetchScalarGridSpec(
            num_scalar_prefetch=1,
            grid=(t // ts,),
            in_specs=[pl.BlockSpec(memory_space=pl.ANY),
                      pl.BlockSpec((ts, d), lambda i, m: (i, 0)),
                      lane_tok, lane_tok,
                      pl.BlockSpec((ts, 2), lambda i, m: (i, 0)),
                      pl.BlockSpec((1, 1, d), lambda i, m: (i // tiles_per_seq, 0, 0)),
                      pl.BlockSpec((1, d), lambda i, m: (0, 0))],
            out_specs=pl.BlockSpec((ts, d), lambda i, m: (i, 0)),
            scratch_shapes=[pltpu.VMEM((2, TILE_SLOTS, D_PACK), U32), pltpu.SemaphoreType.DMA((2,))],
        ),
        out_shape=jax.ShapeDtypeStruct((t, d), F32),
        compiler_params=pltpu.CompilerParams(
            dimension_semantics=("arbitrary",), vmem_limit_bytes=VMEM_LIMIT),
        name="combine",
    )(meta, ys, x, pos, route_w, pos.T, g2, final_g.reshape(1, d))


def _dispatch_plan(tile_counts, n_assign):
    n_tiles = tile_counts.shape[0]
    seg = (tile_counts.astype(I32) + SEG_ALIGN - 1) // SEG_ALIGN * SEG_ALIGN
    totals = jnp.sum(seg, axis=0)
    padded = (totals + MOE_BLOCK - 1) // MOE_BLOCK * MOE_BLOCK
    pad_end = jnp.cumsum(padded)
    pad_start = pad_end - padded
    first_row = pad_start[None, :] + jnp.cumsum(seg, axis=0) - seg
    first_slot = jnp.cumsum(seg, axis=1) - seg
    meta = jnp.stack([seg, first_slot, first_row], axis=1).reshape(-1)
    max_rows = n_assign + n_tiles * N_EXPERTS * (SEG_ALIGN - 1)
    n_blocks = -(-max_rows // MOE_BLOCK) + N_EXPERTS
    blk_start = jnp.arange(n_blocks, dtype=I32) * MOE_BLOCK
    blk_e = jnp.minimum(jnp.sum((pad_end[None, :] <= blk_start[:, None]).astype(I32), axis=1), N_EXPERTS - 1)
    n_used = (pad_end[-1] // MOE_BLOCK).astype(I32).reshape(1)
    tails = jnp.concatenate([padded - totals, pad_start + totals, n_used])
    return meta.astype(I32), tails.astype(I32), blk_e.astype(I32), n_used, n_blocks * MOE_BLOCK


def kernel(x, c, w_mod, b_mod, w_in, w_pool, pool_scale, w_gk_up, b_gk, gla_norm_g, w_out,
           w_router, b_router, w1, w3, w2, final_g):
    bn, sn, d = x.shape
    t = bn * sn
    mod = _modulation(c, w_mod, b_mod).reshape(DEPTH, bn, 6, 1, d)

    w_main = w_in[:, :, :D_MAIN].astype(BF16)
    w_r = w_in[:, :, D_MAIN:].astype(BF16)
    zero_rank = jnp.zeros_like(w_gk_up[:, 0])
    w_gk = jnp.stack([jnp.concatenate([w_gk_up[:, 0], zero_rank], axis=1),
                      jnp.concatenate([zero_rank, w_gk_up[:, 1]], axis=1)], axis=1).astype(BF16)
    b_gk3 = b_gk.reshape(DEPTH, 2, 1, D_QK)
    w_pool_b = w_pool.astype(BF16)
    w_out_b = w_out.astype(BF16)
    wr_t = w_router.T
    wr_hi = wr_t.astype(BF16)
    wr_lo = (wr_t - wr_hi.astype(F32)).astype(BF16)
    br = b_router.reshape(N_EXPERTS, 1)

    for l in range(DEPTH):
        sh1, sc1, g1, sh2, sc2, g2 = (mod[l, :, i] for i in range(6))
        u, g, v, qf, kf, df, qb, kb, db, decf, decb = _inproj(
            x, sc1, sh1, w_main[l], w_r[l], w_gk[l], b_gk3[l])
        y_gla = _gla(qf, kf, df, qb, kb, db, v, decf, decb, g, gla_norm_g[l])
        x, h, pos, route_w, tile_counts = _mix(
            u, y_gla, x, w_pool_b[l], pool_scale[l].reshape(1, D_POOL), w_out_b[l], g1, sc2, sh2,
            wr_hi, wr_lo, br)
        meta, tails, blk_e, n_used, n_rows_sorted = _dispatch_plan(tile_counts[:, :, 0], 2 * t)
        xs = _dispatch(meta, tails, h.reshape(t, d), pos, n_rows_sorted)
        ys = _ffn(l, blk_e, n_used, xs, w1, w3, w2)
        x = _combine(meta, ys, x.reshape(t, d), pos, route_w, g2, final_g, sn, l == DEPTH - 1)
        x = x.reshape(bn, sn, d)
    return x
```

```python
import functools

import jax
import jax.numpy as jnp
from jax import lax
from jax.experimental import pallas as pl
from jax.experimental.pallas import tpu as pltpu

F32 = jnp.float32
BF16 = jnp.bfloat16
I32 = jnp.int32
U32 = jnp.uint32

D_MODEL = 1024
DEPTH = 2
D_POOL = 512
POOL_WINDOWS = (2, 4, 8, 16)
POOL_GROUP = 128
POOL_HALO = 16
D_GLA_V = 512
GLA_HEADS = 4
GLA_DK = 64
GLA_DV = 128
D_QK = GLA_HEADS * GLA_DK
GLA_RANK = 16
GATE_NORM = 16.0
GLA_CHUNK = 64
GLA_GROUP = 4
GLA_OUT_ROWS = 256
D_MAIN = D_POOL + 2 * D_QK + 2 * D_GLA_V
N_EXPERTS = 32
N_GROUPS = 4
EXPERTS_PER_GROUP = 8
D_FF = 512
EPS = 1e-6

SEQ_TILE = 512
MIX_ROWS = 256
INPROJ_TILE = 1024
CUMSUM_GROUP = 256
MOE_BLOCK = 512
LANE = 128
SEG_ALIGN = 8
TILE_SLOTS = 2 * SEQ_TILE + 256
D_PACK = D_MODEL // 2
META_PER_TILE = 3 * N_EXPERTS
assert TILE_SLOTS >= 2 * SEQ_TILE + N_EXPERTS * (SEG_ALIGN - 1) and 2 * SEQ_TILE // SEG_ALIGN <= 256
VMEM_LIMIT = 56 * 1024 * 1024

NT_DIMS = (((1,), (1,)), ((), ()))
TN_DIMS = (((0,), (0,)), ((), ()))


def _dot(a, b):
    return jnp.dot(a, b, preferred_element_type=F32)


def _split_bf16(a):
    hi = a.astype(BF16)
    lo = (a - hi.astype(F32)).astype(BF16)
    return hi, lo


def _mod_kernel(c_ref, w_ref, b_ref, o_ref):
    c = c_ref[...]
    c_act = (c * jax.nn.sigmoid(c)).astype(BF16)
    o_ref[0] = _dot(c_act, w_ref[0].astype(BF16)) + b_ref[0]


def _modulation(c, w_mod, b_mod):
    n_layers, d, n_out = w_mod.shape
    bn = c.shape[0]
    tn = 1536
    return pl.pallas_call(
        _mod_kernel,
        grid=(n_layers, n_out // tn),
        in_specs=[
            pl.BlockSpec((bn, d), lambda l, j: (0, 0)),
            pl.BlockSpec((1, d, tn), lambda l, j: (l, 0, j)),
            pl.BlockSpec((1, 1, tn), lambda l, j: (l, 0, j)),
        ],
        out_specs=pl.BlockSpec((1, bn, tn), lambda l, j: (l, 0, j)),
        out_shape=jax.ShapeDtypeStruct((n_layers, bn, n_out), F32),
        compiler_params=pltpu.CompilerParams(
            dimension_semantics=("arbitrary", "arbitrary"), vmem_limit_bytes=VMEM_LIMIT),
        name="modulation",
    )(c, w_mod, b_mod.reshape(n_layers, 1, n_out))


def _log_sigmoid(x):
    return jnp.minimum(x, 0.0) - jnp.log1p(jnp.exp(-jnp.abs(x)))


def _inproj_kernel(x_ref, sc_ref, sh_ref, wmain_ref, wr_ref, wgk_ref, bgk_ref,
                   u_ref, g_ref, v_ref, qf_ref, kf_ref, df_ref, qb_ref, kb_ref, db_ref,
                   decf_ref, decb_ref):
    tile = x_ref.shape[1]
    sub = CUMSUM_GROUP
    sub_chunks = sub // GLA_CHUNK
    row = lax.broadcasted_iota(I32, (sub, sub), 0)
    col = lax.broadcasted_iota(I32, (sub, sub), 1)
    same_chunk = (row // GLA_CHUNK) == (col // GLA_CHUNK)
    prefix = jnp.where(same_chunk & (col <= row), 1.0, 0.0).astype(BF16)
    suffix = jnp.where(same_chunk & (col >= row), 1.0, 0.0).astype(BF16)

    groups = range(tile // sub)
    rows = [slice(s * sub, (s + 1) * sub) for s in groups]
    chunks = [slice(s * sub_chunks, (s + 1) * sub_chunks) for s in groups]
    dirs = ((0, prefix, GLA_CHUNK - 1, qf_ref, kf_ref, df_ref, decf_ref),
            (1, suffix, 0, qb_ref, kb_ref, db_ref, decb_ref))

    zs, rs = [], []
    for s in groups:
        x = x_ref[0, rows[s], :]
        ms = jnp.mean(x * x, axis=-1, keepdims=True)
        h = x * lax.rsqrt(ms + EPS) * (1.0 + sc_ref[0]) + sh_ref[0]
        hb = h.astype(BF16)
        zs.append(_dot(hb, wmain_ref[...]))
        rs.append(_dot(hb, wr_ref[...]).astype(BF16))

    logas = [[_log_sigmoid(_dot(rs[s], wgk_ref[idx]) + bgk_ref[idx]) / GATE_NORM for idx in range(2)]
             for s in groups]

    bs = []
    for s in groups:
        per_dir = []
        for idx, tri, *_ in dirs:
            hi, lo = _split_bf16(logas[s][idx])
            per_dir.append(_dot(tri, hi) + _dot(tri, lo))
        bs.append(per_dir)

    for s in groups:
        z = zs[s]
        u_ref[0, rows[s], :] = z[:, 0:D_POOL]
        q = z[:, D_POOL:D_POOL + D_QK] * (GLA_DK ** -0.5)
        k = z[:, D_POOL + D_QK:D_POOL + 2 * D_QK]
        v_ref[0, rows[s], :] = z[:, D_POOL + 2 * D_QK:D_POOL + 2 * D_QK + D_GLA_V].astype(BF16)
        g_ref[0, rows[s], :] = z[:, D_POOL + 2 * D_QK + D_GLA_V:D_MAIN]
        for idx, _, last_row, q_out, k_out, d_out, dec_out in dirs:
            b = bs[s][idx]
            b3 = b.reshape(sub_chunks, GLA_CHUNK, D_QK)
            total = b3[:, last_row:last_row + 1, :]
            q_out[0, rows[s], :] = (q * jnp.exp(b)).astype(BF16)
            k_out[0, rows[s], :] = (k * jnp.exp(-b)).astype(BF16)
            d_out[0, rows[s], :] = (k * jnp.exp(total - b3).reshape(sub, D_QK)).astype(BF16)
            dec_out[0, chunks[s], :] = jnp.exp(total).reshape(sub_chunks, D_QK)


def _inproj(x, sc, sh, w_main, w_r, w_gk, b_gk):
    bn, sn, d = x.shape
    ts = INPROJ_TILE
    n_chunks = ts // GLA_CHUNK
    tok = lambda width: pl.BlockSpec((1, ts, width), lambda b, j: (b, j, 0))
    per_batch = pl.BlockSpec((1, 1, d), lambda b, j: (b, 0, 0))
    full = lambda a: pl.BlockSpec(a.shape, lambda b, j: (0,) * a.ndim)
    dec_spec = pl.BlockSpec((1, n_chunks, D_QK), lambda b, j: (b, j, 0))
    tok_shape = lambda width, dt: jax.ShapeDtypeStruct((bn, sn, width), dt)
    dec_shape = jax.ShapeDtypeStruct((bn, sn // GLA_CHUNK, D_QK), F32)
    return pl.pallas_call(
        _inproj_kernel,
        grid=(bn, sn // ts),
        in_specs=[tok(d), per_batch, per_batch, full(w_main), full(w_r), full(w_gk), full(b_gk)],
        out_specs=[tok(D_POOL), tok(D_GLA_V), tok(D_GLA_V)] + [tok(D_QK)] * 6 + [dec_spec, dec_spec],
        out_shape=[tok_shape(D_POOL, F32), tok_shape(D_GLA_V, F32), tok_shape(D_GLA_V, BF16)]
        + [tok_shape(D_QK, BF16)] * 6 + [dec_shape, dec_shape],
        compiler_params=pltpu.CompilerParams(
            dimension_semantics=("arbitrary", "arbitrary"), vmem_limit_bytes=VMEM_LIMIT),
        name="inproj",
    )(x, sc, sh, w_main, w_r, w_gk, b_gk)


def _gla_kernel(qf_ref, kf_ref, df_ref, qb_ref, kb_ref, db_ref, v_ref, decf_ref, decb_ref,
                g_ref, ng_ref, y_ref, of_acc, ob_acc, state_f, state_b):
    sn = v_ref.shape[1]
    n_chunks = sn // GLA_CHUNK
    lane_head = lax.broadcasted_iota(I32, (GLA_CHUNK, D_QK), 1) // GLA_DK
    head_masks = [lane_head == hd for hd in range(GLA_HEADS)]
    state_blocks = (lax.broadcasted_iota(I32, (D_GLA_V, D_QK), 0) // GLA_DV
                    == lax.broadcasted_iota(I32, (D_GLA_V, D_QK), 1) // GLA_DK)
    def group_masks(own):
        masks = []
        for m in range(GLA_GROUP):
            r = lax.broadcasted_iota(I32, (GLA_CHUNK, GLA_CHUNK * (m + 1)), 0)
            c = lax.broadcasted_iota(I32, (GLA_CHUNK, GLA_CHUNK * (m + 1)), 1) - GLA_CHUNK * m
            masks.append((c < 0) | own(r, c))
        return masks

    causal = group_masks(lambda r, c: c <= r)
    anti = group_masks(lambda r, c: c > r)

    def scaled(a, factor):
        return a if factor is None else (a.astype(F32) * factor).astype(BF16)

    def product(factors):
        out = None
        for f in factors:
            out = f if out is None else out * f
        return out

    def group_out(first_chunk, order, q_ref, k_ref, d_ref, dec_ref, state, masks, o_acc):
        base = pl.multiple_of(first_chunk * GLA_CHUNK, GLA_GROUP * GLA_CHUNK)
        slab = pl.ds(base, GLA_GROUP * GLA_CHUNK)
        part = lambda a, c: a[c * GLA_CHUNK:(c + 1) * GLA_CHUNK]
        q_all, k_all, d_all, v_all = q_ref[0, slab, :], k_ref[0, slab, :], d_ref[0, slab, :], v_ref[0, slab, :]
        qe = [part(q_all, c) for c in order]
        ke = [part(k_all, c) for c in order]
        kd = [part(d_all, c) for c in order]
        vv = [part(v_all, c) for c in order]
        dec = [dec_ref[0, pl.ds(first_chunk + c, 1), :] for c in order]

        st = state[...]
        st_heads = jnp.where(state_blocks, st, 0.0).astype(BF16)
        q_in = jnp.concatenate([scaled(qe[m], product(dec[:m])) for m in range(GLA_GROUP)], axis=0)
        inter = lax.dot_general(q_in, st_heads, NT_DIMS, preferred_element_type=F32)
        k_out = jnp.concatenate([scaled(kd[c], product(dec[c + 1:])) for c in range(GLA_GROUP)], axis=0)
        upd = lax.dot_general(jnp.concatenate(vv, axis=0), k_out, TN_DIMS, preferred_element_type=F32)
        state[...] = st * product(dec) + upd

        scores = []
        for m in range(GLA_GROUP):
            keys = jnp.concatenate([scaled(kd[c], product(dec[c + 1:m])) for c in range(m)] + [ke[m]], axis=0)
            q_heads = jnp.concatenate([jnp.where(hm, qe[m], jnp.zeros_like(qe[m])) for hm in head_masks], axis=0)
            scores.append(lax.dot_general(q_heads, keys, NT_DIMS, preferred_element_type=F32))

        def finish():
            for m in range(GLA_GROUP):
                vals = jnp.concatenate(vv[:m + 1], axis=0)
                outs = []
                for hd in range(GLA_HEADS):
                    att = jnp.where(masks[m], scores[m][hd * GLA_CHUNK:(hd + 1) * GLA_CHUNK], 0.0).astype(BF16)
                    outs.append(_dot(att, vals[:, hd * GLA_DV:(hd + 1) * GLA_DV]))
                rows = pl.ds(base + order[m] * GLA_CHUNK, GLA_CHUNK)
                o_acc[rows, :] = jnp.concatenate(outs, axis=1) + part(inter, m)

        return finish

    state_f[...] = jnp.zeros_like(state_f)
    state_b[...] = jnp.zeros_like(state_b)
    n_groups = n_chunks // GLA_GROUP
    ascending = list(range(GLA_GROUP))

    def scan_body(i, carry):
        finish_f = group_out(i * GLA_GROUP, ascending, qf_ref, kf_ref, df_ref, decf_ref, state_f, causal, of_acc)
        finish_b = group_out((n_groups - 1 - i) * GLA_GROUP, ascending[::-1], qb_ref, kb_ref, db_ref, decb_ref,
                             state_b, anti, ob_acc)
        finish_f()
        finish_b()
        return carry

    lax.fori_loop(0, n_groups, scan_body, 0)

    norm_g = ng_ref[...]

    def out_body(s, carry):
        rows = pl.ds(pl.multiple_of(s * GLA_OUT_ROWS, GLA_OUT_ROWS), GLA_OUT_ROWS)
        o = of_acc[rows, :] + ob_acc[rows, :]
        gate = g_ref[0, rows, :]
        gate = gate * jax.nn.sigmoid(gate)
        outs = []
        for hd in range(GLA_HEADS):
            oh = o[:, hd * GLA_DV:(hd + 1) * GLA_DV]
            oh = oh * lax.rsqrt(jnp.mean(oh * oh, axis=-1, keepdims=True) + EPS) * norm_g
            outs.append(oh)
        y_ref[0, rows, :] = (jnp.concatenate(outs, axis=1) * gate).astype(BF16)
        return carry

    lax.fori_loop(0, sn // GLA_OUT_ROWS, out_body, 0)


def _gla(qf, kf, df, qb, kb, db, v, decf, decb, g, norm_g):
    bn, sn, _ = v.shape
    seq = lambda a: pl.BlockSpec((1,) + a.shape[1:], lambda b: (b, 0, 0))
    args = (qf, kf, df, qb, kb, db, v, decf, decb, g)
    return pl.pallas_call(
        _gla_kernel,
        grid=(bn,),
        in_specs=[seq(a) for a in args] + [pl.BlockSpec((1, GLA_DV), lambda b: (0, 0))],
        out_specs=pl.BlockSpec((1, sn, D_GLA_V), lambda b: (b, 0, 0)),
        out_shape=jax.ShapeDtypeStruct((bn, sn, D_GLA_V), BF16),
        scratch_shapes=[pltpu.VMEM((sn, D_GLA_V), F32), pltpu.VMEM((sn, D_GLA_V), F32),
                        pltpu.VMEM((D_GLA_V, D_QK), F32), pltpu.VMEM((D_GLA_V, D_QK), F32)],
        compiler_params=pltpu.CompilerParams(
            dimension_semantics=("arbitrary",), vmem_limit_bytes=VMEM_LIMIT),
        name="gla",
    )(*args, norm_g.reshape(1, GLA_DV))


def _mix_kernel(u_ref, up_ref, un_ref, yg_ref, x_ref, wp_ref, ps_ref, wo_ref, g1_ref, sc_ref, sh_ref,
                wrh_ref, wrl_ref, br_ref,
                xo_ref, h_ref, pos_ref, rw_ref, cnt_ref, ext_ref, earlier_ref, *, seq_len):
    j = pl.program_id(1)
    ts = u_ref.shape[1]
    sub = MIX_ROWS
    groups = range(ts // sub)
    rows = [slice(s * sub, (s + 1) * sub) for s in groups]

    @pl.when((pl.program_id(0) == 0) & (j == 0))
    def _():
        tr = lax.broadcasted_iota(I32, (ts, ts), 0)
        tc = lax.broadcasted_iota(I32, (ts, ts), 1)
        earlier_ref[...] = jnp.where(tr < tc, 1.0, 0.0).astype(BF16)
        ext_ref[ts + 2 * POOL_HALO:, :] = jnp.zeros((POOL_HALO, D_POOL), F32)

    prev_ok = jnp.where(j > 0, 1.0, 0.0)
    next_ok = jnp.where(j < pl.num_programs(1) - 1, 1.0, 0.0)
    ext_ref[0:POOL_HALO, :] = up_ref[0, 0] * prev_ok
    ext_ref[POOL_HALO:POOL_HALO + ts, :] = u_ref[0]
    ext_ref[POOL_HALO + ts:POOL_HALO + ts + POOL_HALO, :] = un_ref[0, 0] * next_ok
    pos = j * ts + lax.broadcasted_iota(I32, (ts, 1), 0)

    def window_sum(lanes, w):
        if w == 2:
            return ext_ref[POOL_HALO - 1:POOL_HALO - 1 + ts, lanes] + ext_ref[POOL_HALO:POOL_HALO + ts, lanes]
        first = SEG_ALIGN
        n = ts + SEG_ALIGN * (w.bit_length() - 2)
        e = ext_ref[first:first + n + SEG_ALIGN, lanes]
        p = e[0:n] + e[1:n + 1]
        step = 2
        while step < w // 2:
            n -= SEG_ALIGN
            p = p[0:n] + p[step:step + n]
            step *= 2
        off = POOL_HALO - w // 2 - first
        return p[off:off + ts] + p[off + w // 2:off + w // 2 + ts]

    pooled = []
    for gi, w in enumerate(POOL_WINDOWS):
        lanes = slice(gi * POOL_GROUP, (gi + 1) * POOL_GROUP)
        acc = window_sum(lanes, w)
        tok = u_ref[0, :, lanes]

        def clipped(edge):
            lo = jnp.clip(pos[edge] - w // 2, 0, seq_len - 1)
            hi = jnp.clip(pos[edge] - w // 2 + w - 1, 0, seq_len - 1)
            return acc[edge] / (hi - lo + 1).astype(F32) - tok[edge]

        inner = slice(POOL_HALO, ts - POOL_HALO)
        pooled.append(jnp.concatenate(
            [clipped(slice(0, POOL_HALO)), acc[inner] * (1.0 / w) - tok[inner], clipped(slice(ts - POOL_HALO, ts))],
            axis=0).astype(BF16))

    ys = []
    for s in groups:
        y_pool = [(_dot(pooled[gi][rows[s]], wp_ref[gi])
                   * ps_ref[:, gi * POOL_GROUP:(gi + 1) * POOL_GROUP]).astype(BF16)
                  for gi in range(len(POOL_WINDOWS))]
        ys.append(jnp.concatenate(y_pool + [yg_ref[0, rows[s], :]], axis=1))
    mixes = [_dot(ys[s], wo_ref[...]) for s in groups]

    logits = []
    for s in groups:
        x = x_ref[0, rows[s], :] + g1_ref[0] * mixes[s]
        xo_ref[0, rows[s], :] = x
        ms = jnp.mean(x * x, axis=-1, keepdims=True)
        h = x * lax.rsqrt(ms + EPS) * (1.0 + sc_ref[0]) + sh_ref[0]
        h_ref[0, rows[s], :] = h.astype(BF16)
        h_hi, h_lo = _split_bf16(h)
        logits.append(lax.dot_general(wrh_ref[...], h_hi, NT_DIMS, preferred_element_type=F32)
                      + lax.dot_general(wrh_ref[...], h_lo, NT_DIMS, preferred_element_type=F32)
                      + lax.dot_general(wrl_ref[...], h_hi, NT_DIMS, preferred_element_type=F32))

    score = jax.nn.sigmoid(jnp.concatenate(logits, axis=1))
    sel = (score + br_ref[...]).reshape(N_GROUPS, EXPERTS_PER_GROUP, ts)
    local = lax.broadcasted_iota(I32, sel.shape, 1)
    big = EXPERTS_PER_GROUP
    m1 = jnp.max(sel, axis=1, keepdims=True)
    i1 = jnp.min(jnp.where(sel == m1, local, big), axis=1, keepdims=True)
    rest = jnp.where(local == i1, -jnp.inf, sel)
    m2 = jnp.max(rest, axis=1, keepdims=True)
    i2 = jnp.min(jnp.where(rest == m2, local, big), axis=1, keepdims=True)
    gscore = m1 + m2
    gid = lax.broadcasted_iota(I32, gscore.shape, 0)
    gbest = jnp.max(gscore, axis=0, keepdims=True)
    gsel = jnp.min(jnp.where(gscore == gbest, gid, N_GROUPS), axis=0, keepdims=True)
    in_group = gid == gsel
    pick0 = (in_group & (local == i1)).reshape(N_EXPERTS, ts)
    pick1 = (in_group & (local == i2)).reshape(N_EXPERTS, ts)
    s0 = jnp.sum(jnp.where(pick0, score, 0.0), axis=0, keepdims=True)
    s1 = jnp.sum(jnp.where(pick1, score, 0.0), axis=0, keepdims=True)
    rw_ref[0:1, :] = s0 / (s0 + s1)
    rw_ref[1:2, :] = s1 / (s0 + s1)

    onehot = jnp.where(pick0 | pick1, 1.0, 0.0).astype(BF16)
    er = lax.broadcasted_iota(I32, (N_EXPERTS, N_EXPERTS), 0)
    ec = lax.broadcasted_iota(I32, (N_EXPERTS, N_EXPERTS), 1)
    lower_exp = jnp.where(ec < er, 1.0, 0.0).astype(BF16)
    same_before = _dot(onehot, earlier_ref[...])
    cnt = jnp.sum(onehot.astype(F32), axis=1, keepdims=True)
    seg = jnp.ceil(cnt * (1.0 / SEG_ALIGN)) * SEG_ALIGN
    seg_lanes = jnp.broadcast_to(seg, (N_EXPERTS, LANE)).astype(BF16)
    first_slot = _dot(lower_exp, seg_lanes)[:, 0:1]
    slot = same_before + first_slot
    p0 = jnp.sum(jnp.where(pick0, slot, 0.0), axis=0, keepdims=True)
    p1 = jnp.sum(jnp.where(pick1, slot, 0.0), axis=0, keepdims=True)
    pos_ref[0:1, :] = p0.astype(I32)
    pos_ref[1:2, :] = p1.astype(I32)
    cnt_ref[0] = cnt


def _mix(u, y_gla, x, w_pool, pool_scale, w_out, g1, sc2, sh2, wr_hi, wr_lo, b_router):
    bn, sn, d = x.shape
    ts = SEQ_TILE
    nt = sn // ts
    hpt = ts // POOL_HALO
    n_halo = sn // POOL_HALO
    u_halo = u.reshape(bn, n_halo, POOL_HALO, D_POOL)
    tok = lambda width: pl.BlockSpec((1, ts, width), lambda b, j: (b, j, 0))
    per_batch = pl.BlockSpec((1, 1, d), lambda b, j: (b, 0, 0))
    full = lambda a: pl.BlockSpec(a.shape, lambda b, j: (0,) * a.ndim)
    prev_halo = pl.BlockSpec((1, 1, POOL_HALO, D_POOL),
                             lambda b, j: (b, jnp.maximum(j * hpt - 1, 0), 0, 0))
    next_halo = pl.BlockSpec((1, 1, POOL_HALO, D_POOL),
                             lambda b, j: (b, jnp.minimum((j + 1) * hpt, n_halo - 1), 0, 0))
    lane_tok = lambda rows: pl.BlockSpec((rows, ts), lambda b, j: (0, b * nt + j))
    t = bn * sn
    return pl.pallas_call(
        functools.partial(_mix_kernel, seq_len=sn),
        grid=(bn, nt),
        in_specs=[tok(D_POOL), prev_halo, next_halo, tok(D_GLA_V), tok(d), full(w_pool), full(pool_scale),
                  full(w_out), per_batch, per_batch, per_batch, full(wr_hi), full(wr_lo), full(b_router)],
        out_specs=[tok(d), tok(d), lane_tok(2), lane_tok(2),
                   pl.BlockSpec((1, N_EXPERTS, 1), lambda b, j: (b * nt + j, 0, 0))],
        out_shape=[jax.ShapeDtypeStruct((bn, sn, d), F32), jax.ShapeDtypeStruct((bn, sn, d), BF16),
                   jax.ShapeDtypeStruct((2, t), I32), jax.ShapeDtypeStruct((2, t), F32),
                   jax.ShapeDtypeStruct((bn * nt, N_EXPERTS, 1), F32)],
        scratch_shapes=[pltpu.VMEM((ts + 3 * POOL_HALO, D_POOL), F32), pltpu.VMEM((ts, ts), BF16)],
        compiler_params=pltpu.CompilerParams(
            dimension_semantics=("arbitrary", "arbitrary"), vmem_limit_bytes=VMEM_LIMIT),
        name="mix_route",
    )(u, u_halo, u_halo, y_gla, x, w_pool, pool_scale, w_out, g1, sc2, sh2, wr_hi, wr_lo, b_router)


def _segment_copies(meta_ref, tile, make_copy, action):
    base = tile * META_PER_TILE
    for e in range(N_EXPERTS):
        n = pl.multiple_of(meta_ref[base + e], SEG_ALIGN)
        first_slot = pl.multiple_of(meta_ref[base + N_EXPERTS + e], SEG_ALIGN)
        first_row = pl.multiple_of(meta_ref[base + 2 * N_EXPERTS + e], SEG_ALIGN)

        @pl.when(n > 0)
        def _():
            action(make_copy(first_slot, first_row, n))


def _start(copy):
    copy.start()


def _wait(copy):
    copy.wait()


def _pack_rows(x):
    lo = lax.bitcast_convert_type(x[:, :D_PACK], U32)
    hi = lax.bitcast_convert_type(x[:, D_PACK:], U32)
    return hi | (lo >> 16)


def _unpack_rows(u):
    lo = lax.bitcast_convert_type(u << 16, F32)
    hi = lax.bitcast_convert_type(u & jnp.uint32(0xFFFF0000), F32)
    return lo, hi


def _dispatch_kernel(meta_ref, tail_ref, h_ref, pos_ref, xs_ref, buf, zbuf, sem):
    i = pl.program_id(0)
    ts = h_ref.shape[0]

    def copy(first_slot, first_row, n):
        return pltpu.make_async_copy(buf.at[pl.ds(first_slot, n)], xs_ref.at[pl.ds(first_row, n)], sem)

    slot = lax.broadcasted_iota(I32, (TILE_SLOTS, ts), 0)
    perm = jnp.where((slot == pos_ref[0:1, :]) | (slot == pos_ref[1:2, :]), 1.0, 0.0).astype(BF16)
    rows = _dot(perm, h_ref[...])

    @pl.when(i > 0)
    def _():
        _segment_copies(meta_ref, i - 1, copy, _wait)

    buf[...] = _pack_rows(rows)
    _segment_copies(meta_ref, i, copy, _start)

    @pl.when(i == pl.num_programs(0) - 1)
    def _():
        _segment_copies(meta_ref, i, copy, _wait)
        zbuf[...] = jnp.zeros_like(zbuf)

        def tail_copies(action):
            for e in range(N_EXPERTS):
                n = pl.multiple_of(tail_ref[e], SEG_ALIGN)
                first_row = pl.multiple_of(tail_ref[N_EXPERTS + e], SEG_ALIGN)

                @pl.when(n > 0)
                def _():
                    action(pltpu.make_async_copy(zbuf.at[pl.ds(0, n)], xs_ref.at[pl.ds(first_row, n)], sem))

        tail_copies(_start)
        tail_copies(_wait)

        def zero_block(blk, carry):
            first_row = pl.multiple_of(blk * MOE_BLOCK, MOE_BLOCK)
            fill = pltpu.make_async_copy(zbuf, xs_ref.at[pl.ds(first_row, MOE_BLOCK)], sem)
            fill.start()
            fill.wait()
            return carry

        lax.fori_loop(tail_ref[2 * N_EXPERTS], xs_ref.shape[0] // MOE_BLOCK, zero_block, 0)


def _dispatch(meta, tails, h, pos, n_rows_sorted):
    t, d = h.shape
    ts = SEQ_TILE
    return pl.pallas_call(
        _dispatch_kernel,
        grid_spec=pltpu.PrefetchScalarGridSpec(
            num_scalar_prefetch=2,
            grid=(t // ts,),
            in_specs=[pl.BlockSpec((ts, d), lambda i, m, tl: (i, 0)),
                      pl.BlockSpec((2, ts), lambda i, m, tl: (0, i))],
            out_specs=pl.BlockSpec(memory_space=pl.ANY),
            scratch_shapes=[pltpu.VMEM((TILE_SLOTS, D_PACK), U32), pltpu.VMEM((MOE_BLOCK, D_PACK), U32),
                            pltpu.SemaphoreType.DMA(())],
        ),
        out_shape=jax.ShapeDtypeStruct((n_rows_sorted, D_PACK), U32),
        compiler_params=pltpu.CompilerParams(
            dimension_semantics=("arbitrary",), vmem_limit_bytes=VMEM_LIMIT),
        name="dispatch",
    )(meta, tails, h, pos)


def _ffn_kernel(blk_e_ref, n_used_ref, xs_ref, w1_ref, w3_ref, w2_ref, ys_ref):
    del blk_e_ref

    @pl.when(pl.program_id(0) < n_used_ref[0])
    def _():
        lo, hi = _unpack_rows(xs_ref[...])
        xb = jnp.concatenate([lo.astype(BF16), hi.astype(BF16)], axis=1)
        a = _dot(xb, w1_ref[0, 0].astype(BF16))
        hid = (a * jax.nn.sigmoid(a)) * _dot(xb, w3_ref[0, 0].astype(BF16))
        y = _dot(hid.astype(BF16), w2_ref[0, 0].astype(BF16))
        ys_ref[...] = _pack_rows(y.astype(BF16).astype(F32))

    @pl.when(pl.program_id(0) >= n_used_ref[0])
    def _():
        ys_ref[...] = jnp.zeros_like(ys_ref)


def _ffn(layer, blk_e, n_used, xs, w1, w3, w2):
    rows = xs.shape[0]
    n_blocks = rows // MOE_BLOCK
    last = lambda i, n_used_ref: jnp.minimum(i, n_used_ref[0] - 1)
    row_spec = pl.BlockSpec((MOE_BLOCK, D_PACK), lambda i, be, nu: (last(i, nu), 0))
    w_spec = lambda a: pl.BlockSpec((1, 1) + a.shape[2:], lambda i, be, nu: (layer, be[last(i, nu)], 0, 0))
    return pl.pallas_call(
        _ffn_kernel,
        grid_spec=pltpu.PrefetchScalarGridSpec(
            num_scalar_prefetch=2,
            grid=(n_blocks,),
            in_specs=[row_spec, w_spec(w1), w_spec(w3), w_spec(w2)],
            out_specs=pl.BlockSpec((MOE_BLOCK, D_PACK), lambda i, be, nu: (i, 0)),
        ),
        out_shape=jax.ShapeDtypeStruct(xs.shape, U32),
        compiler_params=pltpu.CompilerParams(
            dimension_semantics=("arbitrary",), vmem_limit_bytes=VMEM_LIMIT),
        name="expert_ffn",
    )(blk_e, n_used, xs, w1, w3, w2)


def _combine_kernel(meta_ref, ys_ref, x_ref, posc_ref, rwc_ref, g2_ref, fg_ref, o_ref, buf, zbuf, sem,
                    *, final_norm):
    i = pl.program_id(0)
    ts = x_ref.shape[0]

    def fetch(tile, action):
        half = tile % 2

        def copy(first_slot, first_row, n):
            return pltpu.make_async_copy(ys_ref.at[pl.ds(first_row, n)],
                                         buf.at[half, pl.ds(first_slot, n)], sem.at[half])

        _segment_copies(meta_ref, tile, copy, action)
        last = tile * META_PER_TILE + N_EXPERTS - 1
        end = pl.multiple_of(meta_ref[last] + meta_ref[last + N_EXPERTS], SEG_ALIGN)
        n_free = pl.multiple_of(TILE_SLOTS - end, SEG_ALIGN)

        @pl.when(n_free > 0)
        def _():
            action(pltpu.make_async_copy(zbuf.at[pl.ds(0, n_free)], buf.at[half, pl.ds(end, n_free)],
                                         sem.at[half]))

    @pl.when(i == 0)
    def _():
        zbuf[...] = jnp.zeros_like(zbuf)
        fetch(i, _start)

    @pl.when(i + 1 < pl.num_programs(0))
    def _():
        fetch(i + 1, _start)

    fetch(i, _wait)
    lo, hi = _unpack_rows(buf[i % 2])
    rows = jnp.concatenate([lo.astype(BF16), hi.astype(BF16)], axis=1)
    slot = lax.broadcasted_iota(I32, (ts, TILE_SLOTS), 1)
    posc = posc_ref[...]
    rwc = rwc_ref[...]
    unperm = (jnp.where(slot == posc[:, 0:1], rwc[:, 0:1], 0.0)
              + jnp.where(slot == posc[:, 1:2], rwc[:, 1:2], 0.0)).astype(BF16)
    x = x_ref[...] + g2_ref[0] * _dot(unperm, rows)
    if final_norm:
        ms = jnp.mean(x * x, axis=-1, keepdims=True)
        x = x * lax.rsqrt(ms + EPS) * fg_ref[...]
    o_ref[...] = x


def _combine(meta, ys, x, pos, route_w, g2, final_g, seq_len, final_norm):
    t, d = x.shape
    ts = SEQ_TILE
    tiles_per_seq = seq_len // ts
    col_tok = pl.BlockSpec((ts, 2), lambda i, m: (i, 0))
    return pl.pallas_call(
        functools.partial(_combine_kernel, final_norm=final_norm),
        grid_spec=pltpu.PrefetchScalarGridSpec(
            num_scalar_prefetch=1,
            grid=(t // ts,),
            in_specs=[pl.BlockSpec(memory_space=pl.ANY),
                      pl.BlockSpec((ts, d), lambda i, m: (i, 0)),
                      col_tok, col_tok,
                      pl.BlockSpec((1, 1, d), lambda i, m: (i // tiles_per_seq, 0, 0)),
                      pl.BlockSpec((1, d), lambda i, m: (0, 0))],
            out_specs=pl.BlockSpec((ts, d), lambda i, m: (i, 0)),
            scratch_shapes=[pltpu.VMEM((2, TILE_SLOTS, D_PACK), U32),
                            pltpu.VMEM((TILE_SLOTS - 2 * ts, D_PACK), U32), pltpu.SemaphoreType.DMA((2,))],
        ),
        out_shape=jax.ShapeDtypeStruct((t, d), F32),
        compiler_params=pltpu.CompilerParams(
            dimension_semantics=("arbitrary",), vmem_limit_bytes=VMEM_LIMIT),
        name="combine",
    )(meta, ys, x, pos.T, route_w.T, g2, final_g.reshape(1, d))


def _dispatch_plan(tile_counts, n_assign):
    n_tiles = tile_counts.shape[0]
    seg = (tile_counts.astype(I32) + SEG_ALIGN - 1) // SEG_ALIGN * SEG_ALIGN
    totals = jnp.sum(seg, axis=0)
    padded = (totals + MOE_BLOCK - 1) // MOE_BLOCK * MOE_BLOCK
    pad_end = jnp.cumsum(padded)
    pad_start = pad_end - padded
    first_row = pad_start[None, :] + jnp.cumsum(seg, axis=0) - seg
    first_slot = jnp.cumsum(seg, axis=1) - seg
    meta = jnp.stack([seg, first_slot, first_row], axis=1).reshape(-1)
    max_rows = n_assign + n_tiles * N_EXPERTS * (SEG_ALIGN - 1)
    n_blocks = -(-max_rows // MOE_BLOCK) + N_EXPERTS
    blk_start = jnp.arange(n_blocks, dtype=I32) * MOE_BLOCK
    blk_e = jnp.minimum(jnp.sum((pad_end[None, :] <= blk_start[:, None]).astype(I32), axis=1), N_EXPERTS - 1)
    n_used = (pad_end[-1] // MOE_BLOCK).astype(I32).reshape(1)
    tails = jnp.concatenate([padded - totals, pad_start + totals, n_used])
    return meta.astype(I32), tails.astype(I32), blk_e.astype(I32), n_used, n_blocks * MOE_BLOCK


def kernel(x, c, w_mod, b_mod, w_in, w_pool, pool_scale, w_gk_up, b_gk, gla_norm_g, w_out,
           w_router, b_router, w1, w3, w2, final_g):
    bn, sn, d = x.shape
    t = bn * sn
    mod = _modulation(c, w_mod, b_mod).reshape(DEPTH, bn, 6, 1, d)

    w_main = w_in[:, :, :D_MAIN].astype(BF16)
    w_r = w_in[:, :, D_MAIN:].astype(BF16)
    zero_rank = jnp.zeros_like(w_gk_up[:, 0])
    w_gk = jnp.stack([jnp.concatenate([w_gk_up[:, 0], zero_rank], axis=1),
                      jnp.concatenate([zero_rank, w_gk_up[:, 1]], axis=1)], axis=1).astype(BF16)
    b_gk3 = b_gk.reshape(DEPTH, 2, 1, D_QK)
    w_pool_b = w_pool.astype(BF16)
    w_out_b = w_out.astype(BF16)
    wr_t = w_router.T
    wr_hi = wr_t.astype(BF16)
    wr_lo = (wr_t - wr_hi.astype(F32)).astype(BF16)
    br = b_router.reshape(N_EXPERTS, 1)

    for l in range(DEPTH):
        sh1, sc1, g1, sh2, sc2, g2 = (mod[l, :, i] for i in range(6))
        u, g, v, qf, kf, df, qb, kb, db, decf, decb = _inproj(
            x, sc1, sh1, w_main[l], w_r[l], w_gk[l], b_gk3[l])
        y_gla = _gla(qf, kf, df, qb, kb, db, v, decf, decb, g, gla_norm_g[l])
        x, h, pos, route_w, tile_counts = _mix(
            u, y_gla, x, w_pool_b[l], pool_scale[l].reshape(1, D_POOL), w_out_b[l], g1, sc2, sh2,
            wr_hi, wr_lo, br)
        meta, tails, blk_e, n_used, n_rows_sorted = _dispatch_plan(tile_counts[:, :, 0], 2 * t)
        xs = _dispatch(meta, tails, h.reshape(t, d), pos, n_rows_sorted)
        ys = _ffn(l, blk_e, n_used, xs, w1, w3, w2)
        x = _combine(meta, ys, x.reshape(t, d), pos, route_w, g2, final_g, sn, l == DEPTH - 1)
        x = x.reshape(bn, sn, d)
    return x
```

```python
import functools

import jax
import jax.numpy as jnp
from jax import lax
from jax.experimental import pallas as pl
from jax.experimental.pallas import tpu as pltpu

F32 = jnp.float32
BF16 = jnp.bfloat16
I32 = jnp.int32
U32 = jnp.uint32

D_MODEL = 1024
DEPTH = 2
D_POOL = 512
POOL_WINDOWS = (2, 4, 8, 16)
POOL_GROUP = 128
POOL_HALO = 16
D_GLA_V = 512
GLA_HEADS = 4
GLA_DK = 64
GLA_DV = 128
D_QK = GLA_HEADS * GLA_DK
GLA_RANK = 16
GATE_NORM = 16.0
GLA_CHUNK = 64
GLA_GROUP = 4
GLA_OUT_ROWS = 256
D_MAIN = D_POOL + 2 * D_QK + 2 * D_GLA_V
N_EXPERTS = 32
N_GROUPS = 4
EXPERTS_PER_GROUP = 8
D_FF = 512
EPS = 1e-6

SEQ_TILE = 512
MIX_ROWS = 256
INPROJ_TILE = 1024
CUMSUM_GROUP = 256
MOE_BLOCK = 1024
LANE = 128
SEG_ALIGN = 8
TILE_SLOTS = 2 * SEQ_TILE + 256
D_PACK = D_MODEL // 2
META_PER_TILE = 3 * N_EXPERTS
assert TILE_SLOTS >= 2 * SEQ_TILE + N_EXPERTS * (SEG_ALIGN - 1) and 2 * SEQ_TILE // SEG_ALIGN <= 256
VMEM_LIMIT = 56 * 1024 * 1024

NT_DIMS = (((1,), (1,)), ((), ()))
TN_DIMS = (((0,), (0,)), ((), ()))


def _dot(a, b):
    return jnp.dot(a, b, preferred_element_type=F32)


def _split_bf16(a):
    hi = a.astype(BF16)
    lo = (a - hi.astype(F32)).astype(BF16)
    return hi, lo


def _mod_kernel(c_ref, w_ref, b_ref, o_ref):
    c = c_ref[...]
    c_act = (c * jax.nn.sigmoid(c)).astype(BF16)
    o_ref[0] = _dot(c_act, w_ref[0].astype(BF16)) + b_ref[0]


def _modulation(c, w_mod, b_mod):
    n_layers, d, n_out = w_mod.shape
    bn = c.shape[0]
    tn = 1536
    return pl.pallas_call(
        _mod_kernel,
        grid=(n_layers, n_out // tn),
        in_specs=[
            pl.BlockSpec((bn, d), lambda l, j: (0, 0)),
            pl.BlockSpec((1, d, tn), lambda l, j: (l, 0, j)),
            pl.BlockSpec((1, 1, tn), lambda l, j: (l, 0, j)),
        ],
        out_specs=pl.BlockSpec((1, bn, tn), lambda l, j: (l, 0, j)),
        out_shape=jax.ShapeDtypeStruct((n_layers, bn, n_out), F32),
        compiler_params=pltpu.CompilerParams(
            dimension_semantics=("arbitrary", "arbitrary"), vmem_limit_bytes=VMEM_LIMIT),
        name="modulation",
    )(c, w_mod, b_mod.reshape(n_layers, 1, n_out))


def _log_sigmoid(x):
    return jnp.minimum(x, 0.0) - jnp.log1p(jnp.exp(-jnp.abs(x)))


def _inproj_kernel(x_ref, sc_ref, sh_ref, wmain_ref, wr_ref, wgk_ref, bgk_ref,
                   u_ref, g_ref, v_ref, qf_ref, kf_ref, df_ref, qb_ref, kb_ref, db_ref,
                   decf_ref, decb_ref):
    tile = x_ref.shape[1]
    sub = CUMSUM_GROUP
    sub_chunks = sub // GLA_CHUNK
    row = lax.broadcasted_iota(I32, (sub, sub), 0)
    col = lax.broadcasted_iota(I32, (sub, sub), 1)
    same_chunk = (row // GLA_CHUNK) == (col // GLA_CHUNK)
    prefix = jnp.where(same_chunk & (col <= row), 1.0, 0.0).astype(BF16)
    suffix = jnp.where(same_chunk & (col >= row), 1.0, 0.0).astype(BF16)

    groups = range(tile // sub)
    rows = [slice(s * sub, (s + 1) * sub) for s in groups]
    chunks = [slice(s * sub_chunks, (s + 1) * sub_chunks) for s in groups]
    dirs = ((0, prefix, GLA_CHUNK - 1, qf_ref, kf_ref, df_ref, decf_ref),
            (1, suffix, 0, qb_ref, kb_ref, db_ref, decb_ref))

    zs, rs = [], []
    for s in groups:
        x = x_ref[0, rows[s], :]
        ms = jnp.mean(x * x, axis=-1, keepdims=True)
        h = x * lax.rsqrt(ms + EPS) * (1.0 + sc_ref[0]) + sh_ref[0]
        hb = h.astype(BF16)
        zs.append(_dot(hb, wmain_ref[...]))
        rs.append(_dot(hb, wr_ref[...]).astype(BF16))

    logas = [[_log_sigmoid(_dot(rs[s], wgk_ref[idx]) + bgk_ref[idx]) / GATE_NORM for idx in range(2)]
             for s in groups]

    bs = []
    for s in groups:
        per_dir = []
        for idx, tri, *_ in dirs:
            hi, lo = _split_bf16(logas[s][idx])
            per_dir.append(_dot(tri, hi) + _dot(tri, lo))
        bs.append(per_dir)

    for s in groups:
        z = zs[s]
        u_ref[0, rows[s], :] = z[:, 0:D_POOL]
        q = z[:, D_POOL:D_POOL + D_QK] * (GLA_DK ** -0.5)
        k = z[:, D_POOL + D_QK:D_POOL + 2 * D_QK]
        v_ref[0, rows[s], :] = z[:, D_POOL + 2 * D_QK:D_POOL + 2 * D_QK + D_GLA_V].astype(BF16)
        g_ref[0, rows[s], :] = z[:, D_POOL + 2 * D_QK + D_GLA_V:D_MAIN]
        for idx, _, last_row, q_out, k_out, d_out, dec_out in dirs:
            b = bs[s][idx]
            b3 = b.reshape(sub_chunks, GLA_CHUNK, D_QK)
            total = b3[:, last_row:last_row + 1, :]
            q_out[0, rows[s], :] = (q * jnp.exp(b)).astype(BF16)
            k_out[0, rows[s], :] = (k * jnp.exp(-b)).astype(BF16)
            d_out[0, rows[s], :] = (k * jnp.exp(total - b3).reshape(sub, D_QK)).astype(BF16)
            dec_out[0, chunks[s], :] = jnp.exp(total).reshape(sub_chunks, D_QK)


def _inproj(x, sc, sh, w_main, w_r, w_gk, b_gk):
    bn, sn, d = x.shape
    ts = INPROJ_TILE
    n_chunks = ts // GLA_CHUNK
    tok = lambda width: pl.BlockSpec((1, ts, width), lambda b, j: (b, j, 0))
    per_batch = pl.BlockSpec((1, 1, d), lambda b, j: (b, 0, 0))
    full = lambda a: pl.BlockSpec(a.shape, lambda b, j: (0,) * a.ndim)
    dec_spec = pl.BlockSpec((1, n_chunks, D_QK), lambda b, j: (b, j, 0))
    tok_shape = lambda width, dt: jax.ShapeDtypeStruct((bn, sn, width), dt)
    dec_shape = jax.ShapeDtypeStruct((bn, sn // GLA_CHUNK, D_QK), F32)
    return pl.pallas_call(
        _inproj_kernel,
        grid=(bn, sn // ts),
        in_specs=[tok(d), per_batch, per_batch, full(w_main), full(w_r), full(w_gk), full(b_gk)],
        out_specs=[tok(D_POOL), tok(D_GLA_V), tok(D_GLA_V)] + [tok(D_QK)] * 6 + [dec_spec, dec_spec],
        out_shape=[tok_shape(D_POOL, F32), tok_shape(D_GLA_V, F32), tok_shape(D_GLA_V, BF16)]
        + [tok_shape(D_QK, BF16)] * 6 + [dec_shape, dec_shape],
        compiler_params=pltpu.CompilerParams(
            dimension_semantics=("arbitrary", "arbitrary"), vmem_limit_bytes=VMEM_LIMIT),
        name="inproj",
    )(x, sc, sh, w_main, w_r, w_gk, b_gk)


def _gla_kernel(qf_ref, kf_ref, df_ref, qb_ref, kb_ref, db_ref, v_ref, decf_ref, decb_ref,
                g_ref, ng_ref, y_ref, of_acc, ob_acc, state_f, state_b):
    sn = v_ref.shape[1]
    n_chunks = sn // GLA_CHUNK
    lane_head = lax.broadcasted_iota(I32, (GLA_CHUNK, D_QK), 1) // GLA_DK
    head_masks = [lane_head == hd for hd in range(GLA_HEADS)]
    state_blocks = (lax.broadcasted_iota(I32, (D_GLA_V, D_QK), 0) // GLA_DV
                    == lax.broadcasted_iota(I32, (D_GLA_V, D_QK), 1) // GLA_DK)
    def group_masks(own):
        masks = []
        for m in range(GLA_GROUP):
            r = lax.broadcasted_iota(I32, (GLA_CHUNK, GLA_CHUNK * (m + 1)), 0)
            c = lax.broadcasted_iota(I32, (GLA_CHUNK, GLA_CHUNK * (m + 1)), 1) - GLA_CHUNK * m
            masks.append((c < 0) | own(r, c))
        return masks

    causal = group_masks(lambda r, c: c <= r)
    anti = group_masks(lambda r, c: c > r)

    def scaled(a, factor):
        return a if factor is None else (a.astype(F32) * factor).astype(BF16)

    def product(factors):
        out = None
        for f in factors:
            out = f if out is None else out * f
        return out

    def group_out(first_chunk, order, q_ref, k_ref, d_ref, dec_ref, state, masks, o_acc):
        base = pl.multiple_of(first_chunk * GLA_CHUNK, GLA_GROUP * GLA_CHUNK)
        slab = pl.ds(base, GLA_GROUP * GLA_CHUNK)
        part = lambda a, c: a[c * GLA_CHUNK:(c + 1) * GLA_CHUNK]
        q_all, k_all, d_all, v_all = q_ref[0, slab, :], k_ref[0, slab, :], d_ref[0, slab, :], v_ref[0, slab, :]
        qe = [part(q_all, c) for c in order]
        ke = [part(k_all, c) for c in order]
        kd = [part(d_all, c) for c in order]
        vv = [part(v_all, c) for c in order]
        dec = [dec_ref[0, pl.ds(first_chunk + c, 1), :] for c in order]

        st = state[...]
        st_heads = jnp.where(state_blocks, st, 0.0).astype(BF16)
        q_in = jnp.concatenate([scaled(qe[m], product(dec[:m])) for m in range(GLA_GROUP)], axis=0)
        inter = lax.dot_general(q_in, st_heads, NT_DIMS, preferred_element_type=F32)
        k_out = jnp.concatenate([scaled(kd[c], product(dec[c + 1:])) for c in range(GLA_GROUP)], axis=0)
        upd = lax.dot_general(jnp.concatenate(vv, axis=0), k_out, TN_DIMS, preferred_element_type=F32)
        state[...] = st * product(dec) + upd

        scores = []
        for m in range(GLA_GROUP):
            keys = jnp.concatenate([scaled(kd[c], product(dec[c + 1:m])) for c in range(m)] + [ke[m]], axis=0)
            q_heads = jnp.concatenate([jnp.where(hm, qe[m], jnp.zeros_like(qe[m])) for hm in head_masks], axis=0)
            scores.append(lax.dot_general(q_heads, keys, NT_DIMS, preferred_element_type=F32))

        def finish():
            for m in range(GLA_GROUP):
                vals = jnp.concatenate(vv[:m + 1], axis=0)
                outs = []
                for hd in range(GLA_HEADS):
                    att = jnp.where(masks[m], scores[m][hd * GLA_CHUNK:(hd + 1) * GLA_CHUNK], 0.0).astype(BF16)
                    outs.append(_dot(att, vals[:, hd * GLA_DV:(hd + 1) * GLA_DV]))
                rows = pl.ds(base + order[m] * GLA_CHUNK, GLA_CHUNK)
                o_acc[rows, :] = jnp.concatenate(outs, axis=1) + part(inter, m)

        return finish

    state_f[...] = jnp.zeros_like(state_f)
    state_b[...] = jnp.zeros_like(state_b)
    n_groups = n_chunks // GLA_GROUP
    ascending = list(range(GLA_GROUP))

    def scan_body(i, carry):
        finish_f = group_out(i * GLA_GROUP, ascending, qf_ref, kf_ref, df_ref, decf_ref, state_f, causal, of_acc)
        finish_b = group_out((n_groups - 1 - i) * GLA_GROUP, ascending[::-1], qb_ref, kb_ref, db_ref, decb_ref,
                             state_b, anti, ob_acc)
        finish_f()
        finish_b()
        return carry

    lax.fori_loop(0, n_groups, scan_body, 0)

    norm_g = ng_ref[...]

    def out_body(s, carry):
        rows = pl.ds(pl.multiple_of(s * GLA_OUT_ROWS, GLA_OUT_ROWS), GLA_OUT_ROWS)
        o = of_acc[rows, :] + ob_acc[rows, :]
        gate = g_ref[0, rows, :]
        gate = gate * jax.nn.sigmoid(gate)
        outs = []
        for hd in range(GLA_HEADS):
            oh = o[:, hd * GLA_DV:(hd + 1) * GLA_DV]
            oh = oh * lax.rsqrt(jnp.mean(oh * oh, axis=-1, keepdims=True) + EPS) * norm_g
            outs.append(oh)
        y_ref[0, rows, :] = (jnp.concatenate(outs, axis=1) * gate).astype(BF16)
        return carry

    lax.fori_loop(0, sn // GLA_OUT_ROWS, out_body, 0)


def _gla(qf, kf, df, qb, kb, db, v, decf, decb, g, norm_g):
    bn, sn, _ = v.shape
    seq = lambda a: pl.BlockSpec((1,) + a.shape[1:], lambda b: (b, 0, 0))
    args = (qf, kf, df, qb, kb, db, v, decf, decb, g)
    return pl.pallas_call(
        _gla_kernel,
        grid=(bn,),
        in_specs=[seq(a) for a in args] + [pl.BlockSpec((1, GLA_DV), lambda b: (0, 0))],
        out_specs=pl.BlockSpec((1, sn, D_GLA_V), lambda b: (b, 0, 0)),
        out_shape=jax.ShapeDtypeStruct((bn, sn, D_GLA_V), BF16),
        scratch_shapes=[pltpu.VMEM((sn, D_GLA_V), F32), pltpu.VMEM((sn, D_GLA_V), F32),
                        pltpu.VMEM((D_GLA_V, D_QK), F32), pltpu.VMEM((D_GLA_V, D_QK), F32)],
        compiler_params=pltpu.CompilerParams(
            dimension_semantics=("arbitrary",), vmem_limit_bytes=VMEM_LIMIT),
        name="gla",
    )(*args, norm_g.reshape(1, GLA_DV))


def _mix_kernel(u_ref, up_ref, un_ref, yg_ref, x_ref, wp_ref, ps_ref, wo_ref, g1_ref, sc_ref, sh_ref,
                wrh_ref, wrl_ref, br_ref,
                xo_ref, h_ref, pos_ref, rw_ref, cnt_ref, ext_ref, earlier_ref, *, seq_len):
    j = pl.program_id(1)
    ts = u_ref.shape[1]
    sub = MIX_ROWS
    groups = range(ts // sub)
    rows = [slice(s * sub, (s + 1) * sub) for s in groups]

    @pl.when((pl.program_id(0) == 0) & (j == 0))
    def _():
        tr = lax.broadcasted_iota(I32, (ts, ts), 0)
        tc = lax.broadcasted_iota(I32, (ts, ts), 1)
        earlier_ref[...] = jnp.where(tr < tc, 1.0, 0.0).astype(BF16)
        ext_ref[ts + 2 * POOL_HALO:, :] = jnp.zeros((POOL_HALO, D_POOL), F32)

    prev_ok = jnp.where(j > 0, 1.0, 0.0)
    next_ok = jnp.where(j < pl.num_programs(1) - 1, 1.0, 0.0)
    ext_ref[0:POOL_HALO, :] = up_ref[0, 0] * prev_ok
    ext_ref[POOL_HALO:POOL_HALO + ts, :] = u_ref[0]
    ext_ref[POOL_HALO + ts:POOL_HALO + ts + POOL_HALO, :] = un_ref[0, 0] * next_ok
    pos = j * ts + lax.broadcasted_iota(I32, (ts, 1), 0)

    def window_sum(lanes, w):
        if w == 2:
            return ext_ref[POOL_HALO - 1:POOL_HALO - 1 + ts, lanes] + ext_ref[POOL_HALO:POOL_HALO + ts, lanes]
        first = SEG_ALIGN
        n = ts + SEG_ALIGN * (w.bit_length() - 2)
        e = ext_ref[first:first + n + SEG_ALIGN, lanes]
        p = e[0:n] + e[1:n + 1]
        step = 2
        while step < w // 2:
            n -= SEG_ALIGN
            p = p[0:n] + p[step:step + n]
            step *= 2
        off = POOL_HALO - w // 2 - first
        return p[off:off + ts] + p[off + w // 2:off + w // 2 + ts]

    pooled = []
    for gi, w in enumerate(POOL_WINDOWS):
        lanes = slice(gi * POOL_GROUP, (gi + 1) * POOL_GROUP)
        acc = window_sum(lanes, w)
        tok = u_ref[0, :, lanes]

        def clipped(edge):
            lo = jnp.clip(pos[edge] - w // 2, 0, seq_len - 1)
            hi = jnp.clip(pos[edge] - w // 2 + w - 1, 0, seq_len - 1)
            return acc[edge] / (hi - lo + 1).astype(F32) - tok[edge]

        inner = slice(POOL_HALO, ts - POOL_HALO)
        pooled.append(jnp.concatenate(
            [clipped(slice(0, POOL_HALO)), acc[inner] * (1.0 / w) - tok[inner], clipped(slice(ts - POOL_HALO, ts))],
            axis=0).astype(BF16))

    ys = []
    for s in groups:
        y_pool = [(_dot(pooled[gi][rows[s]], wp_ref[gi])
                   * ps_ref[:, gi * POOL_GROUP:(gi + 1) * POOL_GROUP]).astype(BF16)
                  for gi in range(len(POOL_WINDOWS))]
        ys.append(jnp.concatenate(y_pool + [yg_ref[0, rows[s], :]], axis=1))
    mixes = [_dot(ys[s], wo_ref[...]) for s in groups]

    logits = []
    for s in groups:
        x = x_ref[0, rows[s], :] + g1_ref[0] * mixes[s]
        xo_ref[0, rows[s], :] = x
        ms = jnp.mean(x * x, axis=-1, keepdims=True)
        h = x * lax.rsqrt(ms + EPS) * (1.0 + sc_ref[0]) + sh_ref[0]
        h_ref[0, rows[s], :] = h.astype(BF16)
        h_hi, h_lo = _split_bf16(h)
        logits.append(lax.dot_general(wrh_ref[...], h_hi, NT_DIMS, preferred_element_type=F32)
                      + lax.dot_general(wrh_ref[...], h_lo, NT_DIMS, preferred_element_type=F32)
                      + lax.dot_general(wrl_ref[...], h_hi, NT_DIMS, preferred_element_type=F32))

    score = jax.nn.sigmoid(jnp.concatenate(logits, axis=1))
    sel = (score + br_ref[...]).reshape(N_GROUPS, EXPERTS_PER_GROUP, ts)
    local = lax.broadcasted_iota(I32, sel.shape, 1)
    big = EXPERTS_PER_GROUP
    m1 = jnp.max(sel, axis=1, keepdims=True)
    i1 = jnp.min(jnp.where(sel == m1, local, big), axis=1, keepdims=True)
    rest = jnp.where(local == i1, -jnp.inf, sel)
    m2 = jnp.max(rest, axis=1, keepdims=True)
    i2 = jnp.min(jnp.where(rest == m2, local, big), axis=1, keepdims=True)
    gscore = m1 + m2
    gid = lax.broadcasted_iota(I32, gscore.shape, 0)
    gbest = jnp.max(gscore, axis=0, keepdims=True)
    gsel = jnp.min(jnp.where(gscore == gbest, gid, N_GROUPS), axis=0, keepdims=True)
    in_group = gid == gsel
    pick0 = (in_group & (local == i1)).reshape(N_EXPERTS, ts)
    pick1 = (in_group & (local == i2)).reshape(N_EXPERTS, ts)
    s0 = jnp.sum(jnp.where(pick0, score, 0.0), axis=0, keepdims=True)
    s1 = jnp.sum(jnp.where(pick1, score, 0.0), axis=0, keepdims=True)
    rw_ref[0:1, :] = s0 / (s0 + s1)
    rw_ref[1:2, :] = s1 / (s0 + s1)

    onehot = jnp.where(pick0 | pick1, 1.0, 0.0).astype(BF16)
    er = lax.broadcasted_iota(I32, (N_EXPERTS, N_EXPERTS), 0)
    ec = lax.broadcasted_iota(I32, (N_EXPERTS, N_EXPERTS), 1)
    lower_exp = jnp.where(ec < er, 1.0, 0.0).astype(BF16)
    same_before = _dot(onehot, earlier_ref[...])
    cnt = jnp.sum(onehot.astype(F32), axis=1, keepdims=True)
    seg = jnp.ceil(cnt * (1.0 / SEG_ALIGN)) * SEG_ALIGN
    seg_lanes = jnp.broadcast_to(seg, (N_EXPERTS, LANE)).astype(BF16)
    first_slot = _dot(lower_exp, seg_lanes)[:, 0:1]
    slot = same_before + first_slot
    p0 = jnp.sum(jnp.where(pick0, slot, 0.0), axis=0, keepdims=True)
    p1 = jnp.sum(jnp.where(pick1, slot, 0.0), axis=0, keepdims=True)
    pos_ref[0:1, :] = p0.astype(I32)
    pos_ref[1:2, :] = p1.astype(I32)
    cnt_ref[0] = cnt


def _mix(u, y_gla, x, w_pool, pool_scale, w_out, g1, sc2, sh2, wr_hi, wr_lo, b_router):
    bn, sn, d = x.shape
    ts = SEQ_TILE
    nt = sn // ts
    hpt = ts // POOL_HALO
    n_halo = sn // POOL_HALO
    u_halo = u.reshape(bn, n_halo, POOL_HALO, D_POOL)
    tok = lambda width: pl.BlockSpec((1, ts, width), lambda b, j: (b, j, 0))
    per_batch = pl.BlockSpec((1, 1, d), lambda b, j: (b, 0, 0))
    full = lambda a: pl.BlockSpec(a.shape, lambda b, j: (0,) * a.ndim)
    prev_halo = pl.BlockSpec((1, 1, POOL_HALO, D_POOL),
                             lambda b, j: (b, jnp.maximum(j * hpt - 1, 0), 0, 0))
    next_halo = pl.BlockSpec((1, 1, POOL_HALO, D_POOL),
                             lambda b, j: (b, jnp.minimum((j + 1) * hpt, n_halo - 1), 0, 0))
    lane_tok = lambda rows: pl.BlockSpec((rows, ts), lambda b, j: (0, b * nt + j))
    t = bn * sn
    return pl.pallas_call(
        functools.partial(_mix_kernel, seq_len=sn),
        grid=(bn, nt),
        in_specs=[tok(D_POOL), prev_halo, next_halo, tok(D_GLA_V), tok(d), full(w_pool), full(pool_scale),
                  full(w_out), per_batch, per_batch, per_batch, full(wr_hi), full(wr_lo), full(b_router)],
        out_specs=[tok(d), tok(d), lane_tok(2), lane_tok(2),
                   pl.BlockSpec((1, N_EXPERTS, 1), lambda b, j: (b * nt + j, 0, 0))],
        out_shape=[jax.ShapeDtypeStruct((bn, sn, d), F32), jax.ShapeDtypeStruct((bn, sn, d), BF16),
                   jax.ShapeDtypeStruct((2, t), I32), jax.ShapeDtypeStruct((2, t), F32),
                   jax.ShapeDtypeStruct((bn * nt, N_EXPERTS, 1), F32)],
        scratch_shapes=[pltpu.VMEM((ts + 3 * POOL_HALO, D_POOL), F32), pltpu.VMEM((ts, ts), BF16)],
        compiler_params=pltpu.CompilerParams(
            dimension_semantics=("arbitrary", "arbitrary"), vmem_limit_bytes=VMEM_LIMIT),
        name="mix_route",
    )(u, u_halo, u_halo, y_gla, x, w_pool, pool_scale, w_out, g1, sc2, sh2, wr_hi, wr_lo, b_router)


def _segment_copies(meta_ref, tile, make_copy, action):
    base = tile * META_PER_TILE
    for e in range(N_EXPERTS):
        n = pl.multiple_of(meta_ref[base + e], SEG_ALIGN)
        first_slot = pl.multiple_of(meta_ref[base + N_EXPERTS + e], SEG_ALIGN)
        first_row = pl.multiple_of(meta_ref[base + 2 * N_EXPERTS + e], SEG_ALIGN)

        @pl.when(n > 0)
        def _():
            action(make_copy(first_slot, first_row, n))


def _tile_slots_used(meta_ref, tile):
    last = tile * META_PER_TILE + N_EXPERTS - 1
    return pl.multiple_of(meta_ref[last] + meta_ref[last + N_EXPERTS], SEG_ALIGN)


def _start(copy):
    copy.start()


def _wait(copy):
    copy.wait()


def _pack_rows(x):
    lo = lax.bitcast_convert_type(x[:, :D_PACK], U32)
    hi = lax.bitcast_convert_type(x[:, D_PACK:], U32)
    return hi | (lo >> 16)


def _unpack_rows(u):
    lo = lax.bitcast_convert_type(u << 16, F32)
    hi = lax.bitcast_convert_type(u & jnp.uint32(0xFFFF0000), F32)
    return lo, hi


def _dispatch_kernel(meta_ref, tail_ref, h_ref, pos_ref, xs_ref, buf, zbuf, sems, sem):
    i = pl.program_id(0)
    ts = h_ref.shape[0]

    def tile_copy(tile):
        half = tile % 2

        def copy(first_slot, first_row, n):
            return pltpu.make_async_copy(buf.at[half, pl.ds(first_slot, n)], xs_ref.at[pl.ds(first_row, n)],
                                         sems.at[half])
        return copy

    def wait_tile(tile):
        tile_copy(tile)(0, 0, _tile_slots_used(meta_ref, tile)).wait()

    slot = lax.broadcasted_iota(I32, (TILE_SLOTS, ts), 0)
    perm = jnp.where((slot == pos_ref[0:1, :]) | (slot == pos_ref[1:2, :]), 1.0, 0.0).astype(BF16)
    rows = _dot(perm, h_ref[...])

    @pl.when(i > 1)
    def _():
        wait_tile(i - 2)

    buf[i % 2] = _pack_rows(rows)
    _segment_copies(meta_ref, i, tile_copy(i), _start)

    @pl.when(i == pl.num_programs(0) - 1)
    def _():
        @pl.when(i > 0)
        def _():
            wait_tile(i - 1)

        wait_tile(i)
        zbuf[...] = jnp.zeros_like(zbuf)

        def tail_copies(action):
            for e in range(N_EXPERTS):
                n = pl.multiple_of(tail_ref[e], SEG_ALIGN)
                first_row = pl.multiple_of(tail_ref[N_EXPERTS + e], SEG_ALIGN)

                @pl.when(n > 0)
                def _():
                    action(pltpu.make_async_copy(zbuf.at[pl.ds(0, n)], xs_ref.at[pl.ds(first_row, n)], sem))

        tail_copies(_start)
        tail_copies(_wait)

        def zero_block(blk, carry):
            first_row = pl.multiple_of(blk * MOE_BLOCK, MOE_BLOCK)
            fill = pltpu.make_async_copy(zbuf, xs_ref.at[pl.ds(first_row, MOE_BLOCK)], sem)
            fill.start()
            fill.wait()
            return carry

        lax.fori_loop(tail_ref[2 * N_EXPERTS], xs_ref.shape[0] // MOE_BLOCK, zero_block, 0)


def _dispatch(meta, tails, h, pos, n_rows_sorted):
    t, d = h.shape
    ts = SEQ_TILE
    return pl.pallas_call(
        _dispatch_kernel,
        grid_spec=pltpu.PrefetchScalarGridSpec(
            num_scalar_prefetch=2,
            grid=(t // ts,),
            in_specs=[pl.BlockSpec((ts, d), lambda i, m, tl: (i, 0)),
                      pl.BlockSpec((2, ts), lambda i, m, tl: (0, i))],
            out_specs=pl.BlockSpec(memory_space=pl.ANY),
            scratch_shapes=[pltpu.VMEM((2, TILE_SLOTS, D_PACK), U32), pltpu.VMEM((MOE_BLOCK, D_PACK), U32),
                            pltpu.SemaphoreType.DMA((2,)), pltpu.SemaphoreType.DMA(())],
        ),
        out_shape=jax.ShapeDtypeStruct((n_rows_sorted, D_PACK), U32),
        compiler_params=pltpu.CompilerParams(
            dimension_semantics=("arbitrary",), vmem_limit_bytes=VMEM_LIMIT),
        name="dispatch",
    )(meta, tails, h, pos)


def _ffn_kernel(blk_e_ref, n_used_ref, xs_ref, w1_ref, w3_ref, w2_ref, ys_ref):
    del blk_e_ref

    @pl.when(pl.program_id(0) < n_used_ref[0])
    def _():
        lo, hi = _unpack_rows(xs_ref[...])
        xb = jnp.concatenate([lo.astype(BF16), hi.astype(BF16)], axis=1)
        a = _dot(xb, w1_ref[0, 0].astype(BF16))
        hid = (a * jax.nn.sigmoid(a)) * _dot(xb, w3_ref[0, 0].astype(BF16))
        y = _dot(hid.astype(BF16), w2_ref[0, 0].astype(BF16))
        ys_ref[...] = _pack_rows(y.astype(BF16).astype(F32))

    @pl.when(pl.program_id(0) >= n_used_ref[0])
    def _():
        ys_ref[...] = jnp.zeros_like(ys_ref)


def _ffn(layer, blk_e, n_used, xs, w1, w3, w2):
    rows = xs.shape[0]
    n_blocks = rows // MOE_BLOCK
    last = lambda i, n_used_ref: jnp.minimum(i, n_used_ref[0] - 1)
    row_spec = pl.BlockSpec((MOE_BLOCK, D_PACK), lambda i, be, nu: (last(i, nu), 0))
    w_spec = lambda a: pl.BlockSpec((1, 1) + a.shape[2:], lambda i, be, nu: (layer, be[last(i, nu)], 0, 0))
    return pl.pallas_call(
        _ffn_kernel,
        grid_spec=pltpu.PrefetchScalarGridSpec(
            num_scalar_prefetch=2,
            grid=(n_blocks,),
            in_specs=[row_spec, w_spec(w1), w_spec(w3), w_spec(w2)],
            out_specs=pl.BlockSpec((MOE_BLOCK, D_PACK), lambda i, be, nu: (i, 0)),
        ),
        out_shape=jax.ShapeDtypeStruct(xs.shape, U32),
        compiler_params=pltpu.CompilerParams(
            dimension_semantics=("arbitrary",), vmem_limit_bytes=VMEM_LIMIT),
        name="expert_ffn",
    )(blk_e, n_used, xs, w1, w3, w2)


def _combine_kernel(meta_ref, ys_ref, x_ref, posc_ref, rwc_ref, g2_ref, fg_ref, o_ref, buf, zbuf, sem,
                    *, final_norm):
    i = pl.program_id(0)
    ts = x_ref.shape[0]

    def fetch(tile):
        half = tile % 2

        def copy(first_slot, first_row, n):
            return pltpu.make_async_copy(ys_ref.at[pl.ds(first_row, n)],
                                         buf.at[half, pl.ds(first_slot, n)], sem.at[half])

        _segment_copies(meta_ref, tile, copy, _start)
        end = _tile_slots_used(meta_ref, tile)
        n_free = pl.multiple_of(TILE_SLOTS - end, SEG_ALIGN)

        @pl.when(n_free > 0)
        def _():
            pltpu.make_async_copy(zbuf.at[pl.ds(0, n_free)], buf.at[half, pl.ds(end, n_free)],
                                  sem.at[half]).start()

    @pl.when(i == 0)
    def _():
        zbuf[...] = jnp.zeros_like(zbuf)
        fetch(i)

    @pl.when(i + 1 < pl.num_programs(0))
    def _():
        fetch(i + 1)

    pltpu.make_async_copy(ys_ref.at[pl.ds(0, TILE_SLOTS)], buf.at[i % 2], sem.at[i % 2]).wait()
    lo, hi = _unpack_rows(buf[i % 2])
    rows = jnp.concatenate([lo.astype(BF16), hi.astype(BF16)], axis=1)
    slot = lax.broadcasted_iota(I32, (ts, TILE_SLOTS), 1)
    posc = posc_ref[...]
    rwc = rwc_ref[...]
    unperm = (jnp.where(slot == posc[:, 0:1], rwc[:, 0:1], 0.0)
              + jnp.where(slot == posc[:, 1:2], rwc[:, 1:2], 0.0)).astype(BF16)
    x = x_ref[...] + g2_ref[0] * _dot(unperm, rows)
    if final_norm:
        ms = jnp.mean(x * x, axis=-1, keepdims=True)
        x = x * lax.rsqrt(ms + EPS) * fg_ref[...]
    o_ref[...] = x


def _combine(meta, ys, x, pos, route_w, g2, final_g, seq_len, final_norm):
    t, d = x.shape
    ts = SEQ_TILE
    tiles_per_seq = seq_len // ts
    col_tok = pl.BlockSpec((ts, 2), lambda i, m: (i, 0))
    return pl.pallas_call(
        functools.partial(_combine_kernel, final_norm=final_norm),
        grid_spec=pltpu.PrefetchScalarGridSpec(
            num_scalar_prefetch=1,
            grid=(t // ts,),
            in_specs=[pl.BlockSpec(memory_space=pl.ANY),
                      pl.BlockSpec((ts, d), lambda i, m: (i, 0)),
                      col_tok, col_tok,
                      pl.BlockSpec((1, 1, d), lambda i, m: (i // tiles_per_seq, 0, 0)),
                      pl.BlockSpec((1, d), lambda i, m: (0, 0))],
            out_specs=pl.BlockSpec((ts, d), lambda i, m: (i, 0)),
            scratch_shapes=[pltpu.VMEM((2, TILE_SLOTS, D_PACK), U32),
                            pltpu.VMEM((TILE_SLOTS - 2 * ts, D_PACK), U32), pltpu.SemaphoreType.DMA((2,))],
        ),
        out_shape=jax.ShapeDtypeStruct((t, d), F32),
        compiler_params=pltpu.CompilerParams(
            dimension_semantics=("arbitrary",), vmem_limit_bytes=VMEM_LIMIT),
        name="combine",
    )(meta, ys, x, pos.T, route_w.T, g2, final_g.reshape(1, d))


def _dispatch_plan(tile_counts, n_assign):
    n_tiles = tile_counts.shape[0]
    seg = (tile_counts.astype(I32) + SEG_ALIGN - 1) // SEG_ALIGN * SEG_ALIGN
    totals = jnp.sum(seg, axis=0)
    padded = (totals + MOE_BLOCK - 1) // MOE_BLOCK * MOE_BLOCK
    pad_end = jnp.cumsum(padded)
    pad_start = pad_end - padded
    first_row = pad_start[None, :] + jnp.cumsum(seg, axis=0) - seg
    first_slot = jnp.cumsum(seg, axis=1) - seg
    meta = jnp.stack([seg, first_slot, first_row], axis=1).reshape(-1)
    max_rows = n_assign + n_tiles * N_EXPERTS * (SEG_ALIGN - 1)
    n_blocks = -(-max_rows // MOE_BLOCK) + N_EXPERTS
    blk_start = jnp.arange(n_blocks, dtype=I32) * MOE_BLOCK
    blk_e = jnp.minimum(jnp.sum((pad_end[None, :] <= blk_start[:, None]).astype(I32), axis=1), N_EXPERTS - 1)
    n_used = (pad_end[-1] // MOE_BLOCK).astype(I32).reshape(1)
    tails = jnp.concatenate([padded - totals, pad_start + totals, n_used])
    return meta.astype(I32), tails.astype(I32), blk_e.astype(I32), n_used, n_blocks * MOE_BLOCK


def kernel(x, c, w_mod, b_mod, w_in, w_pool, pool_scale, w_gk_up, b_gk, gla_norm_g, w_out,
           w_router, b_router, w1, w3, w2, final_g):
    bn, sn, d = x.shape
    t = bn * sn
    mod = _modulation(c, w_mod, b_mod).reshape(DEPTH, bn, 6, 1, d)

    w_main = w_in[:, :, :D_MAIN].astype(BF16)
    w_r = w_in[:, :, D_MAIN:].astype(BF16)
    zero_rank = jnp.zeros_like(w_gk_up[:, 0])
    w_gk = jnp.stack([jnp.concatenate([w_gk_up[:, 0], zero_rank], axis=1),
                      jnp.concatenate([zero_rank, w_gk_up[:, 1]], axis=1)], axis=1).astype(BF16)
    b_gk3 = b_gk.reshape(DEPTH, 2, 1, D_QK)
    w_pool_b = w_pool.astype(BF16)
    w_out_b = w_out.astype(BF16)
    wr_t = w_router.T
    wr_hi = wr_t.astype(BF16)
    wr_lo = (wr_t - wr_hi.astype(F32)).astype(BF16)
    br = b_router.reshape(N_EXPERTS, 1)

    for l in range(DEPTH):
        sh1, sc1, g1, sh2, sc2, g2 = (mod[l, :, i] for i in range(6))
        u, g, v, qf, kf, df, qb, kb, db, decf, decb = _inproj(
            x, sc1, sh1, w_main[l], w_r[l], w_gk[l], b_gk3[l])
        y_gla = _gla(qf, kf, df, qb, kb, db, v, decf, decb, g, gla_norm_g[l])
        x, h, pos, route_w, tile_counts = _mix(
            u, y_gla, x, w_pool_b[l], pool_scale[l].reshape(1, D_POOL), w_out_b[l], g1, sc2, sh2,
            wr_hi, wr_lo, br)
        meta, tails, blk_e, n_used, n_rows_sorted = _dispatch_plan(tile_counts[:, :, 0], 2 * t)
        xs = _dispatch(meta, tails, h.reshape(t, d), pos, n_rows_sorted)
        ys = _ffn(l, blk_e, n_used, xs, w1, w3, w2)
        x = _combine(meta, ys, x.reshape(t, d), pos, route_w, g2, final_g, sn, l == DEPTH - 1)
        x = x.reshape(bn, sn, d)
    return x
```

```python
import functools

import jax
import jax.numpy as jnp
from jax import lax
from jax.experimental import pallas as pl
from jax.experimental.pallas import tpu as pltpu

F32 = jnp.float32
BF16 = jnp.bfloat16
I32 = jnp.int32
U32 = jnp.uint32

D_MODEL = 1024
DEPTH = 2
D_POOL = 512
POOL_WINDOWS = (2, 4, 8, 16)
POOL_GROUP = 128
POOL_HALO = 16
D_GLA_V = 512
GLA_HEADS = 4
GLA_DK = 64
GLA_DV = 128
D_QK = GLA_HEADS * GLA_DK
GLA_RANK = 16
GATE_NORM = 16.0
GLA_CHUNK = 64
GLA_GROUP = 4
GLA_OUT_ROWS = 256
D_MAIN = D_POOL + 2 * D_QK + 2 * D_GLA_V
N_EXPERTS = 32
N_GROUPS = 4
EXPERTS_PER_GROUP = 8
D_FF = 512
EPS = 1e-6

SEQ_TILE = 512
MIX_ROWS = 256
INPROJ_TILE = 1024
CUMSUM_GROUP = 256
MOE_BLOCK = 1024
LANE = 128
SEG_ALIGN = 8
TILE_SLOTS = 2 * SEQ_TILE + 256
D_PACK = D_MODEL // 2
META_PER_TILE = 3 * N_EXPERTS
assert TILE_SLOTS >= 2 * SEQ_TILE + N_EXPERTS * (SEG_ALIGN - 1) and 2 * SEQ_TILE // SEG_ALIGN <= 256
VMEM_LIMIT = 56 * 1024 * 1024

NT_DIMS = (((1,), (1,)), ((), ()))
TN_DIMS = (((0,), (0,)), ((), ()))


def _dot(a, b):
    return jnp.dot(a, b, preferred_element_type=F32)


def _split_bf16(a):
    hi = a.astype(BF16)
    lo = (a - hi.astype(F32)).astype(BF16)
    return hi, lo


def _mod_kernel(c_ref, w_ref, b_ref, o_ref):
    c = c_ref[...]
    c_act = (c * jax.nn.sigmoid(c)).astype(BF16)
    o_ref[0] = _dot(c_act, w_ref[0].astype(BF16)) + b_ref[0]


def _modulation(c, w_mod, b_mod):
    n_layers, d, n_out = w_mod.shape
    bn = c.shape[0]
    tn = 1536
    return pl.pallas_call(
        _mod_kernel,
        grid=(n_layers, n_out // tn),
        in_specs=[
            pl.BlockSpec((bn, d), lambda l, j: (0, 0)),
            pl.BlockSpec((1, d, tn), lambda l, j: (l, 0, j)),
            pl.BlockSpec((1, 1, tn), lambda l, j: (l, 0, j)),
        ],
        out_specs=pl.BlockSpec((1, bn, tn), lambda l, j: (l, 0, j)),
        out_shape=jax.ShapeDtypeStruct((n_layers, bn, n_out), F32),
        compiler_params=pltpu.CompilerParams(
            dimension_semantics=("arbitrary", "arbitrary"), vmem_limit_bytes=VMEM_LIMIT),
        name="modulation",
    )(c, w_mod, b_mod.reshape(n_layers, 1, n_out))


def _log_sigmoid(x):
    return jnp.minimum(x, 0.0) - jnp.log1p(jnp.exp(-jnp.abs(x)))


def _inproj_kernel(x_ref, sc_ref, sh_ref, wmain_ref, wr_ref, wgk_ref, bgk_ref,
                   u_ref, g_ref, v_ref, qf_ref, kf_ref, df_ref, qb_ref, kb_ref, db_ref,
                   decf_ref, decb_ref):
    tile = x_ref.shape[1]
    sub = CUMSUM_GROUP
    sub_chunks = sub // GLA_CHUNK
    row = lax.broadcasted_iota(I32, (sub, sub), 0)
    col = lax.broadcasted_iota(I32, (sub, sub), 1)
    same_chunk = (row // GLA_CHUNK) == (col // GLA_CHUNK)
    prefix = jnp.where(same_chunk & (col <= row), 1.0, 0.0).astype(BF16)
    suffix = jnp.where(same_chunk & (col >= row), 1.0, 0.0).astype(BF16)

    groups = range(tile // sub)
    rows = [slice(s * sub, (s + 1) * sub) for s in groups]
    chunks = [slice(s * sub_chunks, (s + 1) * sub_chunks) for s in groups]
    dirs = ((0, prefix, GLA_CHUNK - 1, qf_ref, kf_ref, df_ref, decf_ref),
            (1, suffix, 0, qb_ref, kb_ref, db_ref, decb_ref))

    zs, rs = [], []
    for s in groups:
        x = x_ref[0, rows[s], :]
        ms = jnp.mean(x * x, axis=-1, keepdims=True)
        h = x * lax.rsqrt(ms + EPS) * (1.0 + sc_ref[0]) + sh_ref[0]
        hb = h.astype(BF16)
        zs.append(_dot(hb, wmain_ref[...]))
        rs.append(lax.dot_general(wr_ref[...], hb, NT_DIMS, preferred_element_type=F32).astype(BF16))

    logas = [[_log_sigmoid(lax.dot_general(rs[s], wgk_ref[idx], TN_DIMS, preferred_element_type=F32)
                           + bgk_ref[idx]) / GATE_NORM for idx in range(2)]
             for s in groups]

    bs = []
    for s in groups:
        per_dir = []
        for idx, tri, *_ in dirs:
            hi, lo = _split_bf16(logas[s][idx])
            per_dir.append(_dot(tri, hi) + _dot(tri, lo))
        bs.append(per_dir)

    for s in groups:
        z = zs[s]
        u_ref[0, rows[s], :] = z[:, 0:D_POOL]
        q = z[:, D_POOL:D_POOL + D_QK] * (GLA_DK ** -0.5)
        k = z[:, D_POOL + D_QK:D_POOL + 2 * D_QK]
        v_ref[0, rows[s], :] = z[:, D_POOL + 2 * D_QK:D_POOL + 2 * D_QK + D_GLA_V].astype(BF16)
        g_ref[0, rows[s], :] = z[:, D_POOL + 2 * D_QK + D_GLA_V:D_MAIN]
        for idx, _, last_row, q_out, k_out, d_out, dec_out in dirs:
            b = bs[s][idx]
            b3 = b.reshape(sub_chunks, GLA_CHUNK, D_QK)
            total = b3[:, last_row:last_row + 1, :]
            q_out[0, rows[s], :] = (q * jnp.exp(b)).astype(BF16)
            k_out[0, rows[s], :] = (k * jnp.exp(-b)).astype(BF16)
            d_out[0, rows[s], :] = (k * jnp.exp(total - b3).reshape(sub, D_QK)).astype(BF16)
            dec_out[0, chunks[s], :] = jnp.exp(total).reshape(sub_chunks, D_QK)


def _inproj(x, sc, sh, w_main, w_r, w_gk, b_gk):
    bn, sn, d = x.shape
    ts = INPROJ_TILE
    n_chunks = ts // GLA_CHUNK
    tok = lambda width: pl.BlockSpec((1, ts, width), lambda b, j: (b, j, 0))
    per_batch = pl.BlockSpec((1, 1, d), lambda b, j: (b, 0, 0))
    full = lambda a: pl.BlockSpec(a.shape, lambda b, j: (0,) * a.ndim)
    dec_spec = pl.BlockSpec((1, n_chunks, D_QK), lambda b, j: (b, j, 0))
    tok_shape = lambda width, dt: jax.ShapeDtypeStruct((bn, sn, width), dt)
    dec_shape = jax.ShapeDtypeStruct((bn, sn // GLA_CHUNK, D_QK), F32)
    return pl.pallas_call(
        _inproj_kernel,
        grid=(bn, sn // ts),
        in_specs=[tok(d), per_batch, per_batch, full(w_main), full(w_r), full(w_gk), full(b_gk)],
        out_specs=[tok(D_POOL), tok(D_GLA_V), tok(D_GLA_V)] + [tok(D_QK)] * 6 + [dec_spec, dec_spec],
        out_shape=[tok_shape(D_POOL, F32), tok_shape(D_GLA_V, F32), tok_shape(D_GLA_V, BF16)]
        + [tok_shape(D_QK, BF16)] * 6 + [dec_shape, dec_shape],
        compiler_params=pltpu.CompilerParams(
            dimension_semantics=("arbitrary", "arbitrary"), vmem_limit_bytes=VMEM_LIMIT),
        name="inproj",
    )(x, sc, sh, w_main, w_r, w_gk, b_gk)


def _gla_kernel(qf_ref, kf_ref, df_ref, qb_ref, kb_ref, db_ref, v_ref, decf_ref, decb_ref,
                g_ref, ng_ref, y_ref, of_acc, ob_acc, state_f, state_b):
    sn = v_ref.shape[1]
    n_chunks = sn // GLA_CHUNK
    lane_head = lax.broadcasted_iota(I32, (GLA_CHUNK, D_QK), 1) // GLA_DK
    head_masks = [lane_head == hd for hd in range(GLA_HEADS)]
    state_blocks = (lax.broadcasted_iota(I32, (D_GLA_V, D_QK), 0) // GLA_DV
                    == lax.broadcasted_iota(I32, (D_GLA_V, D_QK), 1) // GLA_DK)
    def group_masks(own):
        masks = []
        for m in range(GLA_GROUP):
            r = lax.broadcasted_iota(I32, (GLA_CHUNK, GLA_CHUNK * (m + 1)), 0)
            c = lax.broadcasted_iota(I32, (GLA_CHUNK, GLA_CHUNK * (m + 1)), 1) - GLA_CHUNK * m
            masks.append((c < 0) | own(r, c))
        return masks

    causal = group_masks(lambda r, c: c <= r)
    anti = group_masks(lambda r, c: c > r)

    def scaled(a, factor):
        return a if factor is None else (a.astype(F32) * factor).astype(BF16)

    def product(factors):
        out = None
        for f in factors:
            out = f if out is None else out * f
        return out

    def group_out(first_chunk, order, q_ref, k_ref, d_ref, dec_ref, state, masks, o_acc):
        base = pl.multiple_of(first_chunk * GLA_CHUNK, GLA_GROUP * GLA_CHUNK)
        slab = pl.ds(base, GLA_GROUP * GLA_CHUNK)
        part = lambda a, c: a[c * GLA_CHUNK:(c + 1) * GLA_CHUNK]
        q_all, k_all, d_all, v_all = q_ref[0, slab, :], k_ref[0, slab, :], d_ref[0, slab, :], v_ref[0, slab, :]
        qe = [part(q_all, c) for c in order]
        ke = [part(k_all, c) for c in order]
        kd = [part(d_all, c) for c in order]
        vv = [part(v_all, c) for c in order]
        dec = [dec_ref[0, pl.ds(first_chunk + c, 1), :] for c in order]

        st = state[...]
        st_heads = jnp.where(state_blocks, st, 0.0).astype(BF16)
        q_in = jnp.concatenate([scaled(qe[m], product(dec[:m])) for m in range(GLA_GROUP)], axis=0)
        inter = lax.dot_general(q_in, st_heads, NT_DIMS, preferred_element_type=F32)
        k_out = jnp.concatenate([scaled(kd[c], product(dec[c + 1:])) for c in range(GLA_GROUP)], axis=0)
        upd = lax.dot_general(jnp.concatenate(vv, axis=0), k_out, TN_DIMS, preferred_element_type=F32)
        state[...] = st * product(dec) + upd

        scores = []
        for m in range(GLA_GROUP):
            keys = jnp.concatenate([scaled(kd[c], product(dec[c + 1:m])) for c in range(m)] + [ke[m]], axis=0)
            q_heads = jnp.concatenate([jnp.where(hm, qe[m], jnp.zeros_like(qe[m])) for hm in head_masks], axis=0)
            scores.append(lax.dot_general(q_heads, keys, NT_DIMS, preferred_element_type=F32))

        def finish():
            for m in range(GLA_GROUP):
                vals = jnp.concatenate(vv[:m + 1], axis=0)
                outs = []
                for hd in range(GLA_HEADS):
                    att = jnp.where(masks[m], scores[m][hd * GLA_CHUNK:(hd + 1) * GLA_CHUNK], 0.0).astype(BF16)
                    outs.append(_dot(att, vals[:, hd * GLA_DV:(hd + 1) * GLA_DV]))
                rows = pl.ds(base + order[m] * GLA_CHUNK, GLA_CHUNK)
                o_acc[rows, :] = jnp.concatenate(outs, axis=1) + part(inter, m)

        return finish

    state_f[...] = jnp.zeros_like(state_f)
    state_b[...] = jnp.zeros_like(state_b)
    n_groups = n_chunks // GLA_GROUP
    ascending = list(range(GLA_GROUP))

    def scan_body(i, carry):
        finish_f = group_out(i * GLA_GROUP, ascending, qf_ref, kf_ref, df_ref, decf_ref, state_f, causal, of_acc)
        finish_b = group_out((n_groups - 1 - i) * GLA_GROUP, ascending[::-1], qb_ref, kb_ref, db_ref, decb_ref,
                             state_b, anti, ob_acc)
        finish_f()
        finish_b()
        return carry

    lax.fori_loop(0, n_groups, scan_body, 0)

    norm_g = ng_ref[...]

    def out_body(s, carry):
        rows = pl.ds(pl.multiple_of(s * GLA_OUT_ROWS, GLA_OUT_ROWS), GLA_OUT_ROWS)
        o = of_acc[rows, :] + ob_acc[rows, :]
        gate = g_ref[0, rows, :]
        gate = gate * jax.nn.sigmoid(gate)
        outs = []
        for hd in range(GLA_HEADS):
            oh = o[:, hd * GLA_DV:(hd + 1) * GLA_DV]
            oh = oh * lax.rsqrt(jnp.mean(oh * oh, axis=-1, keepdims=True) + EPS) * norm_g
            outs.append(oh)
        y_ref[0, rows, :] = (jnp.concatenate(outs, axis=1) * gate).astype(BF16)
        return carry

    lax.fori_loop(0, sn // GLA_OUT_ROWS, out_body, 0)


def _gla(qf, kf, df, qb, kb, db, v, decf, decb, g, norm_g):
    bn, sn, _ = v.shape
    seq = lambda a: pl.BlockSpec((1,) + a.shape[1:], lambda b: (b, 0, 0))
    args = (qf, kf, df, qb, kb, db, v, decf, decb, g)
    return pl.pallas_call(
        _gla_kernel,
        grid=(bn,),
        in_specs=[seq(a) for a in args] + [pl.BlockSpec((1, GLA_DV), lambda b: (0, 0))],
        out_specs=pl.BlockSpec((1, sn, D_GLA_V), lambda b: (b, 0, 0)),
        out_shape=jax.ShapeDtypeStruct((bn, sn, D_GLA_V), BF16),
        scratch_shapes=[pltpu.VMEM((sn, D_GLA_V), F32), pltpu.VMEM((sn, D_GLA_V), F32),
                        pltpu.VMEM((D_GLA_V, D_QK), F32), pltpu.VMEM((D_GLA_V, D_QK), F32)],
        compiler_params=pltpu.CompilerParams(
            dimension_semantics=("arbitrary",), vmem_limit_bytes=VMEM_LIMIT),
        name="gla",
    )(*args, norm_g.reshape(1, GLA_DV))


def _mix_kernel(u_ref, up_ref, un_ref, yg_ref, x_ref, wp_ref, ps_ref, wo_ref, g1_ref, sc_ref, sh_ref,
                wrh_ref, wrl_ref, br_ref,
                xo_ref, h_ref, pos_ref, rw_ref, cnt_ref, ext_ref, earlier_ref, *, seq_len):
    j = pl.program_id(1)
    ts = u_ref.shape[1]
    sub = MIX_ROWS
    groups = range(ts // sub)
    rows = [slice(s * sub, (s + 1) * sub) for s in groups]

    @pl.when((pl.program_id(0) == 0) & (j == 0))
    def _():
        tr = lax.broadcasted_iota(I32, (ts, ts), 0)
        tc = lax.broadcasted_iota(I32, (ts, ts), 1)
        earlier_ref[...] = jnp.where(tr < tc, 1.0, 0.0).astype(BF16)
        ext_ref[ts + 2 * POOL_HALO:, :] = jnp.zeros((POOL_HALO, D_POOL), F32)

    prev_ok = jnp.where(j > 0, 1.0, 0.0)
    next_ok = jnp.where(j < pl.num_programs(1) - 1, 1.0, 0.0)
    ext_ref[0:POOL_HALO, :] = up_ref[0, 0] * prev_ok
    ext_ref[POOL_HALO:POOL_HALO + ts, :] = u_ref[0]
    ext_ref[POOL_HALO + ts:POOL_HALO + ts + POOL_HALO, :] = un_ref[0, 0] * next_ok
    pos = j * ts + lax.broadcasted_iota(I32, (ts, 1), 0)

    def window_sum(lanes, w):
        if w == 2:
            return ext_ref[POOL_HALO - 1:POOL_HALO - 1 + ts, lanes] + ext_ref[POOL_HALO:POOL_HALO + ts, lanes]
        first = SEG_ALIGN
        n = ts + SEG_ALIGN * (w.bit_length() - 2)
        e = ext_ref[first:first + n + SEG_ALIGN, lanes]
        p = e[0:n] + e[1:n + 1]
        step = 2
        while step < w // 2:
            n -= SEG_ALIGN
            p = p[0:n] + p[step:step + n]
            step *= 2
        off = POOL_HALO - w // 2 - first
        return p[off:off + ts] + p[off + w // 2:off + w // 2 + ts]

    pooled = []
    for gi, w in enumerate(POOL_WINDOWS):
        lanes = slice(gi * POOL_GROUP, (gi + 1) * POOL_GROUP)
        acc = window_sum(lanes, w)
        tok = u_ref[0, :, lanes]

        def clipped(edge):
            lo = jnp.clip(pos[edge] - w // 2, 0, seq_len - 1)
            hi = jnp.clip(pos[edge] - w // 2 + w - 1, 0, seq_len - 1)
            return acc[edge] / (hi - lo + 1).astype(F32) - tok[edge]

        inner = slice(POOL_HALO, ts - POOL_HALO)
        pooled.append(jnp.concatenate(
            [clipped(slice(0, POOL_HALO)), acc[inner] * (1.0 / w) - tok[inner], clipped(slice(ts - POOL_HALO, ts))],
            axis=0).astype(BF16))

    ys = []
    for s in groups:
        y_pool = [(_dot(pooled[gi][rows[s]], wp_ref[gi])
                   * ps_ref[:, gi * POOL_GROUP:(gi + 1) * POOL_GROUP]).astype(BF16)
                  for gi in range(len(POOL_WINDOWS))]
        ys.append(jnp.concatenate(y_pool + [yg_ref[0, rows[s], :]], axis=1))
    mixes = [_dot(ys[s], wo_ref[...]) for s in groups]

    logits = []
    for s in groups:
        x = x_ref[0, rows[s], :] + g1_ref[0] * mixes[s]
        xo_ref[0, rows[s], :] = x
        ms = jnp.mean(x * x, axis=-1, keepdims=True)
        h = x * lax.rsqrt(ms + EPS) * (1.0 + sc_ref[0]) + sh_ref[0]
        h_ref[0, rows[s], :] = h.astype(BF16)
        h_hi, h_lo = _split_bf16(h)
        logits.append(lax.dot_general(wrh_ref[...], h_hi, NT_DIMS, preferred_element_type=F32)
                      + lax.dot_general(wrh_ref[...], h_lo, NT_DIMS, preferred_element_type=F32)
                      + lax.dot_general(wrl_ref[...], h_hi, NT_DIMS, preferred_element_type=F32))

    score = jax.nn.sigmoid(jnp.concatenate(logits, axis=1))
    sel = (score + br_ref[...]).reshape(N_GROUPS, EXPERTS_PER_GROUP, ts)
    local = lax.broadcasted_iota(I32, sel.shape, 1)
    big = EXPERTS_PER_GROUP
    m1 = jnp.max(sel, axis=1, keepdims=True)
    i1 = jnp.min(jnp.where(sel == m1, local, big), axis=1, keepdims=True)
    rest = jnp.where(local == i1, -jnp.inf, sel)
    m2 = jnp.max(rest, axis=1, keepdims=True)
    i2 = jnp.min(jnp.where(rest == m2, local, big), axis=1, keepdims=True)
    gscore = m1 + m2
    gid = lax.broadcasted_iota(I32, gscore.shape, 0)
    gbest = jnp.max(gscore, axis=0, keepdims=True)
    gsel = jnp.min(jnp.where(gscore == gbest, gid, N_GROUPS), axis=0, keepdims=True)
    in_group = gid == gsel
    pick0 = (in_group & (local == i1)).reshape(N_EXPERTS, ts)
    pick1 = (in_group & (local == i2)).reshape(N_EXPERTS, ts)
    s0 = jnp.sum(jnp.where(pick0, score, 0.0), axis=0, keepdims=True)
    s1 = jnp.sum(jnp.where(pick1, score, 0.0), axis=0, keepdims=True)
    rw_ref[0:1, :] = s0 / (s0 + s1)
    rw_ref[1:2, :] = s1 / (s0 + s1)

    onehot = jnp.where(pick0 | pick1, 1.0, 0.0).astype(BF16)
    er = lax.broadcasted_iota(I32, (N_EXPERTS, N_EXPERTS), 0)
    ec = lax.broadcasted_iota(I32, (N_EXPERTS, N_EXPERTS), 1)
    lower_exp = jnp.where(ec < er, 1.0, 0.0).astype(BF16)
    same_before = _dot(onehot, earlier_ref[...])
    cnt = jnp.sum(onehot.astype(F32), axis=1, keepdims=True)
    seg = jnp.ceil(cnt * (1.0 / SEG_ALIGN)) * SEG_ALIGN
    seg_lanes = jnp.broadcast_to(seg, (N_EXPERTS, LANE)).astype(BF16)
    first_slot = _dot(lower_exp, seg_lanes)[:, 0:1]
    slot = same_before + first_slot
    p0 = jnp.sum(jnp.where(pick0, slot, 0.0), axis=0, keepdims=True)
    p1 = jnp.sum(jnp.where(pick1, slot, 0.0), axis=0, keepdims=True)
    pos_ref[0:1, :] = p0.astype(I32)
    pos_ref[1:2, :] = p1.astype(I32)
    cnt_ref[0] = cnt


def _mix(u, y_gla, x, w_pool, pool_scale, w_out, g1, sc2, sh2, wr_hi, wr_lo, b_router):
    bn, sn, d = x.shape
    ts = SEQ_TILE
    nt = sn // ts
    hpt = ts // POOL_HALO
    n_halo = sn // POOL_HALO
    u_halo = u.reshape(bn, n_halo, POOL_HALO, D_POOL)
    tok = lambda width: pl.BlockSpec((1, ts, width), lambda b, j: (b, j, 0))
    per_batch = pl.BlockSpec((1, 1, d), lambda b, j: (b, 0, 0))
    full = lambda a: pl.BlockSpec(a.shape, lambda b, j: (0,) * a.ndim)
    prev_halo = pl.BlockSpec((1, 1, POOL_HALO, D_POOL),
                             lambda b, j: (b, jnp.maximum(j * hpt - 1, 0), 0, 0))
    next_halo = pl.BlockSpec((1, 1, POOL_HALO, D_POOL),
                             lambda b, j: (b, jnp.minimum((j + 1) * hpt, n_halo - 1), 0, 0))
    lane_tok = lambda rows: pl.BlockSpec((rows, ts), lambda b, j: (0, b * nt + j))
    t = bn * sn
    return pl.pallas_call(
        functools.partial(_mix_kernel, seq_len=sn),
        grid=(bn, nt),
        in_specs=[tok(D_POOL), prev_halo, next_halo, tok(D_GLA_V), tok(d), full(w_pool), full(pool_scale),
                  full(w_out), per_batch, per_batch, per_batch, full(wr_hi), full(wr_lo), full(b_router)],
        out_specs=[tok(d), tok(d), lane_tok(2), lane_tok(2),
                   pl.BlockSpec((1, N_EXPERTS, 1), lambda b, j: (b * nt + j, 0, 0))],
        out_shape=[jax.ShapeDtypeStruct((bn, sn, d), F32), jax.ShapeDtypeStruct((bn, sn, d), BF16),
                   jax.ShapeDtypeStruct((2, t), I32), jax.ShapeDtypeStruct((2, t), F32),
                   jax.ShapeDtypeStruct((bn * nt, N_EXPERTS, 1), F32)],
        scratch_shapes=[pltpu.VMEM((ts + 3 * POOL_HALO, D_POOL), F32), pltpu.VMEM((ts, ts), BF16)],
        compiler_params=pltpu.CompilerParams(
            dimension_semantics=("arbitrary", "arbitrary"), vmem_limit_bytes=VMEM_LIMIT),
        name="mix_route",
    )(u, u_halo, u_halo, y_gla, x, w_pool, pool_scale, w_out, g1, sc2, sh2, wr_hi, wr_lo, b_router)


def _segment_copies(meta_ref, tile, make_copy, action):
    base = tile * META_PER_TILE
    for e in range(N_EXPERTS):
        n = pl.multiple_of(meta_ref[base + e], SEG_ALIGN)
        first_slot = pl.multiple_of(meta_ref[base + N_EXPERTS + e], SEG_ALIGN)
        first_row = pl.multiple_of(meta_ref[base + 2 * N_EXPERTS + e], SEG_ALIGN)

        @pl.when(n > 0)
        def _():
            action(make_copy(first_slot, first_row, n))


def _tile_slots_used(meta_ref, tile):
    last = tile * META_PER_TILE + N_EXPERTS - 1
    return pl.multiple_of(meta_ref[last] + meta_ref[last + N_EXPERTS], SEG_ALIGN)


def _start(copy):
    copy.start()


def _wait(copy):
    copy.wait()


def _pack_rows(x):
    lo = lax.bitcast_convert_type(x[:, :D_PACK], U32)
    hi = lax.bitcast_convert_type(x[:, D_PACK:], U32)
    return hi | (lo >> 16)


def _unpack_rows(u):
    lo = lax.bitcast_convert_type(u << 16, F32)
    hi = lax.bitcast_convert_type(u & jnp.uint32(0xFFFF0000), F32)
    return lo, hi


def _dispatch_kernel(meta_ref, tail_ref, h_ref, pos_ref, xs_ref, buf, zbuf, sems, sem):
    i = pl.program_id(0)
    ts = h_ref.shape[0]

    def tile_copy(tile):
        half = tile % 2

        def copy(first_slot, first_row, n):
            return pltpu.make_async_copy(buf.at[half, pl.ds(first_slot, n)], xs_ref.at[pl.ds(first_row, n)],
                                         sems.at[half])
        return copy

    def wait_tile(tile):
        tile_copy(tile)(0, 0, _tile_slots_used(meta_ref, tile)).wait()

    slot = lax.broadcasted_iota(I32, (TILE_SLOTS, ts), 0)
    perm = jnp.where((slot == pos_ref[0:1, :]) | (slot == pos_ref[1:2, :]), 1.0, 0.0).astype(BF16)
    rows = _dot(perm, h_ref[...])

    @pl.when(i > 1)
    def _():
        wait_tile(i - 2)

    buf[i % 2] = _pack_rows(rows)
    _segment_copies(meta_ref, i, tile_copy(i), _start)

    @pl.when(i == pl.num_programs(0) - 1)
    def _():
        @pl.when(i > 0)
        def _():
            wait_tile(i - 1)

        wait_tile(i)
        zbuf[...] = jnp.zeros_like(zbuf)

        def tail_copies(action):
            for e in range(N_EXPERTS):
                n = pl.multiple_of(tail_ref[e], SEG_ALIGN)
                first_row = pl.multiple_of(tail_ref[N_EXPERTS + e], SEG_ALIGN)

                @pl.when(n > 0)
                def _():
                    action(pltpu.make_async_copy(zbuf.at[pl.ds(0, n)], xs_ref.at[pl.ds(first_row, n)], sem))

        tail_copies(_start)
        tail_copies(_wait)

        def zero_block(blk, carry):
            first_row = pl.multiple_of(blk * MOE_BLOCK, MOE_BLOCK)
            fill = pltpu.make_async_copy(zbuf, xs_ref.at[pl.ds(first_row, MOE_BLOCK)], sem)
            fill.start()
            fill.wait()
            return carry

        lax.fori_loop(tail_ref[2 * N_EXPERTS], xs_ref.shape[0] // MOE_BLOCK, zero_block, 0)


def _dispatch(meta, tails, h, pos, n_rows_sorted):
    t, d = h.shape
    ts = SEQ_TILE
    return pl.pallas_call(
        _dispatch_kernel,
        grid_spec=pltpu.PrefetchScalarGridSpec(
            num_scalar_prefetch=2,
            grid=(t // ts,),
            in_specs=[pl.BlockSpec((ts, d), lambda i, m, tl: (i, 0)),
                      pl.BlockSpec((2, ts), lambda i, m, tl: (0, i))],
            out_specs=pl.BlockSpec(memory_space=pl.ANY),
            scratch_shapes=[pltpu.VMEM((2, TILE_SLOTS, D_PACK), U32), pltpu.VMEM((MOE_BLOCK, D_PACK), U32),
                            pltpu.SemaphoreType.DMA((2,)), pltpu.SemaphoreType.DMA(())],
        ),
        out_shape=jax.ShapeDtypeStruct((n_rows_sorted, D_PACK), U32),
        compiler_params=pltpu.CompilerParams(
            dimension_semantics=("arbitrary",), vmem_limit_bytes=VMEM_LIMIT),
        name="dispatch",
    )(meta, tails, h, pos)


def _ffn_kernel(blk_e_ref, n_used_ref, xs_ref, w1_ref, w3_ref, w2_ref, ys_ref):
    del blk_e_ref

    @pl.when(pl.program_id(0) < n_used_ref[0])
    def _():
        lo, hi = _unpack_rows(xs_ref[...])
        xb = jnp.concatenate([lo.astype(BF16), hi.astype(BF16)], axis=1)
        a = _dot(xb, w1_ref[0, 0].astype(BF16))
        hid = (a * jax.nn.sigmoid(a)) * _dot(xb, w3_ref[0, 0].astype(BF16))
        y = _dot(hid.astype(BF16), w2_ref[0, 0].astype(BF16))
        ys_ref[...] = _pack_rows(y.astype(BF16).astype(F32))

    @pl.when(pl.program_id(0) >= n_used_ref[0])
    def _():
        ys_ref[...] = jnp.zeros_like(ys_ref)


def _ffn(layer, blk_e, n_used, xs, w1, w3, w2):
    rows = xs.shape[0]
    n_blocks = rows // MOE_BLOCK
    last = lambda i, n_used_ref: jnp.minimum(i, n_used_ref[0] - 1)
    row_spec = pl.BlockSpec((MOE_BLOCK, D_PACK), lambda i, be, nu: (last(i, nu), 0))
    w_spec = lambda a: pl.BlockSpec((1, 1) + a.shape[2:], lambda i, be, nu: (layer, be[last(i, nu)], 0, 0))
    return pl.pallas_call(
        _ffn_kernel,
        grid_spec=pltpu.PrefetchScalarGridSpec(
            num_scalar_prefetch=2,
            grid=(n_blocks,),
            in_specs=[row_spec, w_spec(w1), w_spec(w3), w_spec(w2)],
            out_specs=pl.BlockSpec((MOE_BLOCK, D_PACK), lambda i, be, nu: (i, 0)),
        ),
        out_shape=jax.ShapeDtypeStruct(xs.shape, U32),
        compiler_params=pltpu.CompilerParams(
            dimension_semantics=("arbitrary",), vmem_limit_bytes=VMEM_LIMIT),
        name="expert_ffn",
    )(blk_e, n_used, xs, w1, w3, w2)


def _combine_kernel(meta_ref, ys_ref, x_ref, pos_ref, rw_ref, g2_ref, fg_ref, o_ref, buf, zbuf, sem,
                    *, final_norm):
    i = pl.program_id(0)
    ts = x_ref.shape[0]

    def fetch(tile):
        half = tile % 2

        def copy(first_slot, first_row, n):
            return pltpu.make_async_copy(ys_ref.at[pl.ds(first_row, n)],
                                         buf.at[half, pl.ds(first_slot, n)], sem.at[half])

        _segment_copies(meta_ref, tile, copy, _start)
        end = _tile_slots_used(meta_ref, tile)
        n_free = pl.multiple_of(TILE_SLOTS - end, SEG_ALIGN)

        @pl.when(n_free > 0)
        def _():
            pltpu.make_async_copy(zbuf.at[pl.ds(0, n_free)], buf.at[half, pl.ds(end, n_free)],
                                  sem.at[half]).start()

    @pl.when(i == 0)
    def _():
        zbuf[...] = jnp.zeros_like(zbuf)
        fetch(i)

    @pl.when(i + 1 < pl.num_programs(0))
    def _():
        fetch(i + 1)

    pltpu.make_async_copy(ys_ref.at[pl.ds(0, TILE_SLOTS)], buf.at[i % 2], sem.at[i % 2]).wait()
    lo, hi = _unpack_rows(buf[i % 2])
    rows = jnp.concatenate([lo.astype(BF16), hi.astype(BF16)], axis=1)
    slot = lax.broadcasted_iota(I32, (TILE_SLOTS, ts), 0)
    weights = (jnp.where(slot == pos_ref[0:1, :], rw_ref[0:1, :], 0.0)
               + jnp.where(slot == pos_ref[1:2, :], rw_ref[1:2, :], 0.0)).astype(BF16)
    x = x_ref[...] + g2_ref[0] * lax.dot_general(weights, rows, TN_DIMS, preferred_element_type=F32)
    if final_norm:
        ms = jnp.mean(x * x, axis=-1, keepdims=True)
        x = x * lax.rsqrt(ms + EPS) * fg_ref[...]
    o_ref[...] = x


def _combine(meta, ys, x, pos, route_w, g2, final_g, seq_len, final_norm):
    t, d = x.shape
    ts = SEQ_TILE
    tiles_per_seq = seq_len // ts
    lane_tok = pl.BlockSpec((2, ts), lambda i, m: (0, i))
    return pl.pallas_call(
        functools.partial(_combine_kernel, final_norm=final_norm),
        grid_spec=pltpu.PrefetchScalarGridSpec(
            num_scalar_prefetch=1,
            grid=(t // ts,),
            in_specs=[pl.BlockSpec(memory_space=pl.ANY),
                      pl.BlockSpec((ts, d), lambda i, m: (i, 0)),
                      lane_tok, lane_tok,
                      pl.BlockSpec((1, 1, d), lambda i, m: (i // tiles_per_seq, 0, 0)),
                      pl.BlockSpec((1, d), lambda i, m: (0, 0))],
            out_specs=pl.BlockSpec((ts, d), lambda i, m: (i, 0)),
            scratch_shapes=[pltpu.VMEM((2, TILE_SLOTS, D_PACK), U32),
                            pltpu.VMEM((TILE_SLOTS - 2 * ts, D_PACK), U32), pltpu.SemaphoreType.DMA((2,))],
        ),
        out_shape=jax.ShapeDtypeStruct((t, d), F32),
        compiler_params=pltpu.CompilerParams(
            dimension_semantics=("arbitrary",), vmem_limit_bytes=VMEM_LIMIT),
        name="combine",
    )(meta, ys, x, pos, route_w, g2, final_g.reshape(1, d))


def _dispatch_plan(tile_counts, n_assign):
    n_tiles = tile_counts.shape[0]
    seg = (tile_counts.astype(I32) + SEG_ALIGN - 1) // SEG_ALIGN * SEG_ALIGN
    totals = jnp.sum(seg, axis=0)
    padded = (totals + MOE_BLOCK - 1) // MOE_BLOCK * MOE_BLOCK
    pad_end = jnp.cumsum(padded)
    pad_start = pad_end - padded
    first_row = pad_start[None, :] + jnp.cumsum(seg, axis=0) - seg
    first_slot = jnp.cumsum(seg, axis=1) - seg
    meta = jnp.stack([seg, first_slot, first_row], axis=1).reshape(-1)
    max_rows = n_assign + n_tiles * N_EXPERTS * (SEG_ALIGN - 1)
    n_blocks = -(-max_rows // MOE_BLOCK) + N_EXPERTS
    blk_start = jnp.arange(n_blocks, dtype=I32) * MOE_BLOCK
    blk_e = jnp.minimum(jnp.sum((pad_end[None, :] <= blk_start[:, None]).astype(I32), axis=1), N_EXPERTS - 1)
    n_used = (pad_end[-1] // MOE_BLOCK).astype(I32).reshape(1)
    tails = jnp.concatenate([padded - totals, pad_start + totals, n_used])
    return meta.astype(I32), tails.astype(I32), blk_e.astype(I32), n_used, n_blocks * MOE_BLOCK


def kernel(x, c, w_mod, b_mod, w_in, w_pool, pool_scale, w_gk_up, b_gk, gla_norm_g, w_out,
           w_router, b_router, w1, w3, w2, final_g):
    bn, sn, d = x.shape
    t = bn * sn
    mod = _modulation(c, w_mod, b_mod).reshape(DEPTH, bn, 6, 1, d)

    w_main = w_in[:, :, :D_MAIN].astype(BF16)
    w_r = jnp.swapaxes(w_in[:, :, D_MAIN:], 1, 2).astype(BF16)
    zero_rank = jnp.zeros_like(w_gk_up[:, 0])
    w_gk = jnp.stack([jnp.concatenate([w_gk_up[:, 0], zero_rank], axis=1),
                      jnp.concatenate([zero_rank, w_gk_up[:, 1]], axis=1)], axis=1).astype(BF16)
    b_gk3 = b_gk.reshape(DEPTH, 2, 1, D_QK)
    w_pool_b = w_pool.astype(BF16)
    w_out_b = w_out.astype(BF16)
    wr_t = w_router.T
    wr_hi = wr_t.astype(BF16)
    wr_lo = (wr_t - wr_hi.astype(F32)).astype(BF16)
    br = b_router.reshape(N_EXPERTS, 1)

    for l in range(DEPTH):
        sh1, sc1, g1, sh2, sc2, g2 = (mod[l, :, i] for i in range(6))
        u, g, v, qf, kf, df, qb, kb, db, decf, decb = _inproj(
            x, sc1, sh1, w_main[l], w_r[l], w_gk[l], b_gk3[l])
        y_gla = _gla(qf, kf, df, qb, kb, db, v, decf, decb, g, gla_norm_g[l])
        x, h, pos, route_w, tile_counts = _mix(
            u, y_gla, x, w_pool_b[l], pool_scale[l].reshape(1, D_POOL), w_out_b[l], g1, sc2, sh2,
            wr_hi, wr_lo, br)
        meta, tails, blk_e, n_used, n_rows_sorted = _dispatch_plan(tile_counts[:, :, 0], 2 * t)
        xs = _dispatch(meta, tails, h.reshape(t, d), pos, n_rows_sorted)
        ys = _ffn(l, blk_e, n_used, xs, w1, w3, w2)
        x = _combine(meta, ys, x.reshape(t, d), pos, route_w, g2, final_g, sn, l == DEPTH - 1)
        x = x.reshape(bn, sn, d)
    return x
```

```python
import functools

import jax
import jax.numpy as jnp
from jax import lax
from jax.experimental import pallas as pl
from jax.experimental.pallas import tpu as pltpu

F32 = jnp.float32
BF16 = jnp.bfloat16
I32 = jnp.int32
U32 = jnp.uint32

D_MODEL = 1024
DEPTH = 2
D_POOL = 512
POOL_WINDOWS = (2, 4, 8, 16)
POOL_GROUP = 128
POOL_HALO = 16
D_GLA_V = 512
GLA_HEADS = 4
GLA_DK = 64
GLA_DV = 128
D_QK = GLA_HEADS * GLA_DK
GLA_RANK = 16
GATE_NORM = 16.0
GLA_CHUNK = 64
GLA_GROUP = 4
GLA_OUT_ROWS = 256
D_MAIN = D_POOL + 2 * D_QK + 2 * D_GLA_V
N_EXPERTS = 32
N_GROUPS = 4
EXPERTS_PER_GROUP = 8
D_FF = 512
EPS = 1e-6

LANE = 128
SEG_ALIGN = 8
MXU_DIM = 256
VMEM_LIMIT = 56 * 1024 * 1024

SEQ_TILE = 512
MIX_ROWS = 256
INPROJ_TILE = 1024
CUMSUM_GROUP = 256
MOD_COLS = 1536
MOE_BLOCK = 1024
TILE_SLOTS = -(-(2 * SEQ_TILE + N_EXPERTS * (SEG_ALIGN - 1)) // MXU_DIM) * MXU_DIM
D_PACK = D_MODEL // 2
META_PER_TILE = 3 * N_EXPERTS
assert 2 * SEQ_TILE // SEG_ALIGN <= 256

NT_DIMS = (((1,), (1,)), ((), ()))
TN_DIMS = (((0,), (0,)), ((), ()))


def _dot(a, b):
    return jnp.dot(a, b, preferred_element_type=F32)


def _split_bf16(a):
    hi = a.astype(BF16)
    lo = (a - hi.astype(F32)).astype(BF16)
    return hi, lo


def _mod_kernel(c_ref, w_ref, b_ref, o_ref):
    c = c_ref[...]
    c_act = (c * jax.nn.sigmoid(c)).astype(BF16)
    o_ref[0] = _dot(c_act, w_ref[0].astype(BF16)) + b_ref[0]


def _modulation(c, w_mod, b_mod):
    n_layers, d, n_out = w_mod.shape
    bn = c.shape[0]
    tn = MOD_COLS
    return pl.pallas_call(
        _mod_kernel,
        grid=(n_layers, n_out // tn),
        in_specs=[
            pl.BlockSpec((bn, d), lambda l, j: (0, 0)),
            pl.BlockSpec((1, d, tn), lambda l, j: (l, 0, j)),
            pl.BlockSpec((1, 1, tn), lambda l, j: (l, 0, j)),
        ],
        out_specs=pl.BlockSpec((1, bn, tn), lambda l, j: (l, 0, j)),
        out_shape=jax.ShapeDtypeStruct((n_layers, bn, n_out), F32),
        compiler_params=pltpu.CompilerParams(
            dimension_semantics=("arbitrary", "arbitrary"), vmem_limit_bytes=VMEM_LIMIT),
        name="modulation",
    )(c, w_mod, b_mod.reshape(n_layers, 1, n_out))


def _log_sigmoid(x):
    return jnp.minimum(x, 0.0) - jnp.log1p(jnp.exp(-jnp.abs(x)))


def _inproj_kernel(x_ref, sc_ref, sh_ref, wmain_ref, wr_ref, wgk_ref, bgk_ref,
                   u_ref, g_ref, v_ref, qf_ref, kf_ref, df_ref, qb_ref, kb_ref, db_ref,
                   decf_ref, decb_ref):
    tile = x_ref.shape[1]
    sub = CUMSUM_GROUP
    sub_chunks = sub // GLA_CHUNK
    row = lax.broadcasted_iota(I32, (sub, sub), 0)
    col = lax.broadcasted_iota(I32, (sub, sub), 1)
    same_chunk = (row // GLA_CHUNK) == (col // GLA_CHUNK)
    prefix = jnp.where(same_chunk & (col <= row), 1.0, 0.0).astype(BF16)
    suffix = jnp.where(same_chunk & (col >= row), 1.0, 0.0).astype(BF16)

    groups = range(tile // sub)
    rows = [slice(s * sub, (s + 1) * sub) for s in groups]
    chunks = [slice(s * sub_chunks, (s + 1) * sub_chunks) for s in groups]
    dirs = ((0, prefix, GLA_CHUNK - 1, qf_ref, kf_ref, df_ref, decf_ref),
            (1, suffix, 0, qb_ref, kb_ref, db_ref, decb_ref))

    zs, rs = [], []
    for s in groups:
        x = x_ref[0, rows[s], :]
        ms = jnp.mean(x * x, axis=-1, keepdims=True)
        h = x * lax.rsqrt(ms + EPS) * (1.0 + sc_ref[0]) + sh_ref[0]
        hb = h.astype(BF16)
        zs.append(_dot(hb, wmain_ref[...]))
        rs.append(lax.dot_general(wr_ref[...], hb, NT_DIMS, preferred_element_type=F32).astype(BF16))

    logas = [[_log_sigmoid(lax.dot_general(rs[s], wgk_ref[idx], TN_DIMS, preferred_element_type=F32)
                           + bgk_ref[idx]) / GATE_NORM for idx in range(2)]
             for s in groups]

    bs = []
    for s in groups:
        per_dir = []
        for idx, tri, *_ in dirs:
            hi, lo = _split_bf16(logas[s][idx])
            per_dir.append(_dot(tri, hi) + _dot(tri, lo))
        bs.append(per_dir)

    for s in groups:
        z = zs[s]
        u_ref[0, rows[s], :] = z[:, 0:D_POOL]
        q = z[:, D_POOL:D_POOL + D_QK] * (GLA_DK ** -0.5)
        k = z[:, D_POOL + D_QK:D_POOL + 2 * D_QK]
        v_ref[0, rows[s], :] = z[:, D_POOL + 2 * D_QK:D_POOL + 2 * D_QK + D_GLA_V].astype(BF16)
        g_ref[0, rows[s], :] = z[:, D_POOL + 2 * D_QK + D_GLA_V:D_MAIN]
        for idx, _, last_row, q_out, k_out, d_out, dec_out in dirs:
            b = bs[s][idx]
            b3 = b.reshape(sub_chunks, GLA_CHUNK, D_QK)
            total = b3[:, last_row:last_row + 1, :]
            q_out[0, rows[s], :] = (q * jnp.exp(b)).astype(BF16)
            k_out[0, rows[s], :] = (k * jnp.exp(-b)).astype(BF16)
            d_out[0, rows[s], :] = (k * jnp.exp(total - b3).reshape(sub, D_QK)).astype(BF16)
            dec_out[0, chunks[s], :] = jnp.exp(total).reshape(sub_chunks, D_QK)


def _inproj(x, sc, sh, w_main, w_r, w_gk, b_gk):
    bn, sn, d = x.shape
    ts = INPROJ_TILE
    n_chunks = ts // GLA_CHUNK
    tok = lambda width: pl.BlockSpec((1, ts, width), lambda b, j: (b, j, 0))
    per_batch = pl.BlockSpec((1, 1, d), lambda b, j: (b, 0, 0))
    full = lambda a: pl.BlockSpec(a.shape, lambda b, j: (0,) * a.ndim)
    dec_spec = pl.BlockSpec((1, n_chunks, D_QK), lambda b, j: (b, j, 0))
    tok_shape = lambda width, dt: jax.ShapeDtypeStruct((bn, sn, width), dt)
    dec_shape = jax.ShapeDtypeStruct((bn, sn // GLA_CHUNK, D_QK), F32)
    return pl.pallas_call(
        _inproj_kernel,
        grid=(bn, sn // ts),
        in_specs=[tok(d), per_batch, per_batch, full(w_main), full(w_r), full(w_gk), full(b_gk)],
        out_specs=[tok(D_POOL), tok(D_GLA_V), tok(D_GLA_V)] + [tok(D_QK)] * 6 + [dec_spec, dec_spec],
        out_shape=[tok_shape(D_POOL, F32), tok_shape(D_GLA_V, F32), tok_shape(D_GLA_V, BF16)]
        + [tok_shape(D_QK, BF16)] * 6 + [dec_shape, dec_shape],
        compiler_params=pltpu.CompilerParams(
            dimension_semantics=("arbitrary", "arbitrary"), vmem_limit_bytes=VMEM_LIMIT),
        name="inproj",
    )(x, sc, sh, w_main, w_r, w_gk, b_gk)


def _gla_kernel(qf_ref, kf_ref, df_ref, qb_ref, kb_ref, db_ref, v_ref, decf_ref, decb_ref,
                g_ref, ng_ref, y_ref, of_acc, ob_acc, state_f, state_b):
    sn = v_ref.shape[1]
    n_chunks = sn // GLA_CHUNK
    lane_head = lax.broadcasted_iota(I32, (GLA_CHUNK, D_QK), 1) // GLA_DK
    head_masks = [lane_head == hd for hd in range(GLA_HEADS)]
    state_blocks = (lax.broadcasted_iota(I32, (D_GLA_V, D_QK), 0) // GLA_DV
                    == lax.broadcasted_iota(I32, (D_GLA_V, D_QK), 1) // GLA_DK)
    def group_masks(own):
        masks = []
        for m in range(GLA_GROUP):
            r = lax.broadcasted_iota(I32, (GLA_CHUNK, GLA_CHUNK * (m + 1)), 0)
            c = lax.broadcasted_iota(I32, (GLA_CHUNK, GLA_CHUNK * (m + 1)), 1) - GLA_CHUNK * m
            masks.append((c < 0) | own(r, c))
        return masks

    causal = group_masks(lambda r, c: c <= r)
    anti = group_masks(lambda r, c: c > r)

    def scaled(a, factor):
        return a if factor is None else (a.astype(F32) * factor).astype(BF16)

    def product(factors):
        out = None
        for f in factors:
            out = f if out is None else out * f
        return out

    def group_out(first_chunk, order, q_ref, k_ref, d_ref, dec_ref, state, masks, o_acc):
        base = pl.multiple_of(first_chunk * GLA_CHUNK, GLA_GROUP * GLA_CHUNK)
        slab = pl.ds(base, GLA_GROUP * GLA_CHUNK)
        part = lambda a, c: a[c * GLA_CHUNK:(c + 1) * GLA_CHUNK]
        q_all, k_all, d_all, v_all = q_ref[0, slab, :], k_ref[0, slab, :], d_ref[0, slab, :], v_ref[0, slab, :]
        qe = [part(q_all, c) for c in order]
        ke = [part(k_all, c) for c in order]
        kd = [part(d_all, c) for c in order]
        vv = [part(v_all, c) for c in order]
        dec = [dec_ref[0, pl.ds(first_chunk + c, 1), :] for c in order]

        st = state[...]
        st_heads = jnp.where(state_blocks, st, 0.0).astype(BF16)
        q_in = jnp.concatenate([scaled(qe[m], product(dec[:m])) for m in range(GLA_GROUP)], axis=0)
        inter = lax.dot_general(q_in, st_heads, NT_DIMS, preferred_element_type=F32)
        k_out = jnp.concatenate([scaled(kd[c], product(dec[c + 1:])) for c in range(GLA_GROUP)], axis=0)
        upd = lax.dot_general(jnp.concatenate(vv, axis=0), k_out, TN_DIMS, preferred_element_type=F32)
        state[...] = st * product(dec) + upd

        scores = []
        for m in range(GLA_GROUP):
            keys = jnp.concatenate([scaled(kd[c], product(dec[c + 1:m])) for c in range(m)] + [ke[m]], axis=0)
            q_heads = jnp.concatenate([jnp.where(hm, qe[m], jnp.zeros_like(qe[m])) for hm in head_masks], axis=0)
            scores.append(lax.dot_general(q_heads, keys, NT_DIMS, preferred_element_type=F32))

        def finish():
            for m in range(GLA_GROUP):
                vals = jnp.concatenate(vv[:m + 1], axis=0)
                outs = []
                for hd in range(GLA_HEADS):
                    att = jnp.where(masks[m], scores[m][hd * GLA_CHUNK:(hd + 1) * GLA_CHUNK], 0.0).astype(BF16)
                    outs.append(_dot(att, vals[:, hd * GLA_DV:(hd + 1) * GLA_DV]))
                rows = pl.ds(base + order[m] * GLA_CHUNK, GLA_CHUNK)
                o_acc[rows, :] = jnp.concatenate(outs, axis=1) + part(inter, m)

        return finish

    state_f[...] = jnp.zeros_like(state_f)
    state_b[...] = jnp.zeros_like(state_b)
    n_groups = n_chunks // GLA_GROUP
    ascending = list(range(GLA_GROUP))

    def scan_body(i, carry):
        finish_f = group_out(i * GLA_GROUP, ascending, qf_ref, kf_ref, df_ref, decf_ref, state_f, causal, of_acc)
        finish_b = group_out((n_groups - 1 - i) * GLA_GROUP, ascending[::-1], qb_ref, kb_ref, db_ref, decb_ref,
                             state_b, anti, ob_acc)
        finish_f()
        finish_b()
        return carry

    lax.fori_loop(0, n_groups, scan_body, 0)

    norm_g = ng_ref[...]

    def out_body(s, carry):
        rows = pl.ds(pl.multiple_of(s * GLA_OUT_ROWS, GLA_OUT_ROWS), GLA_OUT_ROWS)
        o = of_acc[rows, :] + ob_acc[rows, :]
        gate = g_ref[0, rows, :]
        gate = gate * jax.nn.sigmoid(gate)
        outs = []
        for hd in range(GLA_HEADS):
            oh = o[:, hd * GLA_DV:(hd + 1) * GLA_DV]
            oh = oh * lax.rsqrt(jnp.mean(oh * oh, axis=-1, keepdims=True) + EPS) * norm_g
            outs.append(oh)
        y_ref[0, rows, :] = (jnp.concatenate(outs, axis=1) * gate).astype(BF16)
        return carry

    lax.fori_loop(0, sn // GLA_OUT_ROWS, out_body, 0)


def _gla(qf, kf, df, qb, kb, db, v, decf, decb, g, norm_g):
    bn, sn, _ = v.shape
    seq = lambda a: pl.BlockSpec((1,) + a.shape[1:], lambda b: (b, 0, 0))
    args = (qf, kf, df, qb, kb, db, v, decf, decb, g)
    return pl.pallas_call(
        _gla_kernel,
        grid=(bn,),
        in_specs=[seq(a) for a in args] + [pl.BlockSpec((1, GLA_DV), lambda b: (0, 0))],
        out_specs=pl.BlockSpec((1, sn, D_GLA_V), lambda b: (b, 0, 0)),
        out_shape=jax.ShapeDtypeStruct((bn, sn, D_GLA_V), BF16),
        scratch_shapes=[pltpu.VMEM((sn, D_GLA_V), F32), pltpu.VMEM((sn, D_GLA_V), F32),
                        pltpu.VMEM((D_GLA_V, D_QK), F32), pltpu.VMEM((D_GLA_V, D_QK), F32)],
        compiler_params=pltpu.CompilerParams(
            dimension_semantics=("arbitrary",), vmem_limit_bytes=VMEM_LIMIT),
        name="gla",
    )(*args, norm_g.reshape(1, GLA_DV))


def _mix_kernel(u_ref, up_ref, un_ref, yg_ref, x_ref, wp_ref, ps_ref, wo_ref, g1_ref, sc_ref, sh_ref,
                wrh_ref, wrl_ref, br_ref,
                xo_ref, h_ref, pos_ref, rw_ref, cnt_ref, ext_ref, earlier_ref, *, seq_len):
    j = pl.program_id(1)
    ts = u_ref.shape[1]
    sub = MIX_ROWS
    groups = range(ts // sub)
    rows = [slice(s * sub, (s + 1) * sub) for s in groups]

    @pl.when((pl.program_id(0) == 0) & (j == 0))
    def _():
        tr = lax.broadcasted_iota(I32, (ts, ts), 0)
        tc = lax.broadcasted_iota(I32, (ts, ts), 1)
        earlier_ref[...] = jnp.where(tr < tc, 1.0, 0.0).astype(BF16)
        ext_ref[ts + 2 * POOL_HALO:, :] = jnp.zeros((POOL_HALO, D_POOL), F32)

    prev_ok = jnp.where(j > 0, 1.0, 0.0)
    next_ok = jnp.where(j < pl.num_programs(1) - 1, 1.0, 0.0)
    ext_ref[0:POOL_HALO, :] = up_ref[0, 0] * prev_ok
    ext_ref[POOL_HALO:POOL_HALO + ts, :] = u_ref[0]
    ext_ref[POOL_HALO + ts:POOL_HALO + ts + POOL_HALO, :] = un_ref[0, 0] * next_ok
    pos = j * ts + lax.broadcasted_iota(I32, (ts, 1), 0)

    def window_sum(lanes, w):
        if w == 2:
            return ext_ref[POOL_HALO - 1:POOL_HALO - 1 + ts, lanes] + ext_ref[POOL_HALO:POOL_HALO + ts, lanes]
        first = SEG_ALIGN
        n = ts + SEG_ALIGN * (w.bit_length() - 2)
        e = ext_ref[first:first + n + SEG_ALIGN, lanes]
        p = e[0:n] + e[1:n + 1]
        step = 2
        while step < w // 2:
            n -= SEG_ALIGN
            p = p[0:n] + p[step:step + n]
            step *= 2
        off = POOL_HALO - w // 2 - first
        return p[off:off + ts] + p[off + w // 2:off + w // 2 + ts]

    pooled = []
    for gi, w in enumerate(POOL_WINDOWS):
        lanes = slice(gi * POOL_GROUP, (gi + 1) * POOL_GROUP)
        acc = window_sum(lanes, w)
        tok = u_ref[0, :, lanes]

        def clipped(edge):
            lo = jnp.clip(pos[edge] - w // 2, 0, seq_len - 1)
            hi = jnp.clip(pos[edge] - w // 2 + w - 1, 0, seq_len - 1)
            return acc[edge] / (hi - lo + 1).astype(F32) - tok[edge]

        inner = slice(POOL_HALO, ts - POOL_HALO)
        pooled.append(jnp.concatenate(
            [clipped(slice(0, POOL_HALO)), acc[inner] * (1.0 / w) - tok[inner], clipped(slice(ts - POOL_HALO, ts))],
            axis=0).astype(BF16))

    ys = []
    for s in groups:
        y_pool = [(_dot(pooled[gi][rows[s]], wp_ref[gi])
                   * ps_ref[:, gi * POOL_GROUP:(gi + 1) * POOL_GROUP]).astype(BF16)
                  for gi in range(len(POOL_WINDOWS))]
        ys.append(jnp.concatenate(y_pool + [yg_ref[0, rows[s], :]], axis=1))
    mixes = [_dot(ys[s], wo_ref[...]) for s in groups]

    logits = []
    for s in groups:
        x = x_ref[0, rows[s], :] + g1_ref[0] * mixes[s]
        xo_ref[0, rows[s], :] = x
        ms = jnp.mean(x * x, axis=-1, keepdims=True)
        h = x * lax.rsqrt(ms + EPS) * (1.0 + sc_ref[0]) + sh_ref[0]
        h_ref[0, rows[s], :] = h.astype(BF16)
        h_hi, h_lo = _split_bf16(h)
        logits.append(lax.dot_general(wrh_ref[...], h_hi, NT_DIMS, preferred_element_type=F32)
                      + lax.dot_general(wrh_ref[...], h_lo, NT_DIMS, preferred_element_type=F32)
                      + lax.dot_general(wrl_ref[...], h_hi, NT_DIMS, preferred_element_type=F32))

    score = jax.nn.sigmoid(jnp.concatenate(logits, axis=1))
    sel = (score + br_ref[...]).reshape(N_GROUPS, EXPERTS_PER_GROUP, ts)
    local = lax.broadcasted_iota(I32, sel.shape, 1)
    big = EXPERTS_PER_GROUP
    m1 = jnp.max(sel, axis=1, keepdims=True)
    i1 = jnp.min(jnp.where(sel == m1, local, big), axis=1, keepdims=True)
    rest = jnp.where(local == i1, -jnp.inf, sel)
    m2 = jnp.max(rest, axis=1, keepdims=True)
    i2 = jnp.min(jnp.where(rest == m2, local, big), axis=1, keepdims=True)
    gscore = m1 + m2
    gid = lax.broadcasted_iota(I32, gscore.shape, 0)
    gbest = jnp.max(gscore, axis=0, keepdims=True)
    gsel = jnp.min(jnp.where(gscore == gbest, gid, N_GROUPS), axis=0, keepdims=True)
    in_group = gid == gsel
    pick0 = (in_group & (local == i1)).reshape(N_EXPERTS, ts)
    pick1 = (in_group & (local == i2)).reshape(N_EXPERTS, ts)
    s0 = jnp.sum(jnp.where(pick0, score, 0.0), axis=0, keepdims=True)
    s1 = jnp.sum(jnp.where(pick1, score, 0.0), axis=0, keepdims=True)
    rw_ref[0:1, :] = s0 / (s0 + s1)
    rw_ref[1:2, :] = s1 / (s0 + s1)

    onehot = jnp.where(pick0 | pick1, 1.0, 0.0).astype(BF16)
    er = lax.broadcasted_iota(I32, (N_EXPERTS, N_EXPERTS), 0)
    ec = lax.broadcasted_iota(I32, (N_EXPERTS, N_EXPERTS), 1)
    lower_exp = jnp.where(ec < er, 1.0, 0.0).astype(BF16)
    same_before = _dot(onehot, earlier_ref[...])
    cnt = jnp.sum(onehot.astype(F32), axis=1, keepdims=True)
    seg = jnp.ceil(cnt * (1.0 / SEG_ALIGN)) * SEG_ALIGN
    seg_lanes = jnp.broadcast_to(seg, (N_EXPERTS, LANE)).astype(BF16)
    first_slot = _dot(lower_exp, seg_lanes)[:, 0:1]
    slot = same_before + first_slot
    p0 = jnp.sum(jnp.where(pick0, slot, 0.0), axis=0, keepdims=True)
    p1 = jnp.sum(jnp.where(pick1, slot, 0.0), axis=0, keepdims=True)
    pos_ref[0:1, :] = p0.astype(I32)
    pos_ref[1:2, :] = p1.astype(I32)
    cnt_ref[0] = cnt


def _mix(u, y_gla, x, w_pool, pool_scale, w_out, g1, sc2, sh2, wr_hi, wr_lo, b_router):
    bn, sn, d = x.shape
    ts = SEQ_TILE
    nt = sn // ts
    hpt = ts // POOL_HALO
    n_halo = sn // POOL_HALO
    u_halo = u.reshape(bn, n_halo, POOL_HALO, D_POOL)
    tok = lambda width: pl.BlockSpec((1, ts, width), lambda b, j: (b, j, 0))
    per_batch = pl.BlockSpec((1, 1, d), lambda b, j: (b, 0, 0))
    full = lambda a: pl.BlockSpec(a.shape, lambda b, j: (0,) * a.ndim)
    prev_halo = pl.BlockSpec((1, 1, POOL_HALO, D_POOL),
                             lambda b, j: (b, jnp.maximum(j * hpt - 1, 0), 0, 0))
    next_halo = pl.BlockSpec((1, 1, POOL_HALO, D_POOL),
                             lambda b, j: (b, jnp.minimum((j + 1) * hpt, n_halo - 1), 0, 0))
    lane_tok = lambda rows: pl.BlockSpec((rows, ts), lambda b, j: (0, b * nt + j))
    t = bn * sn
    return pl.pallas_call(
        functools.partial(_mix_kernel, seq_len=sn),
        grid=(bn, nt),
        in_specs=[tok(D_POOL), prev_halo, next_halo, tok(D_GLA_V), tok(d), full(w_pool), full(pool_scale),
                  full(w_out), per_batch, per_batch, per_batch, full(wr_hi), full(wr_lo), full(b_router)],
        out_specs=[tok(d), tok(d), lane_tok(2), lane_tok(2),
                   pl.BlockSpec((1, N_EXPERTS, 1), lambda b, j: (b * nt + j, 0, 0))],
        out_shape=[jax.ShapeDtypeStruct((bn, sn, d), F32), jax.ShapeDtypeStruct((bn, sn, d), BF16),
                   jax.ShapeDtypeStruct((2, t), I32), jax.ShapeDtypeStruct((2, t), F32),
                   jax.ShapeDtypeStruct((bn * nt, N_EXPERTS, 1), F32)],
        scratch_shapes=[pltpu.VMEM((ts + 3 * POOL_HALO, D_POOL), F32), pltpu.VMEM((ts, ts), BF16)],
        compiler_params=pltpu.CompilerParams(
            dimension_semantics=("arbitrary", "arbitrary"), vmem_limit_bytes=VMEM_LIMIT),
        name="mix_route",
    )(u, u_halo, u_halo, y_gla, x, w_pool, pool_scale, w_out, g1, sc2, sh2, wr_hi, wr_lo, b_router)


def _segment_copies(meta_ref, tile, make_copy, action):
    base = tile * META_PER_TILE
    for e in range(N_EXPERTS):
        n = pl.multiple_of(meta_ref[base + e], SEG_ALIGN)
        first_slot = pl.multiple_of(meta_ref[base + N_EXPERTS + e], SEG_ALIGN)
        first_row = pl.multiple_of(meta_ref[base + 2 * N_EXPERTS + e], SEG_ALIGN)

        @pl.when(n > 0)
        def _():
            action(make_copy(first_slot, first_row, n))


def _tile_slots_used(meta_ref, tile):
    last = tile * META_PER_TILE + N_EXPERTS - 1
    return pl.multiple_of(meta_ref[last] + meta_ref[last + N_EXPERTS], SEG_ALIGN)


def _start(copy):
    copy.start()


def _wait(copy):
    copy.wait()


def _pack_rows(x):
    lo = lax.bitcast_convert_type(x[:, :D_PACK], U32)
    hi = lax.bitcast_convert_type(x[:, D_PACK:], U32)
    return hi | (lo >> 16)


def _unpack_rows(u):
    lo = lax.bitcast_convert_type(u << 16, F32)
    hi = lax.bitcast_convert_type(u & jnp.uint32(0xFFFF0000), F32)
    return lo, hi


def _dispatch_kernel(meta_ref, tail_ref, h_ref, pos_ref, xs_ref, buf, zbuf, sems, sem):
    i = pl.program_id(0)
    ts = h_ref.shape[0]

    def tile_copy(tile):
        half = tile % 2

        def copy(first_slot, first_row, n):
            return pltpu.make_async_copy(buf.at[half, pl.ds(first_slot, n)], xs_ref.at[pl.ds(first_row, n)],
                                         sems.at[half])
        return copy

    def wait_tile(tile):
        tile_copy(tile)(0, 0, _tile_slots_used(meta_ref, tile)).wait()

    slot = lax.broadcasted_iota(I32, (TILE_SLOTS, ts), 0)
    perm = jnp.where((slot == pos_ref[0:1, :]) | (slot == pos_ref[1:2, :]), 1.0, 0.0).astype(BF16)
    rows = _dot(perm, h_ref[...])

    @pl.when(i > 1)
    def _():
        wait_tile(i - 2)

    buf[i % 2] = _pack_rows(rows)
    _segment_copies(meta_ref, i, tile_copy(i), _start)

    @pl.when(i == pl.num_programs(0) - 1)
    def _():
        @pl.when(i > 0)
        def _():
            wait_tile(i - 1)

        wait_tile(i)
        zbuf[...] = jnp.zeros_like(zbuf)

        def tail_copies(action):
            for e in range(N_EXPERTS):
                n = pl.multiple_of(tail_ref[e], SEG_ALIGN)
                first_row = pl.multiple_of(tail_ref[N_EXPERTS + e], SEG_ALIGN)

                @pl.when(n > 0)
                def _():
                    action(pltpu.make_async_copy(zbuf.at[pl.ds(0, n)], xs_ref.at[pl.ds(first_row, n)], sem))

        tail_copies(_start)
        tail_copies(_wait)

        def zero_block(blk, carry):
            first_row = pl.multiple_of(blk * MOE_BLOCK, MOE_BLOCK)
            fill = pltpu.make_async_copy(zbuf, xs_ref.at[pl.ds(first_row, MOE_BLOCK)], sem)
            fill.start()
            fill.wait()
            return carry

        lax.fori_loop(tail_ref[2 * N_EXPERTS], xs_ref.shape[0] // MOE_BLOCK, zero_block, 0)


def _dispatch(meta, tails, h, pos, n_rows_sorted):
    t, d = h.shape
    ts = SEQ_TILE
    return pl.pallas_call(
        _dispatch_kernel,
        grid_spec=pltpu.PrefetchScalarGridSpec(
            num_scalar_prefetch=2,
            grid=(t // ts,),
            in_specs=[pl.BlockSpec((ts, d), lambda i, m, tl: (i, 0)),
                      pl.BlockSpec((2, ts), lambda i, m, tl: (0, i))],
            out_specs=pl.BlockSpec(memory_space=pl.ANY),
            scratch_shapes=[pltpu.VMEM((2, TILE_SLOTS, D_PACK), U32), pltpu.VMEM((MOE_BLOCK, D_PACK), U32),
                            pltpu.SemaphoreType.DMA((2,)), pltpu.SemaphoreType.DMA(())],
        ),
        out_shape=jax.ShapeDtypeStruct((n_rows_sorted, D_PACK), U32),
        compiler_params=pltpu.CompilerParams(
            dimension_semantics=("arbitrary",), vmem_limit_bytes=VMEM_LIMIT),
        name="dispatch",
    )(meta, tails, h, pos)


def _ffn_kernel(blk_e_ref, n_used_ref, xs_ref, w1_ref, w3_ref, w2_ref, ys_ref):
    del blk_e_ref

    @pl.when(pl.program_id(0) < n_used_ref[0])
    def _():
        lo, hi = _unpack_rows(xs_ref[...])
        xb = jnp.concatenate([lo.astype(BF16), hi.astype(BF16)], axis=1)
        a = _dot(xb, w1_ref[0, 0].astype(BF16))
        hid = (a * jax.nn.sigmoid(a)) * _dot(xb, w3_ref[0, 0].astype(BF16))
        y = _dot(hid.astype(BF16), w2_ref[0, 0].astype(BF16))
        ys_ref[...] = _pack_rows(y.astype(BF16).astype(F32))


def _ffn(layer, blk_e, n_used, xs, w1, w3, w2):
    rows = xs.shape[0]
    n_blocks = rows // MOE_BLOCK
    last = lambda i, n_used_ref: jnp.minimum(i, n_used_ref[0] - 1)
    row_spec = pl.BlockSpec((MOE_BLOCK, D_PACK), lambda i, be, nu: (last(i, nu), 0))
    w_spec = lambda a: pl.BlockSpec((1, 1) + a.shape[2:], lambda i, be, nu: (layer, be[last(i, nu)], 0, 0))
    return pl.pallas_call(
        _ffn_kernel,
        grid_spec=pltpu.PrefetchScalarGridSpec(
            num_scalar_prefetch=2,
            grid=(n_blocks,),
            in_specs=[row_spec, w_spec(w1), w_spec(w3), w_spec(w2)],
            out_specs=row_spec,
        ),
        out_shape=jax.ShapeDtypeStruct(xs.shape, U32),
        input_output_aliases={2: 0},
        compiler_params=pltpu.CompilerParams(
            dimension_semantics=("arbitrary",), vmem_limit_bytes=VMEM_LIMIT),
        name="expert_ffn",
    )(blk_e, n_used, xs, w1, w3, w2)


def _combine_kernel(meta_ref, ys_ref, x_ref, pos_ref, rw_ref, g2_ref, fg_ref, o_ref, buf, zbuf, sem,
                    *, final_norm):
    i = pl.program_id(0)
    ts = x_ref.shape[0]

    def fetch(tile):
        half = tile % 2

        def copy(first_slot, first_row, n):
            return pltpu.make_async_copy(ys_ref.at[pl.ds(first_row, n)],
                                         buf.at[half, pl.ds(first_slot, n)], sem.at[half])

        _segment_copies(meta_ref, tile, copy, _start)
        end = _tile_slots_used(meta_ref, tile)
        n_free = pl.multiple_of(TILE_SLOTS - end, SEG_ALIGN)

        @pl.when(n_free > 0)
        def _():
            pltpu.make_async_copy(zbuf.at[pl.ds(0, n_free)], buf.at[half, pl.ds(end, n_free)],
                                  sem.at[half]).start()

    @pl.when(i == 0)
    def _():
        zbuf[...] = jnp.zeros_like(zbuf)
        fetch(i)

    @pl.when(i + 1 < pl.num_programs(0))
    def _():
        fetch(i + 1)

    pltpu.make_async_copy(ys_ref.at[pl.ds(0, TILE_SLOTS)], buf.at[i % 2], sem.at[i % 2]).wait()
    lo, hi = _unpack_rows(buf[i % 2])
    rows = jnp.concatenate([lo.astype(BF16), hi.astype(BF16)], axis=1)
    slot = lax.broadcasted_iota(I32, (TILE_SLOTS, ts), 0)
    weights = (jnp.where(slot == pos_ref[0:1, :], rw_ref[0:1, :], 0.0)
               + jnp.where(slot == pos_ref[1:2, :], rw_ref[1:2, :], 0.0)).astype(BF16)
    x = x_ref[...] + g2_ref[0] * lax.dot_general(weights, rows, TN_DIMS, preferred_element_type=F32)
    if final_norm:
        ms = jnp.mean(x * x, axis=-1, keepdims=True)
        x = x * lax.rsqrt(ms + EPS) * fg_ref[...]
    o_ref[...] = x


def _combine(meta, ys, x, pos, route_w, g2, final_g, seq_len, final_norm):
    t, d = x.shape
    ts = SEQ_TILE
    tiles_per_seq = seq_len // ts
    lane_tok = pl.BlockSpec((2, ts), lambda i, m: (0, i))
    return pl.pallas_call(
        functools.partial(_combine_kernel, final_norm=final_norm),
        grid_spec=pltpu.PrefetchScalarGridSpec(
            num_scalar_prefetch=1,
            grid=(t // ts,),
            in_specs=[pl.BlockSpec(memory_space=pl.ANY),
                      pl.BlockSpec((ts, d), lambda i, m: (i, 0)),
                      lane_tok, lane_tok,
                      pl.BlockSpec((1, 1, d), lambda i, m: (i // tiles_per_seq, 0, 0)),
                      pl.BlockSpec((1, d), lambda i, m: (0, 0))],
            out_specs=pl.BlockSpec((ts, d), lambda i, m: (i, 0)),
            scratch_shapes=[pltpu.VMEM((2, TILE_SLOTS, D_PACK), U32),
                            pltpu.VMEM((TILE_SLOTS - 2 * ts, D_PACK), U32), pltpu.SemaphoreType.DMA((2,))],
        ),
        out_shape=jax.ShapeDtypeStruct((t, d), F32),
        compiler_params=pltpu.CompilerParams(
            dimension_semantics=("arbitrary",), vmem_limit_bytes=VMEM_LIMIT),
        name="combine",
    )(meta, ys, x, pos, route_w, g2, final_g.reshape(1, d))


def _dispatch_plan(tile_counts, n_assign):
    n_tiles = tile_counts.shape[0]
    seg = (tile_counts.astype(I32) + SEG_ALIGN - 1) // SEG_ALIGN * SEG_ALIGN
    totals = jnp.sum(seg, axis=0)
    padded = (totals + MOE_BLOCK - 1) // MOE_BLOCK * MOE_BLOCK
    pad_end = jnp.cumsum(padded)
    pad_start = pad_end - padded
    first_row = pad_start[None, :] + jnp.cumsum(seg, axis=0) - seg
    first_slot = jnp.cumsum(seg, axis=1) - seg
    meta = jnp.stack([seg, first_slot, first_row], axis=1).reshape(-1)
    max_rows = n_assign + n_tiles * N_EXPERTS * (SEG_ALIGN - 1)
    n_blocks = -(-max_rows // MOE_BLOCK) + N_EXPERTS
    blk_start = jnp.arange(n_blocks, dtype=I32) * MOE_BLOCK
    blk_e = jnp.minimum(jnp.sum((pad_end[None, :] <= blk_start[:, None]).astype(I32), axis=1), N_EXPERTS - 1)
    n_used = (pad_end[-1] // MOE_BLOCK).astype(I32).reshape(1)
    tails = jnp.concatenate([padded - totals, pad_start + totals, n_used])
    return meta.astype(I32), tails.astype(I32), blk_e.astype(I32), n_used, n_blocks * MOE_BLOCK


def kernel(x, c, w_mod, b_mod, w_in, w_pool, pool_scale, w_gk_up, b_gk, gla_norm_g, w_out,
           w_router, b_router, w1, w3, w2, final_g):
    bn, sn, d = x.shape
    t = bn * sn
    mod = _modulation(c, w_mod, b_mod).reshape(DEPTH, bn, 6, 1, d)

    w_main = w_in[:, :, :D_MAIN].astype(BF16)
    w_r = jnp.swapaxes(w_in[:, :, D_MAIN:], 1, 2).astype(BF16)
    zero_rank = jnp.zeros_like(w_gk_up[:, 0])
    w_gk = jnp.stack([jnp.concatenate([w_gk_up[:, 0], zero_rank], axis=1),
                      jnp.concatenate([zero_rank, w_gk_up[:, 1]], axis=1)], axis=1).astype(BF16)
    b_gk3 = b_gk.reshape(DEPTH, 2, 1, D_QK)
    w_pool_b = w_pool.astype(BF16)
    w_out_b = w_out.astype(BF16)
    wr_t = w_router.T
    wr_hi = wr_t.astype(BF16)
    wr_lo = (wr_t - wr_hi.astype(F32)).astype(BF16)
    br = b_router.reshape(N_EXPERTS, 1)

    for l in range(DEPTH):
        sh1, sc1, g1, sh2, sc2, g2 = (mod[l, :, i] for i in range(6))
        u, g, v, qf, kf, df, qb, kb, db, decf, decb = _inproj(
            x, sc1, sh1, w_main[l], w_r[l], w_gk[l], b_gk3[l])
        y_gla = _gla(qf, kf, df, qb, kb, db, v, decf, decb, g, gla_norm_g[l])
        x, h, pos, route_w, tile_counts = _mix(
            u, y_gla, x, w_pool_b[l], pool_scale[l].reshape(1, D_POOL), w_out_b[l], g1, sc2, sh2,
            wr_hi, wr_lo, br)
        meta, tails, blk_e, n_used, n_rows_sorted = _dispatch_plan(tile_counts[:, :, 0], 2 * t)
        xs = _dispatch(meta, tails, h.reshape(t, d), pos, n_rows_sorted)
        ys = _ffn(l, blk_e, n_used, xs, w1, w3, w2)
        x = _combine(meta, ys, x.reshape(t, d), pos, route_w, g2, final_g, sn, l == DEPTH - 1)
        x = x.reshape(bn, sn, d)
    return x
```

```python
import functools

import jax
import jax.numpy as jnp
from jax import lax
from jax.experimental import pallas as pl
from jax.experimental.pallas import tpu as pltpu

F32 = jnp.float32
BF16 = jnp.bfloat16
I32 = jnp.int32
U32 = jnp.uint32

D_MODEL = 1024
DEPTH = 2
D_POOL = 512
POOL_WINDOWS = (2, 4, 8, 16)
POOL_GROUP = 128
POOL_HALO = 16
D_GLA_V = 512
GLA_HEADS = 4
GLA_DK = 64
GLA_DV = 128
D_QK = GLA_HEADS * GLA_DK
GLA_RANK = 16
GATE_NORM = 16.0
GLA_CHUNK = 64
GLA_GROUP = 4
GLA_OUT_ROWS = 256
D_MAIN = D_POOL + 2 * D_QK + 2 * D_GLA_V
N_EXPERTS = 32
N_GROUPS = 4
EXPERTS_PER_GROUP = 8
D_FF = 512
EPS = 1e-6

LANE = 128
SEG_ALIGN = 8
MXU_DIM = 256
VMEM_LIMIT = 56 * 1024 * 1024

SEQ_TILE = 512
MIX_ROWS = 256
INPROJ_TILE = 1024
CUMSUM_GROUP = 256
MOD_COLS = 1536
MOE_BLOCK = 1024
TILE_SLOTS = -(-(2 * SEQ_TILE + N_EXPERTS * (SEG_ALIGN - 1)) // MXU_DIM) * MXU_DIM
D_PACK = D_MODEL // 2
META_PER_TILE = 3 * N_EXPERTS
assert 2 * SEQ_TILE // SEG_ALIGN <= 256

NT_DIMS = (((1,), (1,)), ((), ()))
TN_DIMS = (((0,), (0,)), ((), ()))


def _dot(a, b):
    return jnp.dot(a, b, preferred_element_type=F32)


def _split_bf16(a):
    hi = a.astype(BF16)
    lo = (a - hi.astype(F32)).astype(BF16)
    return hi, lo


def _mod_kernel(c_ref, w_ref, b_ref, o_ref):
    c = c_ref[...]
    c_act = (c * jax.nn.sigmoid(c)).astype(BF16)
    o_ref[0] = _dot(c_act, w_ref[0].astype(BF16)) + b_ref[0]


def _modulation(c, w_mod, b_mod):
    n_layers, d, n_out = w_mod.shape
    bn = c.shape[0]
    tn = MOD_COLS
    return pl.pallas_call(
        _mod_kernel,
        grid=(n_layers, n_out // tn),
        in_specs=[
            pl.BlockSpec((bn, d), lambda l, j: (0, 0)),
            pl.BlockSpec((1, d, tn), lambda l, j: (l, 0, j)),
            pl.BlockSpec((1, 1, tn), lambda l, j: (l, 0, j)),
        ],
        out_specs=pl.BlockSpec((1, bn, tn), lambda l, j: (l, 0, j)),
        out_shape=jax.ShapeDtypeStruct((n_layers, bn, n_out), F32),
        compiler_params=pltpu.CompilerParams(
            dimension_semantics=("arbitrary", "arbitrary"), vmem_limit_bytes=VMEM_LIMIT),
        name="modulation",
    )(c, w_mod, b_mod.reshape(n_layers, 1, n_out))


def _log_sigmoid(x):
    return jnp.minimum(x, 0.0) - jnp.log1p(jnp.exp(-jnp.abs(x)))


def _inproj_kernel(x_ref, sc_ref, sh_ref, wmain_ref, wr_ref, wgk_ref, bgk_ref,
                   u_ref, g_ref, v_ref, qf_ref, kf_ref, df_ref, qb_ref, kb_ref, db_ref,
                   decf_ref, decb_ref):
    tile = x_ref.shape[1]
    sub = CUMSUM_GROUP
    sub_chunks = sub // GLA_CHUNK
    row = lax.broadcasted_iota(I32, (sub, sub), 0)
    col = lax.broadcasted_iota(I32, (sub, sub), 1)
    same_chunk = (row // GLA_CHUNK) == (col // GLA_CHUNK)
    prefix = jnp.where(same_chunk & (col <= row), 1.0, 0.0).astype(BF16)
    suffix = jnp.where(same_chunk & (col >= row), 1.0, 0.0).astype(BF16)

    groups = range(tile // sub)
    rows = [slice(s * sub, (s + 1) * sub) for s in groups]
    chunks = [slice(s * sub_chunks, (s + 1) * sub_chunks) for s in groups]
    dirs = ((0, prefix, GLA_CHUNK - 1, qf_ref, kf_ref, df_ref, decf_ref),
            (1, suffix, 0, qb_ref, kb_ref, db_ref, decb_ref))

    zs, rs = [], []
    for s in groups:
        x = x_ref[0, rows[s], :]
        ms = jnp.mean(x * x, axis=-1, keepdims=True)
        h = x * lax.rsqrt(ms + EPS) * (1.0 + sc_ref[0]) + sh_ref[0]
        hb = h.astype(BF16)
        zs.append(_dot(hb, wmain_ref[...]))
        rs.append(lax.dot_general(wr_ref[...], hb, NT_DIMS, preferred_element_type=F32).astype(BF16))

    logas = [[_log_sigmoid(lax.dot_general(rs[s], wgk_ref[idx], TN_DIMS, preferred_element_type=F32)
                           + bgk_ref[idx]) / GATE_NORM for idx in range(2)]
             for s in groups]

    bs = []
    for s in groups:
        per_dir = []
        for idx, tri, *_ in dirs:
            hi, lo = _split_bf16(logas[s][idx])
            per_dir.append(_dot(tri, hi) + _dot(tri, lo))
        bs.append(per_dir)

    for s in groups:
        z = zs[s]
        u_ref[0, rows[s], :] = z[:, 0:D_POOL]
        q = z[:, D_POOL:D_POOL + D_QK] * (GLA_DK ** -0.5)
        k = z[:, D_POOL + D_QK:D_POOL + 2 * D_QK]
        v_ref[0, rows[s], :] = z[:, D_POOL + 2 * D_QK:D_POOL + 2 * D_QK + D_GLA_V].astype(BF16)
        g_ref[0, rows[s], :] = z[:, D_POOL + 2 * D_QK + D_GLA_V:D_MAIN]
        for idx, _, last_row, q_out, k_out, d_out, dec_out in dirs:
            b = bs[s][idx]
            b3 = b.reshape(sub_chunks, GLA_CHUNK, D_QK)
            total = b3[:, last_row:last_row + 1, :]
            q_out[0, rows[s], :] = (q * jnp.exp(b)).astype(BF16)
            k_out[0, rows[s], :] = (k * jnp.exp(-b)).astype(BF16)
            d_out[0, rows[s], :] = (k * jnp.exp(total - b3).reshape(sub, D_QK)).astype(BF16)
            dec_out[0, chunks[s], :] = jnp.exp(total).reshape(sub_chunks, D_QK)


def _inproj(x, sc, sh, w_main, w_r, w_gk, b_gk):
    bn, sn, d = x.shape
    ts = INPROJ_TILE
    n_chunks = ts // GLA_CHUNK
    tok = lambda width: pl.BlockSpec((1, ts, width), lambda b, j: (b, j, 0))
    per_batch = pl.BlockSpec((1, 1, d), lambda b, j: (b, 0, 0))
    full = lambda a: pl.BlockSpec(a.shape, lambda b, j: (0,) * a.ndim)
    dec_spec = pl.BlockSpec((1, n_chunks, D_QK), lambda b, j: (b, j, 0))
    tok_shape = lambda width, dt: jax.ShapeDtypeStruct((bn, sn, width), dt)
    dec_shape = jax.ShapeDtypeStruct((bn, sn // GLA_CHUNK, D_QK), F32)
    return pl.pallas_call(
        _inproj_kernel,
        grid=(bn, sn // ts),
        in_specs=[tok(d), per_batch, per_batch, full(w_main), full(w_r), full(w_gk), full(b_gk)],
        out_specs=[tok(D_POOL), tok(D_GLA_V), tok(D_GLA_V)] + [tok(D_QK)] * 6 + [dec_spec, dec_spec],
        out_shape=[tok_shape(D_POOL, F32), tok_shape(D_GLA_V, F32), tok_shape(D_GLA_V, BF16)]
        + [tok_shape(D_QK, BF16)] * 6 + [dec_shape, dec_shape],
        compiler_params=pltpu.CompilerParams(
            dimension_semantics=("arbitrary", "arbitrary"), vmem_limit_bytes=VMEM_LIMIT),
        name="inproj",
    )(x, sc, sh, w_main, w_r, w_gk, b_gk)


def _gla_kernel(qf_ref, kf_ref, df_ref, qb_ref, kb_ref, db_ref, v_ref, decf_ref, decb_ref,
                g_ref, ng_ref, y_ref, of_acc, ob_acc, state_f, state_b):
    sn = v_ref.shape[1]
    n_chunks = sn // GLA_CHUNK
    lane_head = lax.broadcasted_iota(I32, (GLA_CHUNK, D_QK), 1) // GLA_DK
    head_masks = [lane_head == hd for hd in range(GLA_HEADS)]
    state_blocks = (lax.broadcasted_iota(I32, (D_GLA_V, D_QK), 0) // GLA_DV
                    == lax.broadcasted_iota(I32, (D_GLA_V, D_QK), 1) // GLA_DK)
    def group_masks(own):
        masks = []
        for m in range(GLA_GROUP):
            r = lax.broadcasted_iota(I32, (GLA_CHUNK, GLA_CHUNK * (m + 1)), 0)
            c = lax.broadcasted_iota(I32, (GLA_CHUNK, GLA_CHUNK * (m + 1)), 1) - GLA_CHUNK * m
            masks.append((c < 0) | own(r, c))
        return masks

    causal = group_masks(lambda r, c: c <= r)
    anti = group_masks(lambda r, c: c > r)

    def scaled(a, factor):
        return a if factor is None else (a.astype(F32) * factor).astype(BF16)

    def product(factors):
        out = None
        for f in factors:
            out = f if out is None else out * f
        return out

    def group_out(first_chunk, order, q_ref, k_ref, d_ref, dec_ref, state, masks, o_acc):
        base = pl.multiple_of(first_chunk * GLA_CHUNK, GLA_GROUP * GLA_CHUNK)
        slab = pl.ds(base, GLA_GROUP * GLA_CHUNK)
        part = lambda a, c: a[c * GLA_CHUNK:(c + 1) * GLA_CHUNK]
        q_all, k_all, d_all, v_all = q_ref[0, slab, :], k_ref[0, slab, :], d_ref[0, slab, :], v_ref[0, slab, :]
        qe = [part(q_all, c) for c in order]
        ke = [part(k_all, c) for c in order]
        kd = [part(d_all, c) for c in order]
        vv = [part(v_all, c) for c in order]
        dec = [dec_ref[0, pl.ds(first_chunk + c, 1), :] for c in order]

        st = state[...]
        st_heads = jnp.where(state_blocks, st, 0.0).astype(BF16)
        q_in = jnp.concatenate([scaled(qe[m], product(dec[:m])) for m in range(GLA_GROUP)], axis=0)
        inter = lax.dot_general(q_in, st_heads, NT_DIMS, preferred_element_type=F32)
        k_out = jnp.concatenate([scaled(kd[c], product(dec[c + 1:])) for c in range(GLA_GROUP)], axis=0)
        upd = lax.dot_general(jnp.concatenate(vv, axis=0), k_out, TN_DIMS, preferred_element_type=F32)
        state[...] = st * product(dec) + upd

        scores = []
        for m in range(GLA_GROUP):
            keys = jnp.concatenate([scaled(kd[c], product(dec[c + 1:m])) for c in range(m)] + [ke[m]], axis=0)
            q_heads = jnp.concatenate([jnp.where(hm, qe[m], jnp.zeros_like(qe[m])) for hm in head_masks], axis=0)
            scores.append(lax.dot_general(q_heads, keys, NT_DIMS, preferred_element_type=F32))

        def finish():
            for m in range(GLA_GROUP):
                vals = jnp.concatenate(vv[:m + 1], axis=0)
                outs = []
                for hd in range(GLA_HEADS):
                    att = jnp.where(masks[m], scores[m][hd * GLA_CHUNK:(hd + 1) * GLA_CHUNK], 0.0).astype(BF16)
                    outs.append(_dot(att, vals[:, hd * GLA_DV:(hd + 1) * GLA_DV]))
                rows = pl.ds(base + order[m] * GLA_CHUNK, GLA_CHUNK)
                o_acc[rows, :] = jnp.concatenate(outs, axis=1) + part(inter, m)

        return finish

    state_f[...] = jnp.zeros_like(state_f)
    state_b[...] = jnp.zeros_like(state_b)
    n_groups = n_chunks // GLA_GROUP
    ascending = list(range(GLA_GROUP))

    def scan_body(i, carry):
        finish_f = group_out(i * GLA_GROUP, ascending, qf_ref, kf_ref, df_ref, decf_ref, state_f, causal, of_acc)
        finish_b = group_out((n_groups - 1 - i) * GLA_GROUP, ascending[::-1], qb_ref, kb_ref, db_ref, decb_ref,
                             state_b, anti, ob_acc)
        finish_f()
        finish_b()
        return carry

    lax.fori_loop(0, n_groups, scan_body, 0)

    norm_g = ng_ref[...]

    def out_body(s, carry):
        rows = pl.ds(pl.multiple_of(s * GLA_OUT_ROWS, GLA_OUT_ROWS), GLA_OUT_ROWS)
        o = of_acc[rows, :] + ob_acc[rows, :]
        gate = g_ref[0, rows, :]
        gate = gate * jax.nn.sigmoid(gate)
        outs = []
        for hd in range(GLA_HEADS):
            oh = o[:, hd * GLA_DV:(hd + 1) * GLA_DV]
            oh = oh * lax.rsqrt(jnp.mean(oh * oh, axis=-1, keepdims=True) + EPS) * norm_g
            outs.append(oh)
        y_ref[0, rows, :] = (jnp.concatenate(outs, axis=1) * gate).astype(BF16)
        return carry

    lax.fori_loop(0, sn // GLA_OUT_ROWS, out_body, 0)


def _gla(qf, kf, df, qb, kb, db, v, decf, decb, g, norm_g):
    bn, sn, _ = v.shape
    seq = lambda a: pl.BlockSpec((1,) + a.shape[1:], lambda b: (b, 0, 0))
    args = (qf, kf, df, qb, kb, db, v, decf, decb, g)
    return pl.pallas_call(
        _gla_kernel,
        grid=(bn,),
        in_specs=[seq(a) for a in args] + [pl.BlockSpec((1, GLA_DV), lambda b: (0, 0))],
        out_specs=pl.BlockSpec((1, sn, D_GLA_V), lambda b: (b, 0, 0)),
        out_shape=jax.ShapeDtypeStruct((bn, sn, D_GLA_V), BF16),
        scratch_shapes=[pltpu.VMEM((sn, D_GLA_V), F32), pltpu.VMEM((sn, D_GLA_V), F32),
                        pltpu.VMEM((D_GLA_V, D_QK), F32), pltpu.VMEM((D_GLA_V, D_QK), F32)],
        compiler_params=pltpu.CompilerParams(
            dimension_semantics=("arbitrary",), vmem_limit_bytes=VMEM_LIMIT),
        name="gla",
    )(*args, norm_g.reshape(1, GLA_DV))


def _mix_kernel(u_ref, up_ref, un_ref, yg_ref, x_ref, wp_ref, ps_ref, wo_ref, g1_ref, sc_ref, sh_ref,
                wrh_ref, wrl_ref, br_ref,
                xo_ref, h_ref, pos_ref, rw_ref, cnt_ref, ext_ref, earlier_ref, logit_ref, *, seq_len, n_tiles):
    step = pl.program_id(0)
    ts = u_ref.shape[1]
    n_seq_tiles = seq_len // ts
    j = jnp.minimum(step, n_tiles - 1) % n_seq_tiles
    sub = MIX_ROWS
    groups = range(ts // sub)
    rows = [slice(s * sub, (s + 1) * sub) for s in groups]

    @pl.when(step == 0)
    def _():
        tr = lax.broadcasted_iota(I32, (ts, ts), 0)
        tc = lax.broadcasted_iota(I32, (ts, ts), 1)
        earlier_ref[...] = jnp.where(tr < tc, 1.0, 0.0).astype(BF16)
        ext_ref[ts + 2 * POOL_HALO:, :] = jnp.zeros((POOL_HALO, D_POOL), F32)
        logit_ref[...] = jnp.zeros_like(logit_ref)

    prev_ok = jnp.where(j > 0, 1.0, 0.0)
    next_ok = jnp.where(j < n_seq_tiles - 1, 1.0, 0.0)
    ext_ref[0:POOL_HALO, :] = up_ref[0, 0] * prev_ok
    ext_ref[POOL_HALO:POOL_HALO + ts, :] = u_ref[0]
    ext_ref[POOL_HALO + ts:POOL_HALO + ts + POOL_HALO, :] = un_ref[0, 0] * next_ok
    pos = j * ts + lax.broadcasted_iota(I32, (ts, 1), 0)

    def window_sum(lanes, w):
        if w == 2:
            return ext_ref[POOL_HALO - 1:POOL_HALO - 1 + ts, lanes] + ext_ref[POOL_HALO:POOL_HALO + ts, lanes]
        first = SEG_ALIGN
        n = ts + SEG_ALIGN * (w.bit_length() - 2)
        e = ext_ref[first:first + n + SEG_ALIGN, lanes]
        p = e[0:n] + e[1:n + 1]
        step = 2
        while step < w // 2:
            n -= SEG_ALIGN
            p = p[0:n] + p[step:step + n]
            step *= 2
        off = POOL_HALO - w // 2 - first
        return p[off:off + ts] + p[off + w // 2:off + w // 2 + ts]

    pooled = []
    for gi, w in enumerate(POOL_WINDOWS):
        lanes = slice(gi * POOL_GROUP, (gi + 1) * POOL_GROUP)
        acc = window_sum(lanes, w)
        tok = u_ref[0, :, lanes]

        def clipped(edge):
            lo = jnp.clip(pos[edge] - w // 2, 0, seq_len - 1)
            hi = jnp.clip(pos[edge] - w // 2 + w - 1, 0, seq_len - 1)
            return acc[edge] / (hi - lo + 1).astype(F32) - tok[edge]

        inner = slice(POOL_HALO, ts - POOL_HALO)
        pooled.append(jnp.concatenate(
            [clipped(slice(0, POOL_HALO)), acc[inner] * (1.0 / w) - tok[inner], clipped(slice(ts - POOL_HALO, ts))],
            axis=0).astype(BF16))

    ys = []
    for s in groups:
        y_pool = [(_dot(pooled[gi][rows[s]], wp_ref[gi])
                   * ps_ref[:, gi * POOL_GROUP:(gi + 1) * POOL_GROUP]).astype(BF16)
                  for gi in range(len(POOL_WINDOWS))]
        ys.append(jnp.concatenate(y_pool + [yg_ref[0, rows[s], :]], axis=1))
    mixes = [_dot(ys[s], wo_ref[...]) for s in groups]

    logits = []
    for s in groups:
        x = x_ref[0, rows[s], :] + g1_ref[0] * mixes[s]
        xo_ref[0, rows[s], :] = x
        ms = jnp.mean(x * x, axis=-1, keepdims=True)
        h = x * lax.rsqrt(ms + EPS) * (1.0 + sc_ref[0]) + sh_ref[0]
        h_ref[0, rows[s], :] = h.astype(BF16)
        h_hi, h_lo = _split_bf16(h)
        logits.append(lax.dot_general(wrh_ref[...], h_hi, NT_DIMS, preferred_element_type=F32)
                      + lax.dot_general(wrh_ref[...], h_lo, NT_DIMS, preferred_element_type=F32)
                      + lax.dot_general(wrl_ref[...], h_hi, NT_DIMS, preferred_element_type=F32))

    score = jax.nn.sigmoid(logit_ref[...])
    logit_ref[...] = jnp.concatenate(logits, axis=1)
    sel = (score + br_ref[...]).reshape(N_GROUPS, EXPERTS_PER_GROUP, ts)
    local = lax.broadcasted_iota(I32, sel.shape, 1)
    big = EXPERTS_PER_GROUP
    m1 = jnp.max(sel, axis=1, keepdims=True)
    i1 = jnp.min(jnp.where(sel == m1, local, big), axis=1, keepdims=True)
    rest = jnp.where(local == i1, -jnp.inf, sel)
    m2 = jnp.max(rest, axis=1, keepdims=True)
    i2 = jnp.min(jnp.where(rest == m2, local, big), axis=1, keepdims=True)
    gscore = m1 + m2
    gid = lax.broadcasted_iota(I32, gscore.shape, 0)
    gbest = jnp.max(gscore, axis=0, keepdims=True)
    gsel = jnp.min(jnp.where(gscore == gbest, gid, N_GROUPS), axis=0, keepdims=True)
    in_group = gid == gsel
    pick0 = (in_group & (local == i1)).reshape(N_EXPERTS, ts)
    pick1 = (in_group & (local == i2)).reshape(N_EXPERTS, ts)
    s0 = jnp.sum(jnp.where(pick0, score, 0.0), axis=0, keepdims=True)
    s1 = jnp.sum(jnp.where(pick1, score, 0.0), axis=0, keepdims=True)
    rw_ref[0:1, :] = s0 / (s0 + s1)
    rw_ref[1:2, :] = s1 / (s0 + s1)

    onehot = jnp.where(pick0 | pick1, 1.0, 0.0).astype(BF16)
    er = lax.broadcasted_iota(I32, (N_EXPERTS, N_EXPERTS), 0)
    ec = lax.broadcasted_iota(I32, (N_EXPERTS, N_EXPERTS), 1)
    lower_exp = jnp.where(ec < er, 1.0, 0.0).astype(BF16)
    same_before = _dot(onehot, earlier_ref[...])
    cnt = jnp.sum(onehot.astype(F32), axis=1, keepdims=True)
    seg = jnp.ceil(cnt * (1.0 / SEG_ALIGN)) * SEG_ALIGN
    seg_lanes = jnp.broadcast_to(seg, (N_EXPERTS, LANE)).astype(BF16)
    first_slot = _dot(lower_exp, seg_lanes)[:, 0:1]
    slot = same_before + first_slot
    p0 = jnp.sum(jnp.where(pick0, slot, 0.0), axis=0, keepdims=True)
    p1 = jnp.sum(jnp.where(pick1, slot, 0.0), axis=0, keepdims=True)
    pos_ref[0:1, :] = p0.astype(I32)
    pos_ref[1:2, :] = p1.astype(I32)
    cnt_ref[0] = cnt


def _mix(u, y_gla, x, w_pool, pool_scale, w_out, g1, sc2, sh2, wr_hi, wr_lo, b_router):
    bn, sn, d = x.shape
    ts = SEQ_TILE
    nt = sn // ts
    hpt = ts // POOL_HALO
    n_halo = sn // POOL_HALO
    u_halo = u.reshape(bn, n_halo, POOL_HALO, D_POOL)
    n_tiles = bn * nt
    mixed = lambda s: jnp.minimum(s, n_tiles - 1)
    routed = lambda s: jnp.maximum(s - 1, 0)
    tok = lambda width: pl.BlockSpec((1, ts, width), lambda s: (mixed(s) // nt, mixed(s) % nt, 0))
    per_batch = pl.BlockSpec((1, 1, d), lambda s: (mixed(s) // nt, 0, 0))
    full = lambda a: pl.BlockSpec(a.shape, lambda s: (0,) * a.ndim)
    prev_halo = pl.BlockSpec((1, 1, POOL_HALO, D_POOL),
                             lambda s: (mixed(s) // nt, jnp.maximum(mixed(s) % nt * hpt - 1, 0), 0, 0))
    next_halo = pl.BlockSpec((1, 1, POOL_HALO, D_POOL),
                             lambda s: (mixed(s) // nt, jnp.minimum((mixed(s) % nt + 1) * hpt, n_halo - 1), 0, 0))
    lane_tok = lambda rows: pl.BlockSpec((rows, ts), lambda s: (0, routed(s)))
    t = bn * sn
    return pl.pallas_call(
        functools.partial(_mix_kernel, seq_len=sn, n_tiles=n_tiles),
        grid=(n_tiles + 1,),
        in_specs=[tok(D_POOL), prev_halo, next_halo, tok(D_GLA_V), tok(d), full(w_pool), full(pool_scale),
                  full(w_out), per_batch, per_batch, per_batch, full(wr_hi), full(wr_lo), full(b_router)],
        out_specs=[tok(d), tok(d), lane_tok(2), lane_tok(2),
                   pl.BlockSpec((1, N_EXPERTS, 1), lambda s: (routed(s), 0, 0))],
        out_shape=[jax.ShapeDtypeStruct((bn, sn, d), F32), jax.ShapeDtypeStruct((bn, sn, d), BF16),
                   jax.ShapeDtypeStruct((2, t), I32), jax.ShapeDtypeStruct((2, t), F32),
                   jax.ShapeDtypeStruct((bn * nt, N_EXPERTS, 1), F32)],
        scratch_shapes=[pltpu.VMEM((ts + 3 * POOL_HALO, D_POOL), F32), pltpu.VMEM((ts, ts), BF16),
                        pltpu.VMEM((N_EXPERTS, ts), F32)],
        compiler_params=pltpu.CompilerParams(
            dimension_semantics=("arbitrary",), vmem_limit_bytes=VMEM_LIMIT),
        name="mix_route",
    )(u, u_halo, u_halo, y_gla, x, w_pool, pool_scale, w_out, g1, sc2, sh2, wr_hi, wr_lo, b_router)


def _segment_copies(meta_ref, tile, make_copy, action):
    base = tile * META_PER_TILE
    for e in range(N_EXPERTS):
        n = pl.multiple_of(meta_ref[base + e], SEG_ALIGN)
        first_slot = pl.multiple_of(meta_ref[base + N_EXPERTS + e], SEG_ALIGN)
        first_row = pl.multiple_of(meta_ref[base + 2 * N_EXPERTS + e], SEG_ALIGN)

        @pl.when(n > 0)
        def _():
            action(make_copy(first_slot, first_row, n))


def _tile_slots_used(meta_ref, tile):
    last = tile * META_PER_TILE + N_EXPERTS - 1
    return pl.multiple_of(meta_ref[last] + meta_ref[last + N_EXPERTS], SEG_ALIGN)


def _start(copy):
    copy.start()


def _wait(copy):
    copy.wait()


def _pack_rows(x):
    lo = lax.bitcast_convert_type(x[:, :D_PACK], U32)
    hi = lax.bitcast_convert_type(x[:, D_PACK:], U32)
    return hi | (lo >> 16)


def _unpack_rows(u):
    lo = lax.bitcast_convert_type(u << 16, F32)
    hi = lax.bitcast_convert_type(u & jnp.uint32(0xFFFF0000), F32)
    return lo, hi


def _dispatch_kernel(meta_ref, tail_ref, h_ref, pos_ref, xs_ref, buf, zbuf, sems, sem):
    i = pl.program_id(0)
    ts = h_ref.shape[0]

    def tile_copy(tile):
        half = tile % 2

        def copy(first_slot, first_row, n):
            return pltpu.make_async_copy(buf.at[half, pl.ds(first_slot, n)], xs_ref.at[pl.ds(first_row, n)],
                                         sems.at[half])
        return copy

    def wait_tile(tile):
        tile_copy(tile)(0, 0, _tile_slots_used(meta_ref, tile)).wait()

    slot = lax.broadcasted_iota(I32, (TILE_SLOTS, ts), 0)
    perm = jnp.where((slot == pos_ref[0:1, :]) | (slot == pos_ref[1:2, :]), 1.0, 0.0).astype(BF16)
    rows = _dot(perm, h_ref[...])

    @pl.when(i > 1)
    def _():
        wait_tile(i - 2)

    buf[i % 2] = _pack_rows(rows)
    _segment_copies(meta_ref, i, tile_copy(i), _start)

    @pl.when(i == pl.num_programs(0) - 1)
    def _():
        @pl.when(i > 0)
        def _():
            wait_tile(i - 1)

        wait_tile(i)
        zbuf[...] = jnp.zeros_like(zbuf)

        def tail_copies(action):
            for e in range(N_EXPERTS):
                n = pl.multiple_of(tail_ref[e], SEG_ALIGN)
                first_row = pl.multiple_of(tail_ref[N_EXPERTS + e], SEG_ALIGN)

                @pl.when(n > 0)
                def _():
                    action(pltpu.make_async_copy(zbuf.at[pl.ds(0, n)], xs_ref.at[pl.ds(first_row, n)], sem))

        tail_copies(_start)
        tail_copies(_wait)

        def zero_block(blk, carry):
            first_row = pl.multiple_of(blk * MOE_BLOCK, MOE_BLOCK)
            fill = pltpu.make_async_copy(zbuf, xs_ref.at[pl.ds(first_row, MOE_BLOCK)], sem)
            fill.start()
            fill.wait()
            return carry

        lax.fori_loop(tail_ref[2 * N_EXPERTS], xs_ref.shape[0] // MOE_BLOCK, zero_block, 0)


def _dispatch(meta, tails, h, pos, n_rows_sorted):
    t, d = h.shape
    ts = SEQ_TILE
    return pl.pallas_call(
        _dispatch_kernel,
        grid_spec=pltpu.PrefetchScalarGridSpec(
            num_scalar_prefetch=2,
            grid=(t // ts,),
            in_specs=[pl.BlockSpec((ts, d), lambda i, m, tl: (i, 0)),
                      pl.BlockSpec((2, ts), lambda i, m, tl: (0, i))],
            out_specs=pl.BlockSpec(memory_space=pl.ANY),
            scratch_shapes=[pltpu.VMEM((2, TILE_SLOTS, D_PACK), U32), pltpu.VMEM((MOE_BLOCK, D_PACK), U32),
                            pltpu.SemaphoreType.DMA((2,)), pltpu.SemaphoreType.DMA(())],
        ),
        out_shape=jax.ShapeDtypeStruct((n_rows_sorted, D_PACK), U32),
        compiler_params=pltpu.CompilerParams(
            dimension_semantics=("arbitrary",), vmem_limit_bytes=VMEM_LIMIT),
        name="dispatch",
    )(meta, tails, h, pos)


def _ffn_kernel(blk_e_ref, n_used_ref, xs_ref, w1_ref, w3_ref, w2_ref, ys_ref):
    del blk_e_ref

    @pl.when(pl.program_id(0) < n_used_ref[0])
    def _():
        lo, hi = _unpack_rows(xs_ref[...])
        xb = jnp.concatenate([lo.astype(BF16), hi.astype(BF16)], axis=1)
        a = _dot(xb, w1_ref[0, 0].astype(BF16))
        hid = (a * jax.nn.sigmoid(a)) * _dot(xb, w3_ref[0, 0].astype(BF16))
        y = _dot(hid.astype(BF16), w2_ref[0, 0].astype(BF16))
        ys_ref[...] = _pack_rows(y.astype(BF16).astype(F32))


def _ffn(layer, blk_e, n_used, xs, w1, w3, w2):
    rows = xs.shape[0]
    n_blocks = rows // MOE_BLOCK
    last = lambda i, n_used_ref: jnp.minimum(i, n_used_ref[0] - 1)
    row_spec = pl.BlockSpec((MOE_BLOCK, D_PACK), lambda i, be, nu: (last(i, nu), 0))
    w_spec = lambda a: pl.BlockSpec((1, 1) + a.shape[2:], lambda i, be, nu: (layer, be[last(i, nu)], 0, 0))
    return pl.pallas_call(
        _ffn_kernel,
        grid_spec=pltpu.PrefetchScalarGridSpec(
            num_scalar_prefetch=2,
            grid=(n_blocks,),
            in_specs=[row_spec, w_spec(w1), w_spec(w3), w_spec(w2)],
            out_specs=row_spec,
        ),
        out_shape=jax.ShapeDtypeStruct(xs.shape, U32),
        input_output_aliases={2: 0},
        compiler_params=pltpu.CompilerParams(
            dimension_semantics=("arbitrary",), vmem_limit_bytes=VMEM_LIMIT),
        name="expert_ffn",
    )(blk_e, n_used, xs, w1, w3, w2)


def _combine_kernel(meta_ref, ys_ref, x_ref, pos_ref, rw_ref, g2_ref, fg_ref, o_ref, buf, zbuf, sem,
                    *, final_norm):
    i = pl.program_id(0)
    ts = x_ref.shape[0]

    def fetch(tile):
        half = tile % 2

        def copy(first_slot, first_row, n):
            return pltpu.make_async_copy(ys_ref.at[pl.ds(first_row, n)],
                                         buf.at[half, pl.ds(first_slot, n)], sem.at[half])

        _segment_copies(meta_ref, tile, copy, _start)
        end = _tile_slots_used(meta_ref, tile)
        n_free = pl.multiple_of(TILE_SLOTS - end, SEG_ALIGN)

        @pl.when(n_free > 0)
        def _():
            pltpu.make_async_copy(zbuf.at[pl.ds(0, n_free)], buf.at[half, pl.ds(end, n_free)],
                                  sem.at[half]).start()

    @pl.when(i == 0)
    def _():
        zbuf[...] = jnp.zeros_like(zbuf)
        fetch(i)

    @pl.when(i + 1 < pl.num_programs(0))
    def _():
        fetch(i + 1)

    pltpu.make_async_copy(ys_ref.at[pl.ds(0, TILE_SLOTS)], buf.at[i % 2], sem.at[i % 2]).wait()
    lo, hi = _unpack_rows(buf[i % 2])
    rows = jnp.concatenate([lo.astype(BF16), hi.astype(BF16)], axis=1)
    slot = lax.broadcasted_iota(I32, (TILE_SLOTS, ts), 0)
    weights = (jnp.where(slot == pos_ref[0:1, :], rw_ref[0:1, :], 0.0)
               + jnp.where(slot == pos_ref[1:2, :], rw_ref[1:2, :], 0.0)).astype(BF16)
    x = x_ref[...] + g2_ref[0] * lax.dot_general(weights, rows, TN_DIMS, preferred_element_type=F32)
    if final_norm:
        ms = jnp.mean(x * x, axis=-1, keepdims=True)
        x = x * lax.rsqrt(ms + EPS) * fg_ref[...]
    o_ref[...] = x


def _combine(meta, ys, x, pos, route_w, g2, final_g, seq_len, final_norm):
    t, d = x.shape
    ts = SEQ_TILE
    tiles_per_seq = seq_len // ts
    lane_tok = pl.BlockSpec((2, ts), lambda i, m: (0, i))
    return pl.pallas_call(
        functools.partial(_combine_kernel, final_norm=final_norm),
        grid_spec=pltpu.PrefetchScalarGridSpec(
            num_scalar_prefetch=1,
            grid=(t // ts,),
            in_specs=[pl.BlockSpec(memory_space=pl.ANY),
                      pl.BlockSpec((ts, d), lambda i, m: (i, 0)),
                      lane_tok, lane_tok,
                      pl.BlockSpec((1, 1, d), lambda i, m: (i // tiles_per_seq, 0, 0)),
                      pl.BlockSpec((1, d), lambda i, m: (0, 0))],
            out_specs=pl.BlockSpec((ts, d), lambda i, m: (i, 0)),
            scratch_shapes=[pltpu.VMEM((2, TILE_SLOTS, D_PACK), U32),
                            pltpu.VMEM((TILE_SLOTS - 2 * ts, D_PACK), U32), pltpu.SemaphoreType.DMA((2,))],
        ),
        out_shape=jax.ShapeDtypeStruct((t, d), F32),
        compiler_params=pltpu.CompilerParams(
            dimension_semantics=("arbitrary",), vmem_limit_bytes=VMEM_LIMIT),
        name="combine",
    )(meta, ys, x, pos, route_w, g2, final_g.reshape(1, d))


def _dispatch_plan(tile_counts, n_assign):
    n_tiles = tile_counts.shape[0]
    seg = (tile_counts.astype(I32) + SEG_ALIGN - 1) // SEG_ALIGN * SEG_ALIGN
    totals = jnp.sum(seg, axis=0)
    padded = (totals + MOE_BLOCK - 1) // MOE_BLOCK * MOE_BLOCK
    pad_end = jnp.cumsum(padded)
    pad_start = pad_end - padded
    first_row = pad_start[None, :] + jnp.cumsum(seg, axis=0) - seg
    first_slot = jnp.cumsum(seg, axis=1) - seg
    meta = jnp.stack([seg, first_slot, first_row], axis=1).reshape(-1)
    max_rows = n_assign + n_tiles * N_EXPERTS * (SEG_ALIGN - 1)
    n_blocks = -(-max_rows // MOE_BLOCK) + N_EXPERTS
    blk_start = jnp.arange(n_blocks, dtype=I32) * MOE_BLOCK
    blk_e = jnp.minimum(jnp.sum((pad_end[None, :] <= blk_start[:, None]).astype(I32), axis=1), N_EXPERTS - 1)
    n_used = (pad_end[-1] // MOE_BLOCK).astype(I32).reshape(1)
    tails = jnp.concatenate([padded - totals, pad_start + totals, n_used])
    return meta.astype(I32), tails.astype(I32), blk_e.astype(I32), n_used, n_blocks * MOE_BLOCK


def kernel(x, c, w_mod, b_mod, w_in, w_pool, pool_scale, w_gk_up, b_gk, gla_norm_g, w_out,
           w_router, b_router, w1, w3, w2, final_g):
    bn, sn, d = x.shape
    t = bn * sn
    mod = _modulation(c, w_mod, b_mod).reshape(DEPTH, bn, 6, 1, d)

    w_main = w_in[:, :, :D_MAIN].astype(BF16)
    w_r = jnp.swapaxes(w_in[:, :, D_MAIN:], 1, 2).astype(BF16)
    zero_rank = jnp.zeros_like(w_gk_up[:, 0])
    w_gk = jnp.stack([jnp.concatenate([w_gk_up[:, 0], zero_rank], axis=1),
                      jnp.concatenate([zero_rank, w_gk_up[:, 1]], axis=1)], axis=1).astype(BF16)
    b_gk3 = b_gk.reshape(DEPTH, 2, 1, D_QK)
    w_pool_b = w_pool.astype(BF16)
    w_out_b = w_out.astype(BF16)
    wr_t = w_router.T
    wr_hi = wr_t.astype(BF16)
    wr_lo = (wr_t - wr_hi.astype(F32)).astype(BF16)
    br = b_router.reshape(N_EXPERTS, 1)

    for l in range(DEPTH):
        sh1, sc1, g1, sh2, sc2, g2 = (mod[l, :, i] for i in range(6))
        u, g, v, qf, kf, df, qb, kb, db, decf, decb = _inproj(
            x, sc1, sh1, w_main[l], w_r[l], w_gk[l], b_gk3[l])
        y_gla = _gla(qf, kf, df, qb, kb, db, v, decf, decb, g, gla_norm_g[l])
        x, h, pos, route_w, tile_counts = _mix(
            u, y_gla, x, w_pool_b[l], pool_scale[l].reshape(1, D_POOL), w_out_b[l], g1, sc2, sh2,
            wr_hi, wr_lo, br)
        meta, tails, blk_e, n_used, n_rows_sorted = _dispatch_plan(tile_counts[:, :, 0], 2 * t)
        xs = _dispatch(meta, tails, h.reshape(t, d), pos, n_rows_sorted)
        ys = _ffn(l, blk_e, n_used, xs, w1, w3, w2)
        x = _combine(meta, ys, x.reshape(t, d), pos, route_w, g2, final_g, sn, l == DEPTH - 1)
        x = x.reshape(bn, sn, d)
    return x
```

```python
import functools

import jax
import jax.numpy as jnp
from jax import lax
from jax.experimental import pallas as pl
from jax.experimental.pallas import tpu as pltpu

F32 = jnp.float32
BF16 = jnp.bfloat16
I32 = jnp.int32
U32 = jnp.uint32

D_MODEL = 1024
DEPTH = 2
D_POOL = 512
POOL_WINDOWS = (2, 4, 8, 16)
POOL_GROUP = 128
POOL_HALO = 16
D_GLA_V = 512
GLA_HEADS = 4
GLA_DK = 64
GLA_DV = 128
D_QK = GLA_HEADS * GLA_DK
GLA_RANK = 16
GATE_NORM = 16.0
GLA_CHUNK = 64
GLA_GROUP = 4
GLA_OUT_ROWS = 256
D_MAIN = D_POOL + 2 * D_QK + 2 * D_GLA_V
N_EXPERTS = 32
N_GROUPS = 4
EXPERTS_PER_GROUP = 8
D_FF = 512
EPS = 1e-6

LANE = 128
SEG_ALIGN = 8
MXU_DIM = 256
VMEM_LIMIT = 56 * 1024 * 1024

SEQ_TILE = 512
MIX_ROWS = 256
INPROJ_TILE = 1024
CUMSUM_GROUP = 256
MOD_COLS = 1536
MOE_BLOCK = 1024
TILE_SLOTS = -(-(2 * SEQ_TILE + N_EXPERTS * (SEG_ALIGN - 1)) // MXU_DIM) * MXU_DIM
D_PACK = D_MODEL // 2
META_PER_TILE = 3 * N_EXPERTS
assert 2 * SEQ_TILE // SEG_ALIGN <= 256

NT_DIMS = (((1,), (1,)), ((), ()))
TN_DIMS = (((0,), (0,)), ((), ()))


def _dot(a, b):
    return jnp.dot(a, b, preferred_element_type=F32)


def _split_bf16(a):
    hi = a.astype(BF16)
    lo = (a - hi.astype(F32)).astype(BF16)
    return hi, lo


def _mod_kernel(c_ref, w_ref, b_ref, o_ref):
    c = c_ref[...]
    c_act = (c * jax.nn.sigmoid(c)).astype(BF16)
    o_ref[0] = _dot(c_act, w_ref[0].astype(BF16)) + b_ref[0]


def _modulation(c, w_mod, b_mod):
    n_layers, d, n_out = w_mod.shape
    bn = c.shape[0]
    tn = MOD_COLS
    return pl.pallas_call(
        _mod_kernel,
        grid=(n_layers, n_out // tn),
        in_specs=[
            pl.BlockSpec((bn, d), lambda l, j: (0, 0)),
            pl.BlockSpec((1, d, tn), lambda l, j: (l, 0, j)),
            pl.BlockSpec((1, 1, tn), lambda l, j: (l, 0, j)),
        ],
        out_specs=pl.BlockSpec((1, bn, tn), lambda l, j: (l, 0, j)),
        out_shape=jax.ShapeDtypeStruct((n_layers, bn, n_out), F32),
        compiler_params=pltpu.CompilerParams(
            dimension_semantics=("arbitrary", "arbitrary"), vmem_limit_bytes=VMEM_LIMIT),
        name="modulation",
    )(c, w_mod, b_mod.reshape(n_layers, 1, n_out))


def _log_sigmoid(x):
    return jnp.minimum(x, 0.0) - jnp.log1p(jnp.exp(-jnp.abs(x)))


def _inproj_kernel(x_ref, sc_ref, sh_ref, wmain_ref, wr_ref, wgk_ref, bgk_ref,
                   u_ref, g_ref, v_ref, qf_ref, kf_ref, df_ref, qb_ref, kb_ref, db_ref,
                   decf_ref, decb_ref):
    tile = x_ref.shape[1]
    sub = CUMSUM_GROUP
    sub_chunks = sub // GLA_CHUNK
    row = lax.broadcasted_iota(I32, (sub, sub), 0)
    col = lax.broadcasted_iota(I32, (sub, sub), 1)
    same_chunk = (row // GLA_CHUNK) == (col // GLA_CHUNK)
    prefix = jnp.where(same_chunk & (col <= row), 1.0, 0.0).astype(BF16)
    suffix = jnp.where(same_chunk & (col >= row), 1.0, 0.0).astype(BF16)

    groups = range(tile // sub)
    rows = [slice(s * sub, (s + 1) * sub) for s in groups]
    chunks = [slice(s * sub_chunks, (s + 1) * sub_chunks) for s in groups]
    dirs = ((0, prefix, GLA_CHUNK - 1, qf_ref, kf_ref, df_ref, decf_ref),
            (1, suffix, 0, qb_ref, kb_ref, db_ref, decb_ref))

    zs, rs = [], []
    for s in groups:
        x = x_ref[0, rows[s], :]
        ms = jnp.mean(x * x, axis=-1, keepdims=True)
        h = x * lax.rsqrt(ms + EPS) * (1.0 + sc_ref[0]) + sh_ref[0]
        hb = h.astype(BF16)
        zs.append(_dot(hb, wmain_ref[...]))
        rs.append(lax.dot_general(wr_ref[...], hb, NT_DIMS, preferred_element_type=F32).astype(BF16))

    logas = [[_log_sigmoid(lax.dot_general(rs[s], wgk_ref[idx], TN_DIMS, preferred_element_type=F32)
                           + bgk_ref[idx]) / GATE_NORM for idx in range(2)]
             for s in groups]

    bs = []
    for s in groups:
        per_dir = []
        for idx, tri, *_ in dirs:
            hi, lo = _split_bf16(logas[s][idx])
            per_dir.append(_dot(tri, hi) + _dot(tri, lo))
        bs.append(per_dir)

    for s in groups:
        z = zs[s]
        u_ref[0, rows[s], :] = z[:, 0:D_POOL]
        q = z[:, D_POOL:D_POOL + D_QK] * (GLA_DK ** -0.5)
        k = z[:, D_POOL + D_QK:D_POOL + 2 * D_QK]
        v_ref[0, rows[s], :] = z[:, D_POOL + 2 * D_QK:D_POOL + 2 * D_QK + D_GLA_V].astype(BF16)
        g_ref[0, rows[s], :] = z[:, D_POOL + 2 * D_QK + D_GLA_V:D_MAIN]
        for idx, _, last_row, q_out, k_out, d_out, dec_out in dirs:
            b = bs[s][idx]
            b3 = b.reshape(sub_chunks, GLA_CHUNK, D_QK)
            total = b3[:, last_row:last_row + 1, :]
            q_out[0, rows[s], :] = (q * jnp.exp(b)).astype(BF16)
            k_out[0, rows[s], :] = (k * jnp.exp(-b)).astype(BF16)
            d_out[0, rows[s], :] = (k * jnp.exp(total - b3).reshape(sub, D_QK)).astype(BF16)
            dec_out[0, chunks[s], :] = jnp.exp(total).reshape(sub_chunks, D_QK)


def _inproj(x, sc, sh, w_main, w_r, w_gk, b_gk):
    bn, sn, d = x.shape
    ts = INPROJ_TILE
    n_chunks = ts // GLA_CHUNK
    tok = lambda width: pl.BlockSpec((1, ts, width), lambda b, j: (b, j, 0))
    per_batch = pl.BlockSpec((1, 1, d), lambda b, j: (b, 0, 0))
    full = lambda a: pl.BlockSpec(a.shape, lambda b, j: (0,) * a.ndim)
    dec_spec = pl.BlockSpec((1, n_chunks, D_QK), lambda b, j: (b, j, 0))
    tok_shape = lambda width, dt: jax.ShapeDtypeStruct((bn, sn, width), dt)
    dec_shape = jax.ShapeDtypeStruct((bn, sn // GLA_CHUNK, D_QK), F32)
    return pl.pallas_call(
        _inproj_kernel,
        grid=(bn, sn // ts),
        in_specs=[tok(d), per_batch, per_batch, full(w_main), full(w_r), full(w_gk), full(b_gk)],
        out_specs=[tok(D_POOL), tok(D_GLA_V), tok(D_GLA_V)] + [tok(D_QK)] * 6 + [dec_spec, dec_spec],
        out_shape=[tok_shape(D_POOL, F32), tok_shape(D_GLA_V, F32), tok_shape(D_GLA_V, BF16)]
        + [tok_shape(D_QK, BF16)] * 6 + [dec_shape, dec_shape],
        compiler_params=pltpu.CompilerParams(
            dimension_semantics=("arbitrary", "arbitrary"), vmem_limit_bytes=VMEM_LIMIT),
        name="inproj",
    )(x, sc, sh, w_main, w_r, w_gk, b_gk)


def _gla_kernel(qf_ref, kf_ref, df_ref, qb_ref, kb_ref, db_ref, v_ref, decf_ref, decb_ref,
                g_ref, ng_ref, y_ref, of_acc, ob_acc, state_f, state_b):
    sn = v_ref.shape[1]
    n_chunks = sn // GLA_CHUNK
    lane_head = lax.broadcasted_iota(I32, (GLA_CHUNK, D_QK), 1) // GLA_DK
    head_masks = [lane_head == hd for hd in range(GLA_HEADS)]
    state_blocks = (lax.broadcasted_iota(I32, (D_GLA_V, D_QK), 0) // GLA_DV
                    == lax.broadcasted_iota(I32, (D_GLA_V, D_QK), 1) // GLA_DK)
    def group_masks(own):
        masks = []
        for m in range(GLA_GROUP):
            r = lax.broadcasted_iota(I32, (GLA_CHUNK, GLA_CHUNK * (m + 1)), 0)
            c = lax.broadcasted_iota(I32, (GLA_CHUNK, GLA_CHUNK * (m + 1)), 1) - GLA_CHUNK * m
            masks.append((c < 0) | own(r, c))
        return masks

    causal = group_masks(lambda r, c: c <= r)
    anti = group_masks(lambda r, c: c > r)

    def scaled(a, factor):
        return a if factor is None else (a.astype(F32) * factor).astype(BF16)

    def product(factors):
        out = None
        for f in factors:
            out = f if out is None else out * f
        return out

    def group_out(first_chunk, order, q_ref, k_ref, d_ref, dec_ref, state, masks, o_acc):
        base = pl.multiple_of(first_chunk * GLA_CHUNK, GLA_GROUP * GLA_CHUNK)
        slab = pl.ds(base, GLA_GROUP * GLA_CHUNK)
        part = lambda a, c: a[c * GLA_CHUNK:(c + 1) * GLA_CHUNK]
        q_all, k_all, d_all, v_all = q_ref[0, slab, :], k_ref[0, slab, :], d_ref[0, slab, :], v_ref[0, slab, :]
        qe = [part(q_all, c) for c in order]
        ke = [part(k_all, c) for c in order]
        kd = [part(d_all, c) for c in order]
        vv = [part(v_all, c) for c in order]
        dec = [dec_ref[0, pl.ds(first_chunk + c, 1), :] for c in order]

        st = state[...]
        st_heads = jnp.where(state_blocks, st, 0.0).astype(BF16)
        q_in = jnp.concatenate([scaled(qe[m], product(dec[:m])) for m in range(GLA_GROUP)], axis=0)
        inter = lax.dot_general(q_in, st_heads, NT_DIMS, preferred_element_type=F32)
        k_out = jnp.concatenate([scaled(kd[c], product(dec[c + 1:])) for c in range(GLA_GROUP)], axis=0)
        upd = lax.dot_general(jnp.concatenate(vv, axis=0), k_out, TN_DIMS, preferred_element_type=F32)
        state[...] = st * product(dec) + upd

        scores = []
        for m in range(GLA_GROUP):
            keys = jnp.concatenate([scaled(kd[c], product(dec[c + 1:m])) for c in range(m)] + [ke[m]], axis=0)
            q_heads = jnp.concatenate([jnp.where(hm, qe[m], jnp.zeros_like(qe[m])) for hm in head_masks], axis=0)
            scores.append(lax.dot_general(q_heads, keys, NT_DIMS, preferred_element_type=F32))

        def finish():
            for m in range(GLA_GROUP):
                vals = jnp.concatenate(vv[:m + 1], axis=0)
                outs = []
                for hd in range(GLA_HEADS):
                    att = jnp.where(masks[m], scores[m][hd * GLA_CHUNK:(hd + 1) * GLA_CHUNK], 0.0).astype(BF16)
                    outs.append(_dot(att, vals[:, hd * GLA_DV:(hd + 1) * GLA_DV]))
                rows = pl.ds(base + order[m] * GLA_CHUNK, GLA_CHUNK)
                o_acc[rows, :] = jnp.concatenate(outs, axis=1) + part(inter, m)

        return finish

    step = pl.program_id(0)
    cur = step % 2
    prev = 1 - cur

    @pl.when(step == 0)
    def _():
        of_acc[...] = jnp.zeros_like(of_acc)
        ob_acc[...] = jnp.zeros_like(ob_acc)

    state_f[...] = jnp.zeros_like(state_f)
    state_b[...] = jnp.zeros_like(state_b)
    n_groups = n_chunks // GLA_GROUP
    ascending = list(range(GLA_GROUP))
    norm_g = ng_ref[...]
    group_rows = GLA_GROUP * GLA_CHUNK

    def scan_body(i, carry):
        finish_f = group_out(i * GLA_GROUP, ascending, qf_ref, kf_ref, df_ref, decf_ref, state_f, causal,
                             of_acc.at[cur])
        finish_b = group_out((n_groups - 1 - i) * GLA_GROUP, ascending[::-1], qb_ref, kb_ref, db_ref, decb_ref,
                             state_b, anti, ob_acc.at[cur])
        rows = pl.ds(pl.multiple_of(i * group_rows, group_rows), group_rows)
        o = of_acc[prev, rows, :] + ob_acc[prev, rows, :]
        gate = g_ref[0, rows, :]
        gate = gate * jax.nn.sigmoid(gate)
        outs = []
        for hd in range(GLA_HEADS):
            oh = o[:, hd * GLA_DV:(hd + 1) * GLA_DV]
            oh = oh * lax.rsqrt(jnp.mean(oh * oh, axis=-1, keepdims=True) + EPS) * norm_g
            outs.append(oh)
        y_ref[0, rows, :] = (jnp.concatenate(outs, axis=1) * gate).astype(BF16)
        finish_f()
        finish_b()
        return carry

    lax.fori_loop(0, n_groups, scan_body, 0)


def _gla(qf, kf, df, qb, kb, db, v, decf, decb, g, norm_g):
    bn, sn, _ = v.shape
    scanned = lambda s: jnp.minimum(s, bn - 1)
    finished = lambda s: jnp.maximum(s - 1, 0)
    seq = lambda a: pl.BlockSpec((1,) + a.shape[1:], lambda s: (scanned(s), 0, 0))
    args = (qf, kf, df, qb, kb, db, v, decf, decb)
    return pl.pallas_call(
        _gla_kernel,
        grid=(bn + 1,),
        in_specs=[seq(a) for a in args]
        + [pl.BlockSpec((1, sn, D_GLA_V), lambda s: (finished(s), 0, 0)),
           pl.BlockSpec((1, GLA_DV), lambda s: (0, 0))],
        out_specs=pl.BlockSpec((1, sn, D_GLA_V), lambda s: (finished(s), 0, 0)),
        out_shape=jax.ShapeDtypeStruct((bn, sn, D_GLA_V), BF16),
        scratch_shapes=[pltpu.VMEM((2, sn, D_GLA_V), F32), pltpu.VMEM((2, sn, D_GLA_V), F32),
                        pltpu.VMEM((D_GLA_V, D_QK), F32), pltpu.VMEM((D_GLA_V, D_QK), F32)],
        compiler_params=pltpu.CompilerParams(
            dimension_semantics=("arbitrary",), vmem_limit_bytes=VMEM_LIMIT),
        name="gla",
    )(*args, g, norm_g.reshape(1, GLA_DV))


def _mix_kernel(u_ref, up_ref, un_ref, yg_ref, x_ref, wp_ref, ps_ref, wo_ref, g1_ref, sc_ref, sh_ref,
                wrh_ref, wrl_ref, br_ref,
                xo_ref, h_ref, pos_ref, rw_ref, cnt_ref, ext_ref, earlier_ref, logit_ref, *, seq_len, n_tiles):
    step = pl.program_id(0)
    ts = u_ref.shape[1]
    n_seq_tiles = seq_len // ts
    j = jnp.minimum(step, n_tiles - 1) % n_seq_tiles
    sub = MIX_ROWS
    groups = range(ts // sub)
    rows = [slice(s * sub, (s + 1) * sub) for s in groups]

    @pl.when(step == 0)
    def _():
        tr = lax.broadcasted_iota(I32, (ts, ts), 0)
        tc = lax.broadcasted_iota(I32, (ts, ts), 1)
        earlier_ref[...] = jnp.where(tr < tc, 1.0, 0.0).astype(BF16)
        ext_ref[ts + 2 * POOL_HALO:, :] = jnp.zeros((POOL_HALO, D_POOL), F32)
        logit_ref[...] = jnp.zeros_like(logit_ref)

    prev_ok = jnp.where(j > 0, 1.0, 0.0)
    next_ok = jnp.where(j < n_seq_tiles - 1, 1.0, 0.0)
    ext_ref[0:POOL_HALO, :] = up_ref[0, 0] * prev_ok
    ext_ref[POOL_HALO:POOL_HALO + ts, :] = u_ref[0]
    ext_ref[POOL_HALO + ts:POOL_HALO + ts + POOL_HALO, :] = un_ref[0, 0] * next_ok
    pos = j * ts + lax.broadcasted_iota(I32, (ts, 1), 0)

    def window_sum(lanes, w):
        if w == 2:
            return ext_ref[POOL_HALO - 1:POOL_HALO - 1 + ts, lanes] + ext_ref[POOL_HALO:POOL_HALO + ts, lanes]
        first = SEG_ALIGN
        n = ts + SEG_ALIGN * (w.bit_length() - 2)
        e = ext_ref[first:first + n + SEG_ALIGN, lanes]
        p = e[0:n] + e[1:n + 1]
        step = 2
        while step < w // 2:
            n -= SEG_ALIGN
            p = p[0:n] + p[step:step + n]
            step *= 2
        off = POOL_HALO - w // 2 - first
        return p[off:off + ts] + p[off + w // 2:off + w // 2 + ts]

    pooled = []
    for gi, w in enumerate(POOL_WINDOWS):
        lanes = slice(gi * POOL_GROUP, (gi + 1) * POOL_GROUP)
        acc = window_sum(lanes, w)
        tok = u_ref[0, :, lanes]

        def clipped(edge):
            lo = jnp.clip(pos[edge] - w // 2, 0, seq_len - 1)
            hi = jnp.clip(pos[edge] - w // 2 + w - 1, 0, seq_len - 1)
            return acc[edge] / (hi - lo + 1).astype(F32) - tok[edge]

        inner = slice(POOL_HALO, ts - POOL_HALO)
        pooled.append(jnp.concatenate(
            [clipped(slice(0, POOL_HALO)), acc[inner] * (1.0 / w) - tok[inner], clipped(slice(ts - POOL_HALO, ts))],
            axis=0).astype(BF16))

    ys = []
    for s in groups:
        y_pool = [(_dot(pooled[gi][rows[s]], wp_ref[gi])
                   * ps_ref[:, gi * POOL_GROUP:(gi + 1) * POOL_GROUP]).astype(BF16)
                  for gi in range(len(POOL_WINDOWS))]
        ys.append(jnp.concatenate(y_pool + [yg_ref[0, rows[s], :]], axis=1))
    mixes = [_dot(ys[s], wo_ref[...]) for s in groups]

    score = jax.nn.sigmoid(logit_ref[...])
    sel = (score + br_ref[...]).reshape(N_GROUPS, EXPERTS_PER_GROUP, ts)
    local = lax.broadcasted_iota(I32, sel.shape, 1)
    big = EXPERTS_PER_GROUP
    m1 = jnp.max(sel, axis=1, keepdims=True)
    i1 = jnp.min(jnp.where(sel == m1, local, big), axis=1, keepdims=True)
    rest = jnp.where(local == i1, -jnp.inf, sel)
    m2 = jnp.max(rest, axis=1, keepdims=True)
    i2 = jnp.min(jnp.where(rest == m2, local, big), axis=1, keepdims=True)
    gscore = m1 + m2
    gid = lax.broadcasted_iota(I32, gscore.shape, 0)
    gbest = jnp.max(gscore, axis=0, keepdims=True)
    gsel = jnp.min(jnp.where(gscore == gbest, gid, N_GROUPS), axis=0, keepdims=True)
    in_group = gid == gsel
    pick0 = (in_group & (local == i1)).reshape(N_EXPERTS, ts)
    pick1 = (in_group & (local == i2)).reshape(N_EXPERTS, ts)
    s0 = jnp.sum(jnp.where(pick0, score, 0.0), axis=0, keepdims=True)
    s1 = jnp.sum(jnp.where(pick1, score, 0.0), axis=0, keepdims=True)
    rw_ref[0:1, :] = s0 / (s0 + s1)
    rw_ref[1:2, :] = s1 / (s0 + s1)

    onehot = jnp.where(pick0 | pick1, 1.0, 0.0).astype(BF16)
    er = lax.broadcasted_iota(I32, (N_EXPERTS, N_EXPERTS), 0)
    ec = lax.broadcasted_iota(I32, (N_EXPERTS, N_EXPERTS), 1)
    lower_exp = jnp.where(ec < er, 1.0, 0.0).astype(BF16)
    same_before = _dot(onehot, earlier_ref[...])
    cnt = jnp.sum(onehot.astype(F32), axis=1, keepdims=True)
    seg = jnp.ceil(cnt * (1.0 / SEG_ALIGN)) * SEG_ALIGN
    seg_lanes = jnp.broadcast_to(seg, (N_EXPERTS, LANE)).astype(BF16)
    first_slot = _dot(lower_exp, seg_lanes)[:, 0:1]
    slot = same_before + first_slot
    p0 = jnp.sum(jnp.where(pick0, slot, 0.0), axis=0, keepdims=True)
    p1 = jnp.sum(jnp.where(pick1, slot, 0.0), axis=0, keepdims=True)
    pos_ref[0:1, :] = p0.astype(I32)
    pos_ref[1:2, :] = p1.astype(I32)
    cnt_ref[0] = cnt

    logits = []
    for s in groups:
        x = x_ref[0, rows[s], :] + g1_ref[0] * mixes[s]
        xo_ref[0, rows[s], :] = x
        ms = jnp.mean(x * x, axis=-1, keepdims=True)
        h = x * lax.rsqrt(ms + EPS) * (1.0 + sc_ref[0]) + sh_ref[0]
        h_ref[0, rows[s], :] = h.astype(BF16)
        h_hi, h_lo = _split_bf16(h)
        logits.append(lax.dot_general(wrh_ref[...], h_hi, NT_DIMS, preferred_element_type=F32)
                      + lax.dot_general(wrh_ref[...], h_lo, NT_DIMS, preferred_element_type=F32)
                      + lax.dot_general(wrl_ref[...], h_hi, NT_DIMS, preferred_element_type=F32))

    logit_ref[...] = jnp.concatenate(logits, axis=1)


def _mix(u, y_gla, x, w_pool, pool_scale, w_out, g1, sc2, sh2, wr_hi, wr_lo, b_router):
    bn, sn, d = x.shape
    ts = SEQ_TILE
    nt = sn // ts
    hpt = ts // POOL_HALO
    n_halo = sn // POOL_HALO
    u_halo = u.reshape(bn, n_halo, POOL_HALO, D_POOL)
    n_tiles = bn * nt
    mixed = lambda s: jnp.minimum(s, n_tiles - 1)
    routed = lambda s: jnp.maximum(s - 1, 0)
    tok = lambda width: pl.BlockSpec((1, ts, width), lambda s: (mixed(s) // nt, mixed(s) % nt, 0))
    per_batch = pl.BlockSpec((1, 1, d), lambda s: (mixed(s) // nt, 0, 0))
    full = lambda a: pl.BlockSpec(a.shape, lambda s: (0,) * a.ndim)
    prev_halo = pl.BlockSpec((1, 1, POOL_HALO, D_POOL),
                             lambda s: (mixed(s) // nt, jnp.maximum(mixed(s) % nt * hpt - 1, 0), 0, 0))
    next_halo = pl.BlockSpec((1, 1, POOL_HALO, D_POOL),
                             lambda s: (mixed(s) // nt, jnp.minimum((mixed(s) % nt + 1) * hpt, n_halo - 1), 0, 0))
    lane_tok = lambda rows: pl.BlockSpec((rows, ts), lambda s: (0, routed(s)))
    t = bn * sn
    return pl.pallas_call(
        functools.partial(_mix_kernel, seq_len=sn, n_tiles=n_tiles),
        grid=(n_tiles + 1,),
        in_specs=[tok(D_POOL), prev_halo, next_halo, tok(D_GLA_V), tok(d), full(w_pool), full(pool_scale),
                  full(w_out), per_batch, per_batch, per_batch, full(wr_hi), full(wr_lo), full(b_router)],
        out_specs=[tok(d), tok(d), lane_tok(2), lane_tok(2),
                   pl.BlockSpec((1, N_EXPERTS, 1), lambda s: (routed(s), 0, 0))],
        out_shape=[jax.ShapeDtypeStruct((bn, sn, d), F32), jax.ShapeDtypeStruct((bn, sn, d), BF16),
                   jax.ShapeDtypeStruct((2, t), I32), jax.ShapeDtypeStruct((2, t), F32),
                   jax.ShapeDtypeStruct((bn * nt, N_EXPERTS, 1), F32)],
        scratch_shapes=[pltpu.VMEM((ts + 3 * POOL_HALO, D_POOL), F32), pltpu.VMEM((ts, ts), BF16),
                        pltpu.VMEM((N_EXPERTS, ts), F32)],
        compiler_params=pltpu.CompilerParams(
            dimension_semantics=("arbitrary",), vmem_limit_bytes=VMEM_LIMIT),
        name="mix_route",
    )(u, u_halo, u_halo, y_gla, x, w_pool, pool_scale, w_out, g1, sc2, sh2, wr_hi, wr_lo, b_router)


def _segment_copies(meta_ref, tile, make_copy, action):
    base = tile * META_PER_TILE
    for e in range(N_EXPERTS):
        n = pl.multiple_of(meta_ref[base + e], SEG_ALIGN)
        first_slot = pl.multiple_of(meta_ref[base + N_EXPERTS + e], SEG_ALIGN)
        first_row = pl.multiple_of(meta_ref[base + 2 * N_EXPERTS + e], SEG_ALIGN)

        @pl.when(n > 0)
        def _():
            action(make_copy(first_slot, first_row, n))


def _tile_slots_used(meta_ref, tile):
    last = tile * META_PER_TILE + N_EXPERTS - 1
    return pl.multiple_of(meta_ref[last] + meta_ref[last + N_EXPERTS], SEG_ALIGN)


def _start(copy):
    copy.start()


def _wait(copy):
    copy.wait()


def _pack_rows(x):
    lo = lax.bitcast_convert_type(x[:, :D_PACK], U32)
    hi = lax.bitcast_convert_type(x[:, D_PACK:], U32)
    return hi | (lo >> 16)


def _unpack_rows(u):
    lo = lax.bitcast_convert_type(u << 16, F32)
    hi = lax.bitcast_convert_type(u & jnp.uint32(0xFFFF0000), F32)
    return lo, hi


def _dispatch_kernel(meta_ref, tail_ref, h_ref, pos_ref, xs_ref, buf, zbuf, sems, sem):
    i = pl.program_id(0)
    ts = h_ref.shape[0]

    def tile_copy(tile):
        half = tile % 2

        def copy(first_slot, first_row, n):
            return pltpu.make_async_copy(buf.at[half, pl.ds(first_slot, n)], xs_ref.at[pl.ds(first_row, n)],
                                         sems.at[half])
        return copy

    def wait_tile(tile):
        tile_copy(tile)(0, 0, _tile_slots_used(meta_ref, tile)).wait()

    slot = lax.broadcasted_iota(I32, (TILE_SLOTS, ts), 0)
    perm = jnp.where((slot == pos_ref[0:1, :]) | (slot == pos_ref[1:2, :]), 1.0, 0.0).astype(BF16)
    rows = _dot(perm, h_ref[...])

    @pl.when(i > 1)
    def _():
        wait_tile(i - 2)

    buf[i % 2] = _pack_rows(rows)
    _segment_copies(meta_ref, i, tile_copy(i), _start)

    @pl.when(i == pl.num_programs(0) - 1)
    def _():
        @pl.when(i > 0)
        def _():
            wait_tile(i - 1)

        wait_tile(i)
        zbuf[...] = jnp.zeros_like(zbuf)

        def tail_copies(action):
            for e in range(N_EXPERTS):
                n = pl.multiple_of(tail_ref[e], SEG_ALIGN)
                first_row = pl.multiple_of(tail_ref[N_EXPERTS + e], SEG_ALIGN)

                @pl.when(n > 0)
                def _():
                    action(pltpu.make_async_copy(zbuf.at[pl.ds(0, n)], xs_ref.at[pl.ds(first_row, n)], sem))

        tail_copies(_start)
        tail_copies(_wait)

        def zero_block(blk, carry):
            first_row = pl.multiple_of(blk * MOE_BLOCK, MOE_BLOCK)
            fill = pltpu.make_async_copy(zbuf, xs_ref.at[pl.ds(first_row, MOE_BLOCK)], sem)
            fill.start()
            fill.wait()
            return carry

        lax.fori_loop(tail_ref[2 * N_EXPERTS], xs_ref.shape[0] // MOE_BLOCK, zero_block, 0)


def _dispatch(meta, tails, h, pos, n_rows_sorted):
    t, d = h.shape
    ts = SEQ_TILE
    return pl.pallas_call(
        _dispatch_kernel,
        grid_spec=pltpu.PrefetchScalarGridSpec(
            num_scalar_prefetch=2,
            grid=(t // ts,),
            in_specs=[pl.BlockSpec((ts, d), lambda i, m, tl: (i, 0)),
                      pl.BlockSpec((2, ts), lambda i, m, tl: (0, i))],
            out_specs=pl.BlockSpec(memory_space=pl.ANY),
            scratch_shapes=[pltpu.VMEM((2, TILE_SLOTS, D_PACK), U32), pltpu.VMEM((MOE_BLOCK, D_PACK), U32),
                            pltpu.SemaphoreType.DMA((2,)), pltpu.SemaphoreType.DMA(())],
        ),
        out_shape=jax.ShapeDtypeStruct((n_rows_sorted, D_PACK), U32),
        compiler_params=pltpu.CompilerParams(
            dimension_semantics=("arbitrary",), vmem_limit_bytes=VMEM_LIMIT),
        name="dispatch",
    )(meta, tails, h, pos)


def _ffn_kernel(blk_e_ref, n_used_ref, xs_ref, w1_ref, w3_ref, w2_ref, ys_ref):
    del blk_e_ref

    @pl.when(pl.program_id(0) < n_used_ref[0])
    def _():
        lo, hi = _unpack_rows(xs_ref[...])
        xb = jnp.concatenate([lo.astype(BF16), hi.astype(BF16)], axis=1)
        a = _dot(xb, w1_ref[0, 0].astype(BF16))
        hid = (a * jax.nn.sigmoid(a)) * _dot(xb, w3_ref[0, 0].astype(BF16))
        y = _dot(hid.astype(BF16), w2_ref[0, 0].astype(BF16))
        ys_ref[...] = _pack_rows(y.astype(BF16).astype(F32))


def _ffn(layer, blk_e, n_used, xs, w1, w3, w2):
    rows = xs.shape[0]
    n_blocks = rows // MOE_BLOCK
    last = lambda i, n_used_ref: jnp.minimum(i, n_used_ref[0] - 1)
    row_spec = pl.BlockSpec((MOE_BLOCK, D_PACK), lambda i, be, nu: (last(i, nu), 0))
    w_spec = lambda a: pl.BlockSpec((1, 1) + a.shape[2:], lambda i, be, nu: (layer, be[last(i, nu)], 0, 0))
    return pl.pallas_call(
        _ffn_kernel,
        grid_spec=pltpu.PrefetchScalarGridSpec(
            num_scalar_prefetch=2,
            grid=(n_blocks,),
            in_specs=[row_spec, w_spec(w1), w_spec(w3), w_spec(w2)],
            out_specs=row_spec,
        ),
        out_shape=jax.ShapeDtypeStruct(xs.shape, U32),
        input_output_aliases={2: 0},
        compiler_params=pltpu.CompilerParams(
            dimension_semantics=("arbitrary",), vmem_limit_bytes=VMEM_LIMIT),
        name="expert_ffn",
    )(blk_e, n_used, xs, w1, w3, w2)


def _combine_kernel(meta_ref, ys_ref, x_ref, pos_ref, rw_ref, g2_ref, fg_ref, o_ref, buf, zbuf, sem,
                    *, final_norm):
    i = pl.program_id(0)
    ts = x_ref.shape[0]

    def fetch(tile):
        half = tile % 2

        def copy(first_slot, first_row, n):
            return pltpu.make_async_copy(ys_ref.at[pl.ds(first_row, n)],
                                         buf.at[half, pl.ds(first_slot, n)], sem.at[half])

        _segment_copies(meta_ref, tile, copy, _start)
        end = _tile_slots_used(meta_ref, tile)
        n_free = pl.multiple_of(TILE_SLOTS - end, SEG_ALIGN)

        @pl.when(n_free > 0)
        def _():
            pltpu.make_async_copy(zbuf.at[pl.ds(0, n_free)], buf.at[half, pl.ds(end, n_free)],
                                  sem.at[half]).start()

    @pl.when(i == 0)
    def _():
        zbuf[...] = jnp.zeros_like(zbuf)
        fetch(i)

    @pl.when(i + 1 < pl.num_programs(0))
    def _():
        fetch(i + 1)

    pltpu.make_async_copy(ys_ref.at[pl.ds(0, TILE_SLOTS)], buf.at[i % 2], sem.at[i % 2]).wait()
    lo, hi = _unpack_rows(buf[i % 2])
    rows = jnp.concatenate([lo.astype(BF16), hi.astype(BF16)], axis=1)
    slot = lax.broadcasted_iota(I32, (TILE_SLOTS, ts), 0)
    weights = (jnp.where(slot == pos_ref[0:1, :], rw_ref[0:1, :], 0.0)
               + jnp.where(slot == pos_ref[1:2, :], rw_ref[1:2, :], 0.0)).astype(BF16)
    x = x_ref[...] + g2_ref[0] * lax.dot_general(weights, rows, TN_DIMS, preferred_element_type=F32)
    if final_norm:
        ms = jnp.mean(x * x, axis=-1, keepdims=True)
        x = x * lax.rsqrt(ms + EPS) * fg_ref[...]
    o_ref[...] = x


def _combine(meta, ys, x, pos, route_w, g2, final_g, seq_len, final_norm):
    t, d = x.shape
    ts = SEQ_TILE
    tiles_per_seq = seq_len // ts
    lane_tok = pl.BlockSpec((2, ts), lambda i, m: (0, i))
    return pl.pallas_call(
        functools.partial(_combine_kernel, final_norm=final_norm),
        grid_spec=pltpu.PrefetchScalarGridSpec(
            num_scalar_prefetch=1,
            grid=(t // ts,),
            in_specs=[pl.BlockSpec(memory_space=pl.ANY),
                      pl.BlockSpec((ts, d), lambda i, m: (i, 0)),
                      lane_tok, lane_tok,
                      pl.BlockSpec((1, 1, d), lambda i, m: (i // tiles_per_seq, 0, 0)),
                      pl.BlockSpec((1, d), lambda i, m: (0, 0))],
            out_specs=pl.BlockSpec((ts, d), lambda i, m: (i, 0)),
            scratch_shapes=[pltpu.VMEM((2, TILE_SLOTS, D_PACK), U32),
                            pltpu.VMEM((TILE_SLOTS - 2 * ts, D_PACK), U32), pltpu.SemaphoreType.DMA((2,))],
        ),
        out_shape=jax.ShapeDtypeStruct((t, d), F32),
        compiler_params=pltpu.CompilerParams(
            dimension_semantics=("arbitrary",), vmem_limit_bytes=VMEM_LIMIT),
        name="combine",
    )(meta, ys, x, pos, route_w, g2, final_g.reshape(1, d))


def _dispatch_plan(tile_counts, n_assign):
    n_tiles = tile_counts.shape[0]
    seg = (tile_counts.astype(I32) + SEG_ALIGN - 1) // SEG_ALIGN * SEG_ALIGN
    totals = jnp.sum(seg, axis=0)
    padded = (totals + MOE_BLOCK - 1) // MOE_BLOCK * MOE_BLOCK
    pad_end = jnp.cumsum(padded)
    pad_start = pad_end - padded
    first_row = pad_start[None, :] + jnp.cumsum(seg, axis=0) - seg
    first_slot = jnp.cumsum(seg, axis=1) - seg
    meta = jnp.stack([seg, first_slot, first_row], axis=1).reshape(-1)
    max_rows = n_assign + n_tiles * N_EXPERTS * (SEG_ALIGN - 1)
    n_blocks = -(-max_rows // MOE_BLOCK) + N_EXPERTS
    blk_start = jnp.arange(n_blocks, dtype=I32) * MOE_BLOCK
    blk_e = jnp.minimum(jnp.sum((pad_end[None, :] <= blk_start[:, None]).astype(I32), axis=1), N_EXPERTS - 1)
    n_used = (pad_end[-1] // MOE_BLOCK).astype(I32).reshape(1)
    tails = jnp.concatenate([padded - totals, pad_start + totals, n_used])
    return meta.astype(I32), tails.astype(I32), blk_e.astype(I32), n_used, n_blocks * MOE_BLOCK


def kernel(x, c, w_mod, b_mod, w_in, w_pool, pool_scale, w_gk_up, b_gk, gla_norm_g, w_out,
           w_router, b_router, w1, w3, w2, final_g):
    bn, sn, d = x.shape
    t = bn * sn
    mod = _modulation(c, w_mod, b_mod).reshape(DEPTH, bn, 6, 1, d)

    w_main = w_in[:, :, :D_MAIN].astype(BF16)
    w_r = jnp.swapaxes(w_in[:, :, D_MAIN:], 1, 2).astype(BF16)
    zero_rank = jnp.zeros_like(w_gk_up[:, 0])
    w_gk = jnp.stack([jnp.concatenate([w_gk_up[:, 0], zero_rank], axis=1),
                      jnp.concatenate([zero_rank, w_gk_up[:, 1]], axis=1)], axis=1).astype(BF16)
    b_gk3 = b_gk.reshape(DEPTH, 2, 1, D_QK)
    w_pool_b = w_pool.astype(BF16)
    w_out_b = w_out.astype(BF16)
    wr_t = w_router.T
    wr_hi = wr_t.astype(BF16)
    wr_lo = (wr_t - wr_hi.astype(F32)).astype(BF16)
    br = b_router.reshape(N_EXPERTS, 1)

    for l in range(DEPTH):
        sh1, sc1, g1, sh2, sc2, g2 = (mod[l, :, i] for i in range(6))
        u, g, v, qf, kf, df, qb, kb, db, decf, decb = _inproj(
            x, sc1, sh1, w_main[l], w_r[l], w_gk[l], b_gk3[l])
        y_gla = _gla(qf, kf, df, qb, kb, db, v, decf, decb, g, gla_norm_g[l])
        x, h, pos, route_w, tile_counts = _mix(
            u, y_gla, x, w_pool_b[l], pool_scale[l].reshape(1, D_POOL), w_out_b[l], g1, sc2, sh2,
            wr_hi, wr_lo, br)
        meta, tails, blk_e, n_used, n_rows_sorted = _dispatch_plan(tile_counts[:, :, 0], 2 * t)
        xs = _dispatch(meta, tails, h.reshape(t, d), pos, n_rows_sorted)
        ys = _ffn(l, blk_e, n_used, xs, w1, w3, w2)
        x = _combine(meta, ys, x.reshape(t, d), pos, route_w, g2, final_g, sn, l == DEPTH - 1)
        x = x.reshape(bn, sn, d)
    return x
```

```python
import functools

import jax
import jax.numpy as jnp
from jax import lax
from jax.experimental import pallas as pl
from jax.experimental.pallas import tpu as pltpu

F32 = jnp.float32
BF16 = jnp.bfloat16
I32 = jnp.int32
U32 = jnp.uint32

D_MODEL = 1024
DEPTH = 2
D_POOL = 512
POOL_WINDOWS = (2, 4, 8, 16)
POOL_GROUP = 128
POOL_HALO = 16
D_GLA_V = 512
GLA_HEADS = 4
GLA_DK = 64
GLA_DV = 128
D_QK = GLA_HEADS * GLA_DK
GLA_RANK = 16
GATE_NORM = 16.0
GLA_CHUNK = 64
GLA_GROUP = 4
GLA_OUT_ROWS = 256
D_MAIN = D_POOL + 2 * D_QK + 2 * D_GLA_V
N_EXPERTS = 32
N_GROUPS = 4
EXPERTS_PER_GROUP = 8
D_FF = 512
EPS = 1e-6

LANE = 128
SEG_ALIGN = 8
MXU_DIM = 256
VMEM_LIMIT = 56 * 1024 * 1024

SEQ_TILE = 512
MIX_ROWS = 128
INPROJ_TILE = 1024
CUMSUM_GROUP = 256
MOD_COLS = 1536
MOE_BLOCK = 1024
TILE_SLOTS = -(-(2 * SEQ_TILE + N_EXPERTS * (SEG_ALIGN - 1)) // MXU_DIM) * MXU_DIM
D_PACK = D_MODEL // 2
META_PER_TILE = 3 * N_EXPERTS
assert 2 * SEQ_TILE // SEG_ALIGN <= 256

NT_DIMS = (((1,), (1,)), ((), ()))
TN_DIMS = (((0,), (0,)), ((), ()))


def _dot(a, b):
    return jnp.dot(a, b, preferred_element_type=F32)


def _split_bf16(a):
    hi = a.astype(BF16)
    lo = (a - hi.astype(F32)).astype(BF16)
    return hi, lo


def _mod_kernel(c_ref, w_ref, b_ref, o_ref):
    c = c_ref[...]
    c_act = (c * jax.nn.sigmoid(c)).astype(BF16)
    o_ref[0] = _dot(c_act, w_ref[0].astype(BF16)) + b_ref[0]


def _modulation(c, w_mod, b_mod):
    n_layers, d, n_out = w_mod.shape
    bn = c.shape[0]
    tn = MOD_COLS
    return pl.pallas_call(
        _mod_kernel,
        grid=(n_layers, n_out // tn),
        in_specs=[
            pl.BlockSpec((bn, d), lambda l, j: (0, 0)),
            pl.BlockSpec((1, d, tn), lambda l, j: (l, 0, j)),
            pl.BlockSpec((1, 1, tn), lambda l, j: (l, 0, j)),
        ],
        out_specs=pl.BlockSpec((1, bn, tn), lambda l, j: (l, 0, j)),
        out_shape=jax.ShapeDtypeStruct((n_layers, bn, n_out), F32),
        compiler_params=pltpu.CompilerParams(
            dimension_semantics=("arbitrary", "arbitrary"), vmem_limit_bytes=VMEM_LIMIT),
        name="modulation",
    )(c, w_mod, b_mod.reshape(n_layers, 1, n_out))


def _log_sigmoid(x):
    return jnp.minimum(x, 0.0) - jnp.log1p(jnp.exp(-jnp.abs(x)))


def _inproj_kernel(x_ref, sc_ref, sh_ref, wmain_ref, wr_ref, wgk_ref, bgk_ref,
                   u_ref, g_ref, v_ref, qf_ref, kf_ref, df_ref, qb_ref, kb_ref, db_ref,
                   decf_ref, decb_ref):
    tile = x_ref.shape[1]
    sub = CUMSUM_GROUP
    sub_chunks = sub // GLA_CHUNK
    row = lax.broadcasted_iota(I32, (sub, sub), 0)
    col = lax.broadcasted_iota(I32, (sub, sub), 1)
    same_chunk = (row // GLA_CHUNK) == (col // GLA_CHUNK)
    prefix = jnp.where(same_chunk & (col <= row), 1.0, 0.0).astype(BF16)
    suffix = jnp.where(same_chunk & (col >= row), 1.0, 0.0).astype(BF16)

    groups = range(tile // sub)
    rows = [slice(s * sub, (s + 1) * sub) for s in groups]
    chunks = [slice(s * sub_chunks, (s + 1) * sub_chunks) for s in groups]
    dirs = ((0, prefix, GLA_CHUNK - 1, qf_ref, kf_ref, df_ref, decf_ref),
            (1, suffix, 0, qb_ref, kb_ref, db_ref, decb_ref))

    zs, rs = [], []
    for s in groups:
        x = x_ref[0, rows[s], :]
        ms = jnp.mean(x * x, axis=-1, keepdims=True)
        h = x * lax.rsqrt(ms + EPS) * (1.0 + sc_ref[0]) + sh_ref[0]
        hb = h.astype(BF16)
        zs.append(_dot(hb, wmain_ref[...]))
        rs.append(lax.dot_general(wr_ref[...], hb, NT_DIMS, preferred_element_type=F32).astype(BF16))

    logas = [[_log_sigmoid(lax.dot_general(rs[s], wgk_ref[idx], TN_DIMS, preferred_element_type=F32)
                           + bgk_ref[idx]) / GATE_NORM for idx in range(2)]
             for s in groups]

    bs = []
    for s in groups:
        per_dir = []
        for idx, tri, *_ in dirs:
            hi, lo = _split_bf16(logas[s][idx])
            per_dir.append(_dot(tri, hi) + _dot(tri, lo))
        bs.append(per_dir)

    for s in groups:
        z = zs[s]
        u_ref[0, rows[s], :] = z[:, 0:D_POOL]
        q = z[:, D_POOL:D_POOL + D_QK] * (GLA_DK ** -0.5)
        k = z[:, D_POOL + D_QK:D_POOL + 2 * D_QK]
        v_ref[0, rows[s], :] = z[:, D_POOL + 2 * D_QK:D_POOL + 2 * D_QK + D_GLA_V].astype(BF16)
        g_ref[0, rows[s], :] = z[:, D_POOL + 2 * D_QK + D_GLA_V:D_MAIN]
        for idx, _, last_row, q_out, k_out, d_out, dec_out in dirs:
            b = bs[s][idx]
            b3 = b.reshape(sub_chunks, GLA_CHUNK, D_QK)
            total = b3[:, last_row:last_row + 1, :]
            q_out[0, rows[s], :] = (q * jnp.exp(b)).astype(BF16)
            k_out[0, rows[s], :] = (k * jnp.exp(-b)).astype(BF16)
            d_out[0, rows[s], :] = (k * jnp.exp(total - b3).reshape(sub, D_QK)).astype(BF16)
            dec_out[0, chunks[s], :] = jnp.exp(total).reshape(sub_chunks, D_QK)


def _inproj(x, sc, sh, w_main, w_r, w_gk, b_gk):
    bn, sn, d = x.shape
    ts = INPROJ_TILE
    n_chunks = ts // GLA_CHUNK
    tok = lambda width: pl.BlockSpec((1, ts, width), lambda b, j: (b, j, 0))
    per_batch = pl.BlockSpec((1, 1, d), lambda b, j: (b, 0, 0))
    full = lambda a: pl.BlockSpec(a.shape, lambda b, j: (0,) * a.ndim)
    dec_spec = pl.BlockSpec((1, n_chunks, D_QK), lambda b, j: (b, j, 0))
    tok_shape = lambda width, dt: jax.ShapeDtypeStruct((bn, sn, width), dt)
    dec_shape = jax.ShapeDtypeStruct((bn, sn // GLA_CHUNK, D_QK), F32)
    return pl.pallas_call(
        _inproj_kernel,
        grid=(bn, sn // ts),
        in_specs=[tok(d), per_batch, per_batch, full(w_main), full(w_r), full(w_gk), full(b_gk)],
        out_specs=[tok(D_POOL), tok(D_GLA_V), tok(D_GLA_V)] + [tok(D_QK)] * 6 + [dec_spec, dec_spec],
        out_shape=[tok_shape(D_POOL, F32), tok_shape(D_GLA_V, F32), tok_shape(D_GLA_V, BF16)]
        + [tok_shape(D_QK, BF16)] * 6 + [dec_shape, dec_shape],
        compiler_params=pltpu.CompilerParams(
            dimension_semantics=("arbitrary", "arbitrary"), vmem_limit_bytes=VMEM_LIMIT),
        name="inproj",
    )(x, sc, sh, w_main, w_r, w_gk, b_gk)


def _gla_kernel(qf_ref, kf_ref, df_ref, qb_ref, kb_ref, db_ref, v_ref, decf_ref, decb_ref,
                g_ref, ng_ref, y_ref, of_acc, ob_acc, state_f, state_b):
    sn = v_ref.shape[1]
    n_chunks = sn // GLA_CHUNK
    lane_head = lax.broadcasted_iota(I32, (GLA_CHUNK, D_QK), 1) // GLA_DK
    head_masks = [lane_head == hd for hd in range(GLA_HEADS)]
    state_blocks = (lax.broadcasted_iota(I32, (D_GLA_V, D_QK), 0) // GLA_DV
                    == lax.broadcasted_iota(I32, (D_GLA_V, D_QK), 1) // GLA_DK)
    def group_masks(own):
        masks = []
        for m in range(GLA_GROUP):
            r = lax.broadcasted_iota(I32, (GLA_CHUNK, GLA_CHUNK * (m + 1)), 0)
            c = lax.broadcasted_iota(I32, (GLA_CHUNK, GLA_CHUNK * (m + 1)), 1) - GLA_CHUNK * m
            masks.append((c < 0) | own(r, c))
        return masks

    causal = group_masks(lambda r, c: c <= r)
    anti = group_masks(lambda r, c: c > r)

    def scaled(a, factor):
        return a if factor is None else (a.astype(F32) * factor).astype(BF16)

    def product(factors):
        out = None
        for f in factors:
            out = f if out is None else out * f
        return out

    def group_out(first_chunk, order, q_ref, k_ref, d_ref, dec_ref, state, masks, o_acc):
        base = pl.multiple_of(first_chunk * GLA_CHUNK, GLA_GROUP * GLA_CHUNK)
        slab = pl.ds(base, GLA_GROUP * GLA_CHUNK)
        part = lambda a, c: a[c * GLA_CHUNK:(c + 1) * GLA_CHUNK]
        q_all, k_all, d_all, v_all = q_ref[0, slab, :], k_ref[0, slab, :], d_ref[0, slab, :], v_ref[0, slab, :]
        qe = [part(q_all, c) for c in order]
        ke = [part(k_all, c) for c in order]
        kd = [part(d_all, c) for c in order]
        vv = [part(v_all, c) for c in order]
        dec = [dec_ref[0, pl.ds(first_chunk + c, 1), :] for c in order]

        st = state[...]
        st_heads = jnp.where(state_blocks, st, 0.0).astype(BF16)
        q_in = jnp.concatenate([scaled(qe[m], product(dec[:m])) for m in range(GLA_GROUP)], axis=0)
        inter = lax.dot_general(q_in, st_heads, NT_DIMS, preferred_element_type=F32)
        k_out = jnp.concatenate([scaled(kd[c], product(dec[c + 1:])) for c in range(GLA_GROUP)], axis=0)
        upd = lax.dot_general(jnp.concatenate(vv, axis=0), k_out, TN_DIMS, preferred_element_type=F32)
        state[...] = st * product(dec) + upd

        scores = []
        for m in range(GLA_GROUP):
            keys = jnp.concatenate([scaled(kd[c], product(dec[c + 1:m])) for c in range(m)] + [ke[m]], axis=0)
            q_heads = jnp.concatenate([jnp.where(hm, qe[m], jnp.zeros_like(qe[m])) for hm in head_masks], axis=0)
            scores.append(lax.dot_general(q_heads, keys, NT_DIMS, preferred_element_type=F32))

        def finish():
            for m in range(GLA_GROUP):
                vals = jnp.concatenate(vv[:m + 1], axis=0)
                outs = []
                for hd in range(GLA_HEADS):
                    att = jnp.where(masks[m], scores[m][hd * GLA_CHUNK:(hd + 1) * GLA_CHUNK], 0.0).astype(BF16)
                    outs.append(_dot(att, vals[:, hd * GLA_DV:(hd + 1) * GLA_DV]))
                rows = pl.ds(base + order[m] * GLA_CHUNK, GLA_CHUNK)
                o_acc[rows, :] = jnp.concatenate(outs, axis=1) + part(inter, m)

        return finish

    step = pl.program_id(0)
    cur = step % 2
    prev = 1 - cur

    @pl.when(step == 0)
    def _():
        of_acc[...] = jnp.zeros_like(of_acc)
        ob_acc[...] = jnp.zeros_like(ob_acc)

    state_f[...] = jnp.zeros_like(state_f)
    state_b[...] = jnp.zeros_like(state_b)
    n_groups = n_chunks // GLA_GROUP
    ascending = list(range(GLA_GROUP))
    norm_g = ng_ref[...]
    group_rows = GLA_GROUP * GLA_CHUNK

    def scan_body(i, carry):
        finish_f = group_out(i * GLA_GROUP, ascending, qf_ref, kf_ref, df_ref, decf_ref, state_f, causal,
                             of_acc.at[cur])
        finish_b = group_out((n_groups - 1 - i) * GLA_GROUP, ascending[::-1], qb_ref, kb_ref, db_ref, decb_ref,
                             state_b, anti, ob_acc.at[cur])
        rows = pl.ds(pl.multiple_of(i * group_rows, group_rows), group_rows)
        o = of_acc[prev, rows, :] + ob_acc[prev, rows, :]
        gate = g_ref[0, rows, :]
        gate = gate * jax.nn.sigmoid(gate)
        outs = []
        for hd in range(GLA_HEADS):
            oh = o[:, hd * GLA_DV:(hd + 1) * GLA_DV]
            oh = oh * lax.rsqrt(jnp.mean(oh * oh, axis=-1, keepdims=True) + EPS) * norm_g
            outs.append(oh)
        y_ref[0, rows, :] = (jnp.concatenate(outs, axis=1) * gate).astype(BF16)
        finish_f()
        finish_b()
        return carry

    lax.fori_loop(0, n_groups, scan_body, 0)


def _gla(qf, kf, df, qb, kb, db, v, decf, decb, g, norm_g):
    bn, sn, _ = v.shape
    scanned = lambda s: jnp.minimum(s, bn - 1)
    finished = lambda s: jnp.maximum(s - 1, 0)
    seq = lambda a: pl.BlockSpec((1,) + a.shape[1:], lambda s: (scanned(s), 0, 0))
    args = (qf, kf, df, qb, kb, db, v, decf, decb)
    return pl.pallas_call(
        _gla_kernel,
        grid=(bn + 1,),
        in_specs=[seq(a) for a in args]
        + [pl.BlockSpec((1, sn, D_GLA_V), lambda s: (finished(s), 0, 0)),
           pl.BlockSpec((1, GLA_DV), lambda s: (0, 0))],
        out_specs=pl.BlockSpec((1, sn, D_GLA_V), lambda s: (finished(s), 0, 0)),
        out_shape=jax.ShapeDtypeStruct((bn, sn, D_GLA_V), BF16),
        scratch_shapes=[pltpu.VMEM((2, sn, D_GLA_V), F32), pltpu.VMEM((2, sn, D_GLA_V), F32),
                        pltpu.VMEM((D_GLA_V, D_QK), F32), pltpu.VMEM((D_GLA_V, D_QK), F32)],
        compiler_params=pltpu.CompilerParams(
            dimension_semantics=("arbitrary",), vmem_limit_bytes=VMEM_LIMIT),
        name="gla",
    )(*args, g, norm_g.reshape(1, GLA_DV))


def _mix_kernel(u_ref, up_ref, un_ref, yg_ref, x_ref, wp_ref, ps_ref, wo_ref, g1_ref, sc_ref, sh_ref,
                wrh_ref, wrl_ref, br_ref,
                xo_ref, h_ref, pos_ref, rw_ref, cnt_ref, ext_ref, earlier_ref, logit_ref, *, seq_len, n_tiles):
    step = pl.program_id(0)
    ts = u_ref.shape[1]
    n_seq_tiles = seq_len // ts
    j = jnp.minimum(step, n_tiles - 1) % n_seq_tiles
    sub = MIX_ROWS
    groups = range(ts // sub)
    rows = [slice(s * sub, (s + 1) * sub) for s in groups]

    @pl.when(step == 0)
    def _():
        tr = lax.broadcasted_iota(I32, (ts, ts), 0)
        tc = lax.broadcasted_iota(I32, (ts, ts), 1)
        earlier_ref[...] = jnp.where(tr < tc, 1.0, 0.0).astype(BF16)
        ext_ref[ts + 2 * POOL_HALO:, :] = jnp.zeros((POOL_HALO, D_POOL), F32)
        logit_ref[...] = jnp.zeros_like(logit_ref)

    prev_ok = jnp.where(j > 0, 1.0, 0.0)
    next_ok = jnp.where(j < n_seq_tiles - 1, 1.0, 0.0)
    ext_ref[0:POOL_HALO, :] = up_ref[0, 0] * prev_ok
    ext_ref[POOL_HALO:POOL_HALO + ts, :] = u_ref[0]
    ext_ref[POOL_HALO + ts:POOL_HALO + ts + POOL_HALO, :] = un_ref[0, 0] * next_ok
    pos = j * ts + lax.broadcasted_iota(I32, (ts, 1), 0)

    def window_sum(lanes, w):
        if w == 2:
            return ext_ref[POOL_HALO - 1:POOL_HALO - 1 + ts, lanes] + ext_ref[POOL_HALO:POOL_HALO + ts, lanes]
        first = SEG_ALIGN
        n = ts + SEG_ALIGN * (w.bit_length() - 2)
        e = ext_ref[first:first + n + SEG_ALIGN, lanes]
        p = e[0:n] + e[1:n + 1]
        step = 2
        while step < w // 2:
            n -= SEG_ALIGN
            p = p[0:n] + p[step:step + n]
            step *= 2
        off = POOL_HALO - w // 2 - first
        return p[off:off + ts] + p[off + w // 2:off + w // 2 + ts]

    pooled = []
    for gi, w in enumerate(POOL_WINDOWS):
        lanes = slice(gi * POOL_GROUP, (gi + 1) * POOL_GROUP)
        acc = window_sum(lanes, w)
        tok = u_ref[0, :, lanes]

        def clipped(edge):
            lo = jnp.clip(pos[edge] - w // 2, 0, seq_len - 1)
            hi = jnp.clip(pos[edge] - w // 2 + w - 1, 0, seq_len - 1)
            return acc[edge] / (hi - lo + 1).astype(F32) - tok[edge]

        inner = slice(POOL_HALO, ts - POOL_HALO)
        pooled.append(jnp.concatenate(
            [clipped(slice(0, POOL_HALO)), acc[inner] * (1.0 / w) - tok[inner], clipped(slice(ts - POOL_HALO, ts))],
            axis=0).astype(BF16))

    ys = []
    for s in groups:
        y_pool = [(_dot(pooled[gi][rows[s]], wp_ref[gi])
                   * ps_ref[:, gi * POOL_GROUP:(gi + 1) * POOL_GROUP]).astype(BF16)
                  for gi in range(len(POOL_WINDOWS))]
        ys.append(jnp.concatenate(y_pool + [yg_ref[0, rows[s], :]], axis=1))
    mixes = [_dot(ys[s], wo_ref[...]) for s in groups]

    score = jax.nn.sigmoid(logit_ref[...])
    sel = (score + br_ref[...]).reshape(N_GROUPS, EXPERTS_PER_GROUP, ts)
    local = lax.broadcasted_iota(I32, sel.shape, 1)
    big = EXPERTS_PER_GROUP
    m1 = jnp.max(sel, axis=1, keepdims=True)
    i1 = jnp.min(jnp.where(sel == m1, local, big), axis=1, keepdims=True)
    rest = jnp.where(local == i1, -jnp.inf, sel)
    m2 = jnp.max(rest, axis=1, keepdims=True)
    i2 = jnp.min(jnp.where(rest == m2, local, big), axis=1, keepdims=True)
    gscore = m1 + m2
    gid = lax.broadcasted_iota(I32, gscore.shape, 0)
    gbest = jnp.max(gscore, axis=0, keepdims=True)
    gsel = jnp.min(jnp.where(gscore == gbest, gid, N_GROUPS), axis=0, keepdims=True)
    in_group = gid == gsel
    pick0 = (in_group & (local == i1)).reshape(N_EXPERTS, ts)
    pick1 = (in_group & (local == i2)).reshape(N_EXPERTS, ts)
    s0 = jnp.sum(jnp.where(pick0, score, 0.0), axis=0, keepdims=True)
    s1 = jnp.sum(jnp.where(pick1, score, 0.0), axis=0, keepdims=True)
    rw_ref[0:1, :] = s0 / (s0 + s1)
    rw_ref[1:2, :] = s1 / (s0 + s1)

    onehot = jnp.where(pick0 | pick1, 1.0, 0.0).astype(BF16)
    er = lax.broadcasted_iota(I32, (N_EXPERTS, N_EXPERTS), 0)
    ec = lax.broadcasted_iota(I32, (N_EXPERTS, N_EXPERTS), 1)
    lower_exp = jnp.where(ec < er, 1.0, 0.0).astype(BF16)
    same_before = _dot(onehot, earlier_ref[...])
    cnt = jnp.sum(onehot.astype(F32), axis=1, keepdims=True)
    seg = jnp.ceil(cnt * (1.0 / SEG_ALIGN)) * SEG_ALIGN
    seg_lanes = jnp.broadcast_to(seg, (N_EXPERTS, LANE)).astype(BF16)
    first_slot = _dot(lower_exp, seg_lanes)[:, 0:1]
    slot = same_before + first_slot
    p0 = jnp.sum(jnp.where(pick0, slot, 0.0), axis=0, keepdims=True)
    p1 = jnp.sum(jnp.where(pick1, slot, 0.0), axis=0, keepdims=True)
    pos_ref[0:1, :] = p0.astype(I32)
    pos_ref[1:2, :] = p1.astype(I32)
    cnt_ref[0] = cnt

    logits = []
    for s in groups:
        x = x_ref[0, rows[s], :] + g1_ref[0] * mixes[s]
        xo_ref[0, rows[s], :] = x
        ms = jnp.mean(x * x, axis=-1, keepdims=True)
        h = x * lax.rsqrt(ms + EPS) * (1.0 + sc_ref[0]) + sh_ref[0]
        h_ref[0, rows[s], :] = h.astype(BF16)
        h_hi, h_lo = _split_bf16(h)
        logits.append(lax.dot_general(wrh_ref[...], h_hi, NT_DIMS, preferred_element_type=F32)
                      + lax.dot_general(wrh_ref[...], h_lo, NT_DIMS, preferred_element_type=F32)
                      + lax.dot_general(wrl_ref[...], h_hi, NT_DIMS, preferred_element_type=F32))

    logit_ref[...] = jnp.concatenate(logits, axis=1)


def _mix(u, y_gla, x, w_pool, pool_scale, w_out, g1, sc2, sh2, wr_hi, wr_lo, b_router):
    bn, sn, d = x.shape
    ts = SEQ_TILE
    nt = sn // ts
    hpt = ts // POOL_HALO
    n_halo = sn // POOL_HALO
    u_halo = u.reshape(bn, n_halo, POOL_HALO, D_POOL)
    n_tiles = bn * nt
    mixed = lambda s: jnp.minimum(s, n_tiles - 1)
    routed = lambda s: jnp.maximum(s - 1, 0)
    tok = lambda width: pl.BlockSpec((1, ts, width), lambda s: (mixed(s) // nt, mixed(s) % nt, 0))
    per_batch = pl.BlockSpec((1, 1, d), lambda s: (mixed(s) // nt, 0, 0))
    full = lambda a: pl.BlockSpec(a.shape, lambda s: (0,) * a.ndim)
    prev_halo = pl.BlockSpec((1, 1, POOL_HALO, D_POOL),
                             lambda s: (mixed(s) // nt, jnp.maximum(mixed(s) % nt * hpt - 1, 0), 0, 0))
    next_halo = pl.BlockSpec((1, 1, POOL_HALO, D_POOL),
                             lambda s: (mixed(s) // nt, jnp.minimum((mixed(s) % nt + 1) * hpt, n_halo - 1), 0, 0))
    lane_tok = lambda rows: pl.BlockSpec((rows, ts), lambda s: (0, routed(s)))
    t = bn * sn
    return pl.pallas_call(
        functools.partial(_mix_kernel, seq_len=sn, n_tiles=n_tiles),
        grid=(n_tiles + 1,),
        in_specs=[tok(D_POOL), prev_halo, next_halo, tok(D_GLA_V), tok(d), full(w_pool), full(pool_scale),
                  full(w_out), per_batch, per_batch, per_batch, full(wr_hi), full(wr_lo), full(b_router)],
        out_specs=[tok(d), tok(d), lane_tok(2), lane_tok(2),
                   pl.BlockSpec((1, N_EXPERTS, 1), lambda s: (routed(s), 0, 0))],
        out_shape=[jax.ShapeDtypeStruct((bn, sn, d), F32), jax.ShapeDtypeStruct((bn, sn, d), BF16),
                   jax.ShapeDtypeStruct((2, t), I32), jax.ShapeDtypeStruct((2, t), F32),
                   jax.ShapeDtypeStruct((bn * nt, N_EXPERTS, 1), F32)],
        scratch_shapes=[pltpu.VMEM((ts + 3 * POOL_HALO, D_POOL), F32), pltpu.VMEM((ts, ts), BF16),
                        pltpu.VMEM((N_EXPERTS, ts), F32)],
        compiler_params=pltpu.CompilerParams(
            dimension_semantics=("arbitrary",), vmem_limit_bytes=VMEM_LIMIT),
        name="mix_route",
    )(u, u_halo, u_halo, y_gla, x, w_pool, pool_scale, w_out, g1, sc2, sh2, wr_hi, wr_lo, b_router)


def _segment_copies(meta_ref, tile, make_copy, action):
    base = tile * META_PER_TILE
    for e in range(N_EXPERTS):
        n = pl.multiple_of(meta_ref[base + e], SEG_ALIGN)
        first_slot = pl.multiple_of(meta_ref[base + N_EXPERTS + e], SEG_ALIGN)
        first_row = pl.multiple_of(meta_ref[base + 2 * N_EXPERTS + e], SEG_ALIGN)

        @pl.when(n > 0)
        def _():
            action(make_copy(first_slot, first_row, n))


def _tile_slots_used(meta_ref, tile):
    last = tile * META_PER_TILE + N_EXPERTS - 1
    return pl.multiple_of(meta_ref[last] + meta_ref[last + N_EXPERTS], SEG_ALIGN)


def _start(copy):
    copy.start()


def _wait(copy):
    copy.wait()


def _pack_rows(x):
    lo = lax.bitcast_convert_type(x[:, :D_PACK], U32)
    hi = lax.bitcast_convert_type(x[:, D_PACK:], U32)
    return hi | (lo >> 16)


def _unpack_rows(u):
    lo = lax.bitcast_convert_type(u << 16, F32)
    hi = lax.bitcast_convert_type(u & jnp.uint32(0xFFFF0000), F32)
    return lo, hi


def _dispatch_kernel(meta_ref, tail_ref, h_ref, pos_ref, xs_ref, buf, zbuf, sems, sem):
    i = pl.program_id(0)
    ts = h_ref.shape[0]

    def tile_copy(tile):
        half = tile % 2

        def copy(first_slot, first_row, n):
            return pltpu.make_async_copy(buf.at[half, pl.ds(first_slot, n)], xs_ref.at[pl.ds(first_row, n)],
                                         sems.at[half])
        return copy

    def wait_tile(tile):
        tile_copy(tile)(0, 0, _tile_slots_used(meta_ref, tile)).wait()

    slot = lax.broadcasted_iota(I32, (TILE_SLOTS, ts), 0)
    perm = jnp.where((slot == pos_ref[0:1, :]) | (slot == pos_ref[1:2, :]), 1.0, 0.0).astype(BF16)
    rows = _dot(perm, h_ref[...])

    @pl.when(i > 1)
    def _():
        wait_tile(i - 2)

    buf[i % 2] = _pack_rows(rows)
    _segment_copies(meta_ref, i, tile_copy(i), _start)

    @pl.when(i == pl.num_programs(0) - 1)
    def _():
        @pl.when(i > 0)
        def _():
            wait_tile(i - 1)

        wait_tile(i)
        zbuf[...] = jnp.zeros_like(zbuf)

        def tail_copies(action):
            for e in range(N_EXPERTS):
                n = pl.multiple_of(tail_ref[e], SEG_ALIGN)
                first_row = pl.multiple_of(tail_ref[N_EXPERTS + e], SEG_ALIGN)

                @pl.when(n > 0)
                def _():
                    action(pltpu.make_async_copy(zbuf.at[pl.ds(0, n)], xs_ref.at[pl.ds(first_row, n)], sem))

        tail_copies(_start)
        tail_copies(_wait)

        def zero_block(blk, carry):
            first_row = pl.multiple_of(blk * MOE_BLOCK, MOE_BLOCK)
            fill = pltpu.make_async_copy(zbuf, xs_ref.at[pl.ds(first_row, MOE_BLOCK)], sem)
            fill.start()
            fill.wait()
            return carry

        lax.fori_loop(tail_ref[2 * N_EXPERTS], xs_ref.shape[0] // MOE_BLOCK, zero_block, 0)


def _dispatch(meta, tails, h, pos, n_rows_sorted):
    t, d = h.shape
    ts = SEQ_TILE
    return pl.pallas_call(
        _dispatch_kernel,
        grid_spec=pltpu.PrefetchScalarGridSpec(
            num_scalar_prefetch=2,
            grid=(t // ts,),
            in_specs=[pl.BlockSpec((ts, d), lambda i, m, tl: (i, 0)),
                      pl.BlockSpec((2, ts), lambda i, m, tl: (0, i))],
            out_specs=pl.BlockSpec(memory_space=pl.ANY),
            scratch_shapes=[pltpu.VMEM((2, TILE_SLOTS, D_PACK), U32), pltpu.VMEM((MOE_BLOCK, D_PACK), U32),
                            pltpu.SemaphoreType.DMA((2,)), pltpu.SemaphoreType.DMA(())],
        ),
        out_shape=jax.ShapeDtypeStruct((n_rows_sorted, D_PACK), U32),
        compiler_params=pltpu.CompilerParams(
            dimension_semantics=("arbitrary",), vmem_limit_bytes=VMEM_LIMIT),
        name="dispatch",
    )(meta, tails, h, pos)


def _ffn_kernel(blk_e_ref, n_used_ref, xs_ref, w1_ref, w3_ref, w2_ref, ys_ref):
    del blk_e_ref

    @pl.when(pl.program_id(0) < n_used_ref[0])
    def _():
        lo, hi = _unpack_rows(xs_ref[...])
        xb = jnp.concatenate([lo.astype(BF16), hi.astype(BF16)], axis=1)
        a = _dot(xb, w1_ref[0, 0].astype(BF16))
        hid = (a * jax.nn.sigmoid(a)) * _dot(xb, w3_ref[0, 0].astype(BF16))
        y = _dot(hid.astype(BF16), w2_ref[0, 0].astype(BF16))
        ys_ref[...] = _pack_rows(y.astype(BF16).astype(F32))


def _ffn(layer, blk_e, n_used, xs, w1, w3, w2):
    rows = xs.shape[0]
    n_blocks = rows // MOE_BLOCK
    last = lambda i, n_used_ref: jnp.minimum(i, n_used_ref[0] - 1)
    row_spec = pl.BlockSpec((MOE_BLOCK, D_PACK), lambda i, be, nu: (last(i, nu), 0))
    w_spec = lambda a: pl.BlockSpec((1, 1) + a.shape[2:], lambda i, be, nu: (layer, be[last(i, nu)], 0, 0))
    return pl.pallas_call(
        _ffn_kernel,
        grid_spec=pltpu.PrefetchScalarGridSpec(
            num_scalar_prefetch=2,
            grid=(n_blocks,),
            in_specs=[row_spec, w_spec(w1), w_spec(w3), w_spec(w2)],
            out_specs=row_spec,
        ),
        out_shape=jax.ShapeDtypeStruct(xs.shape, U32),
        input_output_aliases={2: 0},
        compiler_params=pltpu.CompilerParams(
            dimension_semantics=("arbitrary",), vmem_limit_bytes=VMEM_LIMIT),
        name="expert_ffn",
    )(blk_e, n_used, xs, w1, w3, w2)


def _combine_kernel(meta_ref, ys_ref, x_ref, pos_ref, rw_ref, g2_ref, fg_ref, o_ref, buf, zbuf, sem,
                    *, final_norm):
    i = pl.program_id(0)
    ts = x_ref.shape[0]

    def fetch(tile):
        half = tile % 2

        def copy(first_slot, first_row, n):
            return pltpu.make_async_copy(ys_ref.at[pl.ds(first_row, n)],
                                         buf.at[half, pl.ds(first_slot, n)], sem.at[half])

        _segment_copies(meta_ref, tile, copy, _start)
        end = _tile_slots_used(meta_ref, tile)
        n_free = pl.multiple_of(TILE_SLOTS - end, SEG_ALIGN)

        @pl.when(n_free > 0)
        def _():
            pltpu.make_async_copy(zbuf.at[pl.ds(0, n_free)], buf.at[half, pl.ds(end, n_free)],
                                  sem.at[half]).start()

    @pl.when(i == 0)
    def _():
        zbuf[...] = jnp.zeros_like(zbuf)
        fetch(i)

    @pl.when(i + 1 < pl.num_programs(0))
    def _():
        fetch(i + 1)

    pltpu.make_async_copy(ys_ref.at[pl.ds(0, TILE_SLOTS)], buf.at[i % 2], sem.at[i % 2]).wait()
    lo, hi = _unpack_rows(buf[i % 2])
    rows = jnp.concatenate([lo.astype(BF16), hi.astype(BF16)], axis=1)
    slot = lax.broadcasted_iota(I32, (TILE_SLOTS, ts), 0)
    weights = (jnp.where(slot == pos_ref[0:1, :], rw_ref[0:1, :], 0.0)
               + jnp.where(slot == pos_ref[1:2, :], rw_ref[1:2, :], 0.0)).astype(BF16)
    x = x_ref[...] + g2_ref[0] * lax.dot_general(weights, rows, TN_DIMS, preferred_element_type=F32)
    if final_norm:
        ms = jnp.mean(x * x, axis=-1, keepdims=True)
        x = x * lax.rsqrt(ms + EPS) * fg_ref[...]
    o_ref[...] = x


def _combine(meta, ys, x, pos, route_w, g2, final_g, seq_len, final_norm):
    t, d = x.shape
    ts = SEQ_TILE
    tiles_per_seq = seq_len // ts
    lane_tok = pl.BlockSpec((2, ts), lambda i, m: (0, i))
    return pl.pallas_call(
        functools.partial(_combine_kernel, final_norm=final_norm),
        grid_spec=pltpu.PrefetchScalarGridSpec(
            num_scalar_prefetch=1,
            grid=(t // ts,),
            in_specs=[pl.BlockSpec(memory_space=pl.ANY),
                      pl.BlockSpec((ts, d), lambda i, m: (i, 0)),
                      lane_tok, lane_tok,
                      pl.BlockSpec((1, 1, d), lambda i, m: (i // tiles_per_seq, 0, 0)),
                      pl.BlockSpec((1, d), lambda i, m: (0, 0))],
            out_specs=pl.BlockSpec((ts, d), lambda i, m: (i, 0)),
            scratch_shapes=[pltpu.VMEM((2, TILE_SLOTS, D_PACK), U32),
                            pltpu.VMEM((TILE_SLOTS - 2 * ts, D_PACK), U32), pltpu.SemaphoreType.DMA((2,))],
        ),
        out_shape=jax.ShapeDtypeStruct((t, d), F32),
        compiler_params=pltpu.CompilerParams(
            dimension_semantics=("arbitrary",), vmem_limit_bytes=VMEM_LIMIT),
        name="combine",
    )(meta, ys, x, pos, route_w, g2, final_g.reshape(1, d))


def _dispatch_plan(tile_counts, n_assign):
    n_tiles = tile_counts.shape[0]
    seg = (tile_counts.astype(I32) + SEG_ALIGN - 1) // SEG_ALIGN * SEG_ALIGN
    totals = jnp.sum(seg, axis=0)
    padded = (totals + MOE_BLOCK - 1) // MOE_BLOCK * MOE_BLOCK
    pad_end = jnp.cumsum(padded)
    pad_start = pad_end - padded
    first_row = pad_start[None, :] + jnp.cumsum(seg, axis=0) - seg
    first_slot = jnp.cumsum(seg, axis=1) - seg
    meta = jnp.stack([seg, first_slot, first_row], axis=1).reshape(-1)
    max_rows = n_assign + n_tiles * N_EXPERTS * (SEG_ALIGN - 1)
    n_blocks = -(-max_rows // MOE_BLOCK) + N_EXPERTS
    blk_start = jnp.arange(n_blocks, dtype=I32) * MOE_BLOCK
    blk_e = jnp.minimum(jnp.sum((pad_end[None, :] <= blk_start[:, None]).astype(I32), axis=1), N_EXPERTS - 1)
    n_used = (pad_end[-1] // MOE_BLOCK).astype(I32).reshape(1)
    tails = jnp.concatenate([padded - totals, pad_start + totals, n_used])
    return meta.astype(I32), tails.astype(I32), blk_e.astype(I32), n_used, n_blocks * MOE_BLOCK


def kernel(x, c, w_mod, b_mod, w_in, w_pool, pool_scale, w_gk_up, b_gk, gla_norm_g, w_out,
           w_router, b_router, w1, w3, w2, final_g):
    bn, sn, d = x.shape
    t = bn * sn
    mod = _modulation(c, w_mod, b_mod).reshape(DEPTH, bn, 6, 1, d)

    w_main = w_in[:, :, :D_MAIN].astype(BF16)
    w_r = jnp.swapaxes(w_in[:, :, D_MAIN:], 1, 2).astype(BF16)
    zero_rank = jnp.zeros_like(w_gk_up[:, 0])
    w_gk = jnp.stack([jnp.concatenate([w_gk_up[:, 0], zero_rank], axis=1),
                      jnp.concatenate([zero_rank, w_gk_up[:, 1]], axis=1)], axis=1).astype(BF16)
    b_gk3 = b_gk.reshape(DEPTH, 2, 1, D_QK)
    w_pool_b = w_pool.astype(BF16)
    w_out_b = w_out.astype(BF16)
    wr_t = w_router.T
    wr_hi = wr_t.astype(BF16)
    wr_lo = (wr_t - wr_hi.astype(F32)).astype(BF16)
    br = b_router.reshape(N_EXPERTS, 1)

    for l in range(DEPTH):
        sh1, sc1, g1, sh2, sc2, g2 = (mod[l, :, i] for i in range(6))
        u, g, v, qf, kf, df, qb, kb, db, decf, decb = _inproj(
            x, sc1, sh1, w_main[l], w_r[l], w_gk[l], b_gk3[l])
        y_gla = _gla(qf, kf, df, qb, kb, db, v, decf, decb, g, gla_norm_g[l])
        x, h, pos, route_w, tile_counts = _mix(
            u, y_gla, x, w_pool_b[l], pool_scale[l].reshape(1, D_POOL), w_out_b[l], g1, sc2, sh2,
            wr_hi, wr_lo, br)
        meta, tails, blk_e, n_used, n_rows_sorted = _dispatch_plan(tile_counts[:, :, 0], 2 * t)
        xs = _dispatch(meta, tails, h.reshape(t, d), pos, n_rows_sorted)
        ys = _ffn(l, blk_e, n_used, xs, w1, w3, w2)
        x = _combine(meta, ys, x.reshape(t, d), pos, route_w, g2, final_g, sn, l == DEPTH - 1)
        x = x.reshape(bn, sn, d)
    return x
```

```python
import functools

import jax
import jax.numpy as jnp
from jax import lax
from jax.experimental import pallas as pl
from jax.experimental.pallas import tpu as pltpu

F32 = jnp.float32
BF16 = jnp.bfloat16
I32 = jnp.int32
U32 = jnp.uint32

D_MODEL = 1024
DEPTH = 2
D_POOL = 512
POOL_WINDOWS = (2, 4, 8, 16)
POOL_GROUP = 128
POOL_HALO = 16
D_GLA_V = 512
GLA_HEADS = 4
GLA_DK = 64
GLA_DV = 128
D_QK = GLA_HEADS * GLA_DK
GLA_RANK = 16
GATE_NORM = 16.0
GLA_CHUNK = 64
GLA_GROUP = 4
GLA_OUT_ROWS = 256
D_MAIN = D_POOL + 2 * D_QK + 2 * D_GLA_V
N_EXPERTS = 32
N_GROUPS = 4
EXPERTS_PER_GROUP = 8
D_FF = 512
EPS = 1e-6

LANE = 128
SEG_ALIGN = 8
MXU_DIM = 256
VMEM_LIMIT = 56 * 1024 * 1024

SEQ_TILE = 512
MIX_ROWS = 128
INPROJ_TILE = 1024
CUMSUM_GROUP = 256
MOD_COLS = 1536
MOE_BLOCK = 1024
TILE_SLOTS = -(-(2 * SEQ_TILE + N_EXPERTS * (SEG_ALIGN - 1)) // MXU_DIM) * MXU_DIM
D_PACK = D_MODEL // 2
META_PER_TILE = 3 * N_EXPERTS
assert 2 * SEQ_TILE // SEG_ALIGN <= 256

NT_DIMS = (((1,), (1,)), ((), ()))
TN_DIMS = (((0,), (0,)), ((), ()))


def _dot(a, b):
    return jnp.dot(a, b, preferred_element_type=F32)


def _split_bf16(a):
    hi = a.astype(BF16)
    lo = (a - hi.astype(F32)).astype(BF16)
    return hi, lo


def _mod_kernel(c_ref, w_ref, b_ref, o_ref):
    c = c_ref[...]
    c_act = (c * jax.nn.sigmoid(c)).astype(BF16)
    o_ref[0] = _dot(c_act, w_ref[0].astype(BF16)) + b_ref[0]


def _modulation(c, w_mod, b_mod):
    n_layers, d, n_out = w_mod.shape
    bn = c.shape[0]
    tn = MOD_COLS
    return pl.pallas_call(
        _mod_kernel,
        grid=(n_layers, n_out // tn),
        in_specs=[
            pl.BlockSpec((bn, d), lambda l, j: (0, 0)),
            pl.BlockSpec((1, d, tn), lambda l, j: (l, 0, j)),
            pl.BlockSpec((1, 1, tn), lambda l, j: (l, 0, j)),
        ],
        out_specs=pl.BlockSpec((1, bn, tn), lambda l, j: (l, 0, j)),
        out_shape=jax.ShapeDtypeStruct((n_layers, bn, n_out), F32),
        compiler_params=pltpu.CompilerParams(
            dimension_semantics=("arbitrary", "arbitrary"), vmem_limit_bytes=VMEM_LIMIT),
        name="modulation",
    )(c, w_mod, b_mod.reshape(n_layers, 1, n_out))


def _log_sigmoid(x):
    return jnp.minimum(x, 0.0) - jnp.log1p(jnp.exp(-jnp.abs(x)))


def _inproj_kernel(x_ref, sc_ref, sh_ref, wmain_ref, wr_ref, wgk_ref, bgk_ref,
                   u_ref, g_ref, v_ref, qf_ref, kf_ref, df_ref, qb_ref, kb_ref, db_ref,
                   decf_ref, decb_ref):
    tile = x_ref.shape[1]
    sub = CUMSUM_GROUP
    sub_chunks = sub // GLA_CHUNK
    row = lax.broadcasted_iota(I32, (sub, sub), 0)
    col = lax.broadcasted_iota(I32, (sub, sub), 1)
    same_chunk = (row // GLA_CHUNK) == (col // GLA_CHUNK)
    prefix = jnp.where(same_chunk & (col <= row), 1.0, 0.0).astype(BF16)
    suffix = jnp.where(same_chunk & (col >= row), 1.0, 0.0).astype(BF16)

    groups = range(tile // sub)
    rows = [slice(s * sub, (s + 1) * sub) for s in groups]
    chunks = [slice(s * sub_chunks, (s + 1) * sub_chunks) for s in groups]
    dirs = ((0, prefix, GLA_CHUNK - 1, qf_ref, kf_ref, df_ref, decf_ref),
            (1, suffix, 0, qb_ref, kb_ref, db_ref, decb_ref))

    zs, rs = [], []
    for s in groups:
        x = x_ref[0, rows[s], :]
        ms = jnp.mean(x * x, axis=-1, keepdims=True)
        h = x * lax.rsqrt(ms + EPS) * (1.0 + sc_ref[0]) + sh_ref[0]
        hb = h.astype(BF16)
        zs.append(_dot(hb, wmain_ref[...]))
        rs.append(lax.dot_general(wr_ref[...], hb, NT_DIMS, preferred_element_type=F32).astype(BF16))

    logas = [[_log_sigmoid(lax.dot_general(rs[s], wgk_ref[idx], TN_DIMS, preferred_element_type=F32)
                           + bgk_ref[idx]) / GATE_NORM for idx in range(2)]
             for s in groups]

    bs = []
    for s in groups:
        per_dir = []
        for idx, tri, *_ in dirs:
            hi, lo = _split_bf16(logas[s][idx])
            per_dir.append(_dot(tri, hi) + _dot(tri, lo))
        bs.append(per_dir)

    for s in groups:
        z = zs[s]
        u_ref[0, rows[s], :] = z[:, 0:D_POOL]
        q = z[:, D_POOL:D_POOL + D_QK] * (GLA_DK ** -0.5)
        k = z[:, D_POOL + D_QK:D_POOL + 2 * D_QK]
        v_ref[0, rows[s], :] = z[:, D_POOL + 2 * D_QK:D_POOL + 2 * D_QK + D_GLA_V].astype(BF16)
        g_ref[0, rows[s], :] = z[:, D_POOL + 2 * D_QK + D_GLA_V:D_MAIN]
        for idx, _, last_row, q_out, k_out, d_out, dec_out in dirs:
            b = bs[s][idx]
            b3 = b.reshape(sub_chunks, GLA_CHUNK, D_QK)
            total = b3[:, last_row:last_row + 1, :]
            q_out[0, rows[s], :] = (q * jnp.exp(b)).astype(BF16)
            k_out[0, rows[s], :] = (k * jnp.exp(-b)).astype(BF16)
            d_out[0, rows[s], :] = (k * jnp.exp(total - b3).reshape(sub, D_QK)).astype(BF16)
            dec_out[0, chunks[s], :] = jnp.exp(total).reshape(sub_chunks, D_QK)


def _inproj(x, sc, sh, w_main, w_r, w_gk, b_gk):
    bn, sn, d = x.shape
    ts = INPROJ_TILE
    n_chunks = ts // GLA_CHUNK
    tok = lambda width: pl.BlockSpec((1, ts, width), lambda b, j: (b, j, 0))
    per_batch = pl.BlockSpec((1, 1, d), lambda b, j: (b, 0, 0))
    full = lambda a: pl.BlockSpec(a.shape, lambda b, j: (0,) * a.ndim)
    dec_spec = pl.BlockSpec((1, n_chunks, D_QK), lambda b, j: (b, j, 0))
    tok_shape = lambda width, dt: jax.ShapeDtypeStruct((bn, sn, width), dt)
    dec_shape = jax.ShapeDtypeStruct((bn, sn // GLA_CHUNK, D_QK), F32)
    return pl.pallas_call(
        _inproj_kernel,
        grid=(bn, sn // ts),
        in_specs=[tok(d), per_batch, per_batch, full(w_main), full(w_r), full(w_gk), full(b_gk)],
        out_specs=[tok(D_POOL), tok(D_GLA_V), tok(D_GLA_V)] + [tok(D_QK)] * 6 + [dec_spec, dec_spec],
        out_shape=[tok_shape(D_POOL, F32), tok_shape(D_GLA_V, F32), tok_shape(D_GLA_V, BF16)]
        + [tok_shape(D_QK, BF16)] * 6 + [dec_shape, dec_shape],
        compiler_params=pltpu.CompilerParams(
            dimension_semantics=("arbitrary", "arbitrary"), vmem_limit_bytes=VMEM_LIMIT),
        name="inproj",
    )(x, sc, sh, w_main, w_r, w_gk, b_gk)


def _gla_kernel(qf_ref, kf_ref, df_ref, qb_ref, kb_ref, db_ref, v_ref, decf_ref, decb_ref,
                g_ref, ng_ref, y_ref, of_acc, ob_acc, state_f, state_b):
    sn = v_ref.shape[1]
    n_chunks = sn // GLA_CHUNK
    lane_head = lax.broadcasted_iota(I32, (GLA_CHUNK, D_QK), 1) // GLA_DK
    head_masks = [lane_head == hd for hd in range(GLA_HEADS)]
    state_blocks = (lax.broadcasted_iota(I32, (D_GLA_V, D_QK), 0) // GLA_DV
                    == lax.broadcasted_iota(I32, (D_GLA_V, D_QK), 1) // GLA_DK)
    def group_masks(own):
        masks = []
        for m in range(GLA_GROUP):
            r = lax.broadcasted_iota(I32, (GLA_CHUNK, GLA_CHUNK * (m + 1)), 0)
            c = lax.broadcasted_iota(I32, (GLA_CHUNK, GLA_CHUNK * (m + 1)), 1) - GLA_CHUNK * m
            masks.append((c < 0) | own(r, c))
        return masks

    causal = group_masks(lambda r, c: c <= r)
    anti = group_masks(lambda r, c: c > r)

    def scaled(a, factor):
        return a if factor is None else (a.astype(F32) * factor).astype(BF16)

    def product(factors):
        out = None
        for f in factors:
            out = f if out is None else out * f
        return out

    def group_out(first_chunk, order, q_ref, k_ref, d_ref, dec_ref, state, masks, o_acc):
        base = pl.multiple_of(first_chunk * GLA_CHUNK, GLA_GROUP * GLA_CHUNK)
        slab = pl.ds(base, GLA_GROUP * GLA_CHUNK)
        part = lambda a, c: a[c * GLA_CHUNK:(c + 1) * GLA_CHUNK]
        q_all, k_all, d_all, v_all = q_ref[0, slab, :], k_ref[0, slab, :], d_ref[0, slab, :], v_ref[0, slab, :]
        qe = [part(q_all, c) for c in order]
        ke = [part(k_all, c) for c in order]
        kd = [part(d_all, c) for c in order]
        vv = [part(v_all, c) for c in order]
        dec = [dec_ref[0, pl.ds(first_chunk + c, 1), :] for c in order]

        st = state[...]
        st_heads = jnp.where(state_blocks, st, 0.0).astype(BF16)
        q_in = jnp.concatenate([scaled(qe[m], product(dec[:m])) for m in range(GLA_GROUP)], axis=0)
        inter = lax.dot_general(q_in, st_heads, NT_DIMS, preferred_element_type=F32)
        k_out = jnp.concatenate([scaled(kd[c], product(dec[c + 1:])) for c in range(GLA_GROUP)], axis=0)
        upd = lax.dot_general(jnp.concatenate(vv, axis=0), k_out, TN_DIMS, preferred_element_type=F32)
        state[...] = st * product(dec) + upd

        scores = []
        for m in range(GLA_GROUP):
            keys = jnp.concatenate([scaled(kd[c], product(dec[c + 1:m])) for c in range(m)] + [ke[m]], axis=0)
            q_heads = jnp.concatenate([jnp.where(hm, qe[m], jnp.zeros_like(qe[m])) for hm in head_masks], axis=0)
            scores.append(lax.dot_general(q_heads, keys, NT_DIMS, preferred_element_type=F32))

        def finish():
            for m in range(GLA_GROUP):
                vals = jnp.concatenate(vv[:m + 1], axis=0)
                outs = []
                for hd in range(GLA_HEADS):
                    att = jnp.where(masks[m], scores[m][hd * GLA_CHUNK:(hd + 1) * GLA_CHUNK], 0.0).astype(BF16)
                    outs.append(_dot(att, vals[:, hd * GLA_DV:(hd + 1) * GLA_DV]))
                rows = pl.ds(base + order[m] * GLA_CHUNK, GLA_CHUNK)
                o_acc[rows, :] = jnp.concatenate(outs, axis=1) + part(inter, m)

        return finish

    step = pl.program_id(0)
    cur = step % 2
    prev = 1 - cur

    @pl.when(step == 0)
    def _():
        of_acc[...] = jnp.zeros_like(of_acc)
        ob_acc[...] = jnp.zeros_like(ob_acc)

    state_f[...] = jnp.zeros_like(state_f)
    state_b[...] = jnp.zeros_like(state_b)
    n_groups = n_chunks // GLA_GROUP
    ascending = list(range(GLA_GROUP))
    norm_g = ng_ref[...]
    group_rows = GLA_GROUP * GLA_CHUNK

    def scan_body(i, carry):
        finish_f = group_out(i * GLA_GROUP, ascending, qf_ref, kf_ref, df_ref, decf_ref, state_f, causal,
                             of_acc.at[cur])
        finish_b = group_out((n_groups - 1 - i) * GLA_GROUP, ascending[::-1], qb_ref, kb_ref, db_ref, decb_ref,
                             state_b, anti, ob_acc.at[cur])
        rows = pl.ds(pl.multiple_of(i * group_rows, group_rows), group_rows)
        o = of_acc[prev, rows, :] + ob_acc[prev, rows, :]
        gate = g_ref[0, rows, :]
        gate = gate * jax.nn.sigmoid(gate)
        outs = []
        for hd in range(GLA_HEADS):
            oh = o[:, hd * GLA_DV:(hd + 1) * GLA_DV]
            oh = oh * lax.rsqrt(jnp.mean(oh * oh, axis=-1, keepdims=True) + EPS) * norm_g
            outs.append(oh)
        y_ref[0, rows, :] = (jnp.concatenate(outs, axis=1) * gate).astype(BF16)
        finish_f()
        finish_b()
        return carry

    lax.fori_loop(0, n_groups, scan_body, 0)


def _gla(qf, kf, df, qb, kb, db, v, decf, decb, g, norm_g):
    bn, sn, _ = v.shape
    scanned = lambda s: jnp.minimum(s, bn - 1)
    finished = lambda s: jnp.maximum(s - 1, 0)
    seq = lambda a: pl.BlockSpec((1,) + a.shape[1:], lambda s: (scanned(s), 0, 0))
    args = (qf, kf, df, qb, kb, db, v, decf, decb)
    return pl.pallas_call(
        _gla_kernel,
        grid=(bn + 1,),
        in_specs=[seq(a) for a in args]
        + [pl.BlockSpec((1, sn, D_GLA_V), lambda s: (finished(s), 0, 0)),
           pl.BlockSpec((1, GLA_DV), lambda s: (0, 0))],
        out_specs=pl.BlockSpec((1, sn, D_GLA_V), lambda s: (finished(s), 0, 0)),
        out_shape=jax.ShapeDtypeStruct((bn, sn, D_GLA_V), BF16),
        scratch_shapes=[pltpu.VMEM((2, sn, D_GLA_V), F32), pltpu.VMEM((2, sn, D_GLA_V), F32),
                        pltpu.VMEM((D_GLA_V, D_QK), F32), pltpu.VMEM((D_GLA_V, D_QK), F32)],
        compiler_params=pltpu.CompilerParams(
            dimension_semantics=("arbitrary",), vmem_limit_bytes=VMEM_LIMIT),
        name="gla",
    )(*args, g, norm_g.reshape(1, GLA_DV))


def _mix_kernel(u_ref, up_ref, un_ref, yg_ref, x_ref, wp_ref, ps_ref, wo_ref, g1_ref, sc_ref, sh_ref,
                wrh_ref, wrl_ref, br_ref,
                xo_ref, h_ref, pos_ref, rw_ref, cnt_ref, ext_ref, earlier_ref, logit_ref, *, seq_len, n_tiles):
    step = pl.program_id(0)
    ts = u_ref.shape[1]
    n_seq_tiles = seq_len // ts
    j = jnp.minimum(step, n_tiles - 1) % n_seq_tiles
    sub = MIX_ROWS
    groups = range(ts // sub)
    rows = [slice(s * sub, (s + 1) * sub) for s in groups]

    @pl.when(step == 0)
    def _():
        tr = lax.broadcasted_iota(I32, (ts, ts), 0)
        tc = lax.broadcasted_iota(I32, (ts, ts), 1)
        earlier_ref[...] = jnp.where(tr < tc, 1.0, 0.0).astype(BF16)
        ext_ref[ts + 2 * POOL_HALO:, :] = jnp.zeros((POOL_HALO, D_POOL), F32)
        logit_ref[...] = jnp.zeros_like(logit_ref)

    prev_ok = jnp.where(j > 0, 1.0, 0.0)
    next_ok = jnp.where(j < n_seq_tiles - 1, 1.0, 0.0)
    ext_ref[0:POOL_HALO, :] = up_ref[0, 0] * prev_ok
    ext_ref[POOL_HALO:POOL_HALO + ts, :] = u_ref[0]
    ext_ref[POOL_HALO + ts:POOL_HALO + ts + POOL_HALO, :] = un_ref[0, 0] * next_ok
    pos = j * ts + lax.broadcasted_iota(I32, (ts, 1), 0)

    def window_sum(lanes, w):
        if w == 2:
            return ext_ref[POOL_HALO - 1:POOL_HALO - 1 + ts, lanes] + ext_ref[POOL_HALO:POOL_HALO + ts, lanes]
        first = SEG_ALIGN
        n = ts + SEG_ALIGN * (w.bit_length() - 2)
        e = ext_ref[first:first + n + SEG_ALIGN, lanes]
        p = e[0:n] + e[1:n + 1]
        step = 2
        while step < w // 2:
            n -= SEG_ALIGN
            p = p[0:n] + p[step:step + n]
            step *= 2
        off = POOL_HALO - w // 2 - first
        return p[off:off + ts] + p[off + w // 2:off + w // 2 + ts]

    pooled = []
    for gi, w in enumerate(POOL_WINDOWS):
        lanes = slice(gi * POOL_GROUP, (gi + 1) * POOL_GROUP)
        acc = window_sum(lanes, w)
        tok = u_ref[0, :, lanes]

        def clipped(edge):
            lo = jnp.clip(pos[edge] - w // 2, 0, seq_len - 1)
            hi = jnp.clip(pos[edge] - w // 2 + w - 1, 0, seq_len - 1)
            return acc[edge] / (hi - lo + 1).astype(F32) - tok[edge]

        inner = slice(POOL_HALO, ts - POOL_HALO)
        pooled.append(jnp.concatenate(
            [clipped(slice(0, POOL_HALO)), acc[inner] * (1.0 / w) - tok[inner], clipped(slice(ts - POOL_HALO, ts))],
            axis=0).astype(BF16))

    ys = []
    for s in groups:
        y_pool = [(_dot(pooled[gi][rows[s]], wp_ref[gi])
                   * ps_ref[:, gi * POOL_GROUP:(gi + 1) * POOL_GROUP]).astype(BF16)
                  for gi in range(len(POOL_WINDOWS))]
        ys.append(jnp.concatenate(y_pool + [yg_ref[0, rows[s], :]], axis=1))
    mixes = [_dot(ys[s], wo_ref[...]) for s in groups]

    score = jax.nn.sigmoid(logit_ref[...])
    sel = (score + br_ref[...]).reshape(N_GROUPS, EXPERTS_PER_GROUP, ts)
    local = lax.broadcasted_iota(I32, sel.shape, 1)
    big = EXPERTS_PER_GROUP
    m1 = jnp.max(sel, axis=1, keepdims=True)
    i1 = jnp.min(jnp.where(sel == m1, local, big), axis=1, keepdims=True)
    rest = jnp.where(local == i1, -jnp.inf, sel)
    m2 = jnp.max(rest, axis=1, keepdims=True)
    i2 = jnp.min(jnp.where(rest == m2, local, big), axis=1, keepdims=True)
    gscore = m1 + m2
    gid = lax.broadcasted_iota(I32, gscore.shape, 0)
    gbest = jnp.max(gscore, axis=0, keepdims=True)
    gsel = jnp.min(jnp.where(gscore == gbest, gid, N_GROUPS), axis=0, keepdims=True)
    in_group = gid == gsel
    pick0 = (in_group & (local == i1)).reshape(N_EXPERTS, ts)
    pick1 = (in_group & (local == i2)).reshape(N_EXPERTS, ts)
    s0 = jnp.sum(jnp.where(pick0, score, 0.0), axis=0, keepdims=True)
    s1 = jnp.sum(jnp.where(pick1, score, 0.0), axis=0, keepdims=True)
    rw_ref[0:1, :] = s0 / (s0 + s1)
    rw_ref[1:2, :] = s1 / (s0 + s1)

    onehot = jnp.where(pick0 | pick1, 1.0, 0.0).astype(BF16)
    er = lax.broadcasted_iota(I32, (N_EXPERTS, N_EXPERTS), 0)
    ec = lax.broadcasted_iota(I32, (N_EXPERTS, N_EXPERTS), 1)
    lower_exp = jnp.where(ec < er, 1.0, 0.0).astype(BF16)
    same_before = _dot(onehot, earlier_ref[...])
    cnt = jnp.sum(onehot.astype(F32), axis=1, keepdims=True)
    seg = jnp.ceil(cnt * (1.0 / SEG_ALIGN)) * SEG_ALIGN
    seg_lanes = jnp.broadcast_to(seg, (N_EXPERTS, LANE)).astype(BF16)
    first_slot = _dot(lower_exp, seg_lanes)[:, 0:1]
    slot = same_before + first_slot
    p0 = jnp.sum(jnp.where(pick0, slot, 0.0), axis=0, keepdims=True)
    p1 = jnp.sum(jnp.where(pick1, slot, 0.0), axis=0, keepdims=True)
    pos_ref[0:1, :] = p0.astype(I32)
    pos_ref[1:2, :] = p1.astype(I32)
    cnt_ref[0] = cnt

    logits = []
    for s in groups:
        x = x_ref[0, rows[s], :] + g1_ref[0] * mixes[s]
        xo_ref[0, rows[s], :] = x
        ms = jnp.mean(x * x, axis=-1, keepdims=True)
        h = x * lax.rsqrt(ms + EPS) * (1.0 + sc_ref[0]) + sh_ref[0]
        h_ref[0, rows[s], :] = h.astype(BF16)
        h_hi, h_lo = _split_bf16(h)
        logits.append(lax.dot_general(wrh_ref[...], h_hi, NT_DIMS, preferred_element_type=F32)
                      + lax.dot_general(wrh_ref[...], h_lo, NT_DIMS, preferred_element_type=F32)
                      + lax.dot_general(wrl_ref[...], h_hi, NT_DIMS, preferred_element_type=F32))

    logit_ref[...] = jnp.concatenate(logits, axis=1)


def _mix(u, y_gla, x, w_pool, pool_scale, w_out, g1, sc2, sh2, wr_hi, wr_lo, b_router):
    bn, sn, d = x.shape
    ts = SEQ_TILE
    nt = sn // ts
    hpt = ts // POOL_HALO
    n_halo = sn // POOL_HALO
    u_halo = u.reshape(bn, n_halo, POOL_HALO, D_POOL)
    n_tiles = bn * nt
    mixed = lambda s: jnp.minimum(s, n_tiles - 1)
    routed = lambda s: jnp.maximum(s - 1, 0)
    tok = lambda width: pl.BlockSpec((1, ts, width), lambda s: (mixed(s) // nt, mixed(s) % nt, 0))
    per_batch = pl.BlockSpec((1, 1, d), lambda s: (mixed(s) // nt, 0, 0))
    full = lambda a: pl.BlockSpec(a.shape, lambda s: (0,) * a.ndim)
    prev_halo = pl.BlockSpec((1, 1, POOL_HALO, D_POOL),
                             lambda s: (mixed(s) // nt, jnp.maximum(mixed(s) % nt * hpt - 1, 0), 0, 0))
    next_halo = pl.BlockSpec((1, 1, POOL_HALO, D_POOL),
                             lambda s: (mixed(s) // nt, jnp.minimum((mixed(s) % nt + 1) * hpt, n_halo - 1), 0, 0))
    lane_tok = lambda rows: pl.BlockSpec((rows, ts), lambda s: (0, routed(s)))
    t = bn * sn
    return pl.pallas_call(
        functools.partial(_mix_kernel, seq_len=sn, n_tiles=n_tiles),
        grid=(n_tiles + 1,),
        in_specs=[tok(D_POOL), prev_halo, next_halo, tok(D_GLA_V), tok(d), full(w_pool), full(pool_scale),
                  full(w_out), per_batch, per_batch, per_batch, full(wr_hi), full(wr_lo), full(b_router)],
        out_specs=[tok(d), tok(d), lane_tok(2), lane_tok(2),
                   pl.BlockSpec((1, N_EXPERTS, 1), lambda s: (routed(s), 0, 0))],
        out_shape=[jax.ShapeDtypeStruct((bn, sn, d), F32), jax.ShapeDtypeStruct((bn, sn, d), BF16),
                   jax.ShapeDtypeStruct((2, t), I32), jax.ShapeDtypeStruct((2, t), F32),
                   jax.ShapeDtypeStruct((bn * nt, N_EXPERTS, 1), F32)],
        scratch_shapes=[pltpu.VMEM((ts + 3 * POOL_HALO, D_POOL), F32), pltpu.VMEM((ts, ts), BF16),
                        pltpu.VMEM((N_EXPERTS, ts), F32)],
        compiler_params=pltpu.CompilerParams(
            dimension_semantics=("arbitrary",), vmem_limit_bytes=VMEM_LIMIT),
        name="mix_route",
    )(u, u_halo, u_halo, y_gla, x, w_pool, pool_scale, w_out, g1, sc2, sh2, wr_hi, wr_lo, b_router)


def _segment_copies(meta_ref, tile, make_copy, action):
    base = tile * META_PER_TILE
    for e in range(N_EXPERTS):
        n = pl.multiple_of(meta_ref[base + e], SEG_ALIGN)
        first_slot = pl.multiple_of(meta_ref[base + N_EXPERTS + e], SEG_ALIGN)
        first_row = pl.multiple_of(meta_ref[base + 2 * N_EXPERTS + e], SEG_ALIGN)

        @pl.when(n > 0)
        def _():
            action(make_copy(first_slot, first_row, n))


def _tile_slots_used(meta_ref, tile):
    last = tile * META_PER_TILE + N_EXPERTS - 1
    return pl.multiple_of(meta_ref[last] + meta_ref[last + N_EXPERTS], SEG_ALIGN)


def _start(copy):
    copy.start()


def _wait(copy):
    copy.wait()


def _pack_rows(x):
    lo = lax.bitcast_convert_type(x[:, :D_PACK], U32)
    hi = lax.bitcast_convert_type(x[:, D_PACK:], U32)
    return hi | (lo >> 16)


def _unpack_rows(u):
    lo = lax.bitcast_convert_type(u << 16, F32)
    hi = lax.bitcast_convert_type(u & jnp.uint32(0xFFFF0000), F32)
    return lo, hi


def _dispatch_kernel(meta_ref, tail_ref, h_ref, pos_ref, xs_ref, buf, zbuf, sems, sem):
    i = pl.program_id(0)
    ts = h_ref.shape[0]

    def tile_copy(tile):
        half = tile % 2

        def copy(first_slot, first_row, n):
            return pltpu.make_async_copy(buf.at[half, pl.ds(first_slot, n)], xs_ref.at[pl.ds(first_row, n)],
                                         sems.at[half])
        return copy

    def wait_tile(tile):
        tile_copy(tile)(0, 0, _tile_slots_used(meta_ref, tile)).wait()

    slot = lax.broadcasted_iota(I32, (TILE_SLOTS, ts), 0)
    perm = jnp.where((slot == pos_ref[0:1, :]) | (slot == pos_ref[1:2, :]), 1.0, 0.0).astype(BF16)
    rows = _dot(perm, h_ref[...])

    @pl.when(i > 1)
    def _():
        wait_tile(i - 2)

    buf[i % 2] = _pack_rows(rows)
    _segment_copies(meta_ref, i, tile_copy(i), _start)

    @pl.when(i == pl.num_programs(0) - 1)
    def _():
        @pl.when(i > 0)
        def _():
            wait_tile(i - 1)

        wait_tile(i)
        zbuf[...] = jnp.zeros_like(zbuf)

        def tail_copies(action):
            for e in range(N_EXPERTS):
                n = pl.multiple_of(tail_ref[e], SEG_ALIGN)
                first_row = pl.multiple_of(tail_ref[N_EXPERTS + e], SEG_ALIGN)

                @pl.when(n > 0)
                def _():
                    action(pltpu.make_async_copy(zbuf.at[pl.ds(0, n)], xs_ref.at[pl.ds(first_row, n)], sem))

        tail_copies(_start)
        tail_copies(_wait)

        def zero_block(blk, carry):
            first_row = pl.multiple_of(blk * MOE_BLOCK, MOE_BLOCK)
            fill = pltpu.make_async_copy(zbuf, xs_ref.at[pl.ds(first_row, MOE_BLOCK)], sem)
            fill.start()
            fill.wait()
            return carry

        lax.fori_loop(tail_ref[2 * N_EXPERTS], xs_ref.shape[0] // MOE_BLOCK, zero_block, 0)


def _dispatch(meta, tails, h, pos, n_rows_sorted):
    t, d = h.shape
    ts = SEQ_TILE
    return pl.pallas_call(
        _dispatch_kernel,
        grid_spec=pltpu.PrefetchScalarGridSpec(
            num_scalar_prefetch=2,
            grid=(t // ts,),
            in_specs=[pl.BlockSpec((ts, d), lambda i, m, tl: (i, 0)),
                      pl.BlockSpec((2, ts), lambda i, m, tl: (0, i))],
            out_specs=pl.BlockSpec(memory_space=pl.ANY),
            scratch_shapes=[pltpu.VMEM((2, TILE_SLOTS, D_PACK), U32), pltpu.VMEM((MOE_BLOCK, D_PACK), U32),
                            pltpu.SemaphoreType.DMA((2,)), pltpu.SemaphoreType.DMA(())],
        ),
        out_shape=jax.ShapeDtypeStruct((n_rows_sorted, D_PACK), U32),
        compiler_params=pltpu.CompilerParams(
            dimension_semantics=("arbitrary",), vmem_limit_bytes=VMEM_LIMIT),
        name="dispatch",
    )(meta, tails, h, pos)


def _ffn_kernel(blk_e_ref, n_used_ref, xs_ref, w1_ref, w3_ref, w2_ref, ys_ref):
    del blk_e_ref

    @pl.when(pl.program_id(0) < n_used_ref[0])
    def _():
        lo, hi = _unpack_rows(xs_ref[...])
        xb = jnp.concatenate([lo.astype(BF16), hi.astype(BF16)], axis=1)
        a = _dot(xb, w1_ref[0, 0].astype(BF16))
        up = _dot(xb, w3_ref[0, 0].astype(BF16))
        hid = (a * jax.nn.sigmoid(a)) * up
        y = _dot(hid.astype(BF16), w2_ref[0, 0].astype(BF16))
        ys_ref[...] = _pack_rows(y.astype(BF16).astype(F32))


def _ffn(layer, blk_e, n_used, xs, w1, w3, w2):
    rows = xs.shape[0]
    n_blocks = rows // MOE_BLOCK
    last = lambda i, n_used_ref: jnp.minimum(i, n_used_ref[0] - 1)
    row_spec = pl.BlockSpec((MOE_BLOCK, D_PACK), lambda i, be, nu: (last(i, nu), 0))
    w_spec = lambda a: pl.BlockSpec((1, 1) + a.shape[2:], lambda i, be, nu: (layer, be[last(i, nu)], 0, 0))
    return pl.pallas_call(
        _ffn_kernel,
        grid_spec=pltpu.PrefetchScalarGridSpec(
            num_scalar_prefetch=2,
            grid=(n_blocks,),
            in_specs=[row_spec, w_spec(w1), w_spec(w3), w_spec(w2)],
            out_specs=row_spec,
        ),
        out_shape=jax.ShapeDtypeStruct(xs.shape, U32),
        input_output_aliases={2: 0},
        compiler_params=pltpu.CompilerParams(
            dimension_semantics=("arbitrary",), vmem_limit_bytes=VMEM_LIMIT),
        name="expert_ffn",
    )(blk_e, n_used, xs, w1, w3, w2)


def _combine_kernel(meta_ref, ys_ref, x_ref, pos_ref, rw_ref, g2_ref, fg_ref, o_ref, buf, zbuf, sem,
                    *, final_norm):
    i = pl.program_id(0)
    ts = x_ref.shape[0]

    def fetch(tile):
        half = tile % 2

        def copy(first_slot, first_row, n):
            return pltpu.make_async_copy(ys_ref.at[pl.ds(first_row, n)],
                                         buf.at[half, pl.ds(first_slot, n)], sem.at[half])

        _segment_copies(meta_ref, tile, copy, _start)
        end = _tile_slots_used(meta_ref, tile)
        n_free = pl.multiple_of(TILE_SLOTS - end, SEG_ALIGN)

        @pl.when(n_free > 0)
        def _():
            pltpu.make_async_copy(zbuf.at[pl.ds(0, n_free)], buf.at[half, pl.ds(end, n_free)],
                                  sem.at[half]).start()

    @pl.when(i == 0)
    def _():
        zbuf[...] = jnp.zeros_like(zbuf)
        fetch(i)

    @pl.when(i + 1 < pl.num_programs(0))
    def _():
        fetch(i + 1)

    pltpu.make_async_copy(ys_ref.at[pl.ds(0, TILE_SLOTS)], buf.at[i % 2], sem.at[i % 2]).wait()
    lo, hi = _unpack_rows(buf[i % 2])
    rows = jnp.concatenate([lo.astype(BF16), hi.astype(BF16)], axis=1)
    slot = lax.broadcasted_iota(I32, (TILE_SLOTS, ts), 0)
    weights = (jnp.where(slot == pos_ref[0:1, :], rw_ref[0:1, :], 0.0)
               + jnp.where(slot == pos_ref[1:2, :], rw_ref[1:2, :], 0.0)).astype(BF16)
    x = x_ref[...] + g2_ref[0] * lax.dot_general(weights, rows, TN_DIMS, preferred_element_type=F32)
    if final_norm:
        ms = jnp.mean(x * x, axis=-1, keepdims=True)
        x = x * lax.rsqrt(ms + EPS) * fg_ref[...]
    o_ref[...] = x


def _combine(meta, ys, x, pos, route_w, g2, final_g, seq_len, final_norm):
    t, d = x.shape
    ts = SEQ_TILE
    tiles_per_seq = seq_len // ts
    lane_tok = pl.BlockSpec((2, ts), lambda i, m: (0, i))
    return pl.pallas_call(
        functools.partial(_combine_kernel, final_norm=final_norm),
        grid_spec=pltpu.PrefetchScalarGridSpec(
            num_scalar_prefetch=1,
            grid=(t // ts,),
            in_specs=[pl.BlockSpec(memory_space=pl.ANY),
                      pl.BlockSpec((ts, d), lambda i, m: (i, 0)),
                      lane_tok, lane_tok,
                      pl.BlockSpec((1, 1, d), lambda i, m: (i // tiles_per_seq, 0, 0)),
                      pl.BlockSpec((1, d), lambda i, m: (0, 0))],
            out_specs=pl.BlockSpec((ts, d), lambda i, m: (i, 0)),
            scratch_shapes=[pltpu.VMEM((2, TILE_SLOTS, D_PACK), U32),
                            pltpu.VMEM((TILE_SLOTS - 2 * ts, D_PACK), U32), pltpu.SemaphoreType.DMA((2,))],
        ),
        out_shape=jax.ShapeDtypeStruct((t, d), F32),
        compiler_params=pltpu.CompilerParams(
            dimension_semantics=("arbitrary",), vmem_limit_bytes=VMEM_LIMIT),
        name="combine",
    )(meta, ys, x, pos, route_w, g2, final_g.reshape(1, d))


def _dispatch_plan(tile_counts, n_assign):
    n_tiles = tile_counts.shape[0]
    seg = (tile_counts.astype(I32) + SEG_ALIGN - 1) // SEG_ALIGN * SEG_ALIGN
    totals = jnp.sum(seg, axis=0)
    padded = (totals + MOE_BLOCK - 1) // MOE_BLOCK * MOE_BLOCK
    pad_end = jnp.cumsum(padded)
    pad_start = pad_end - padded
    first_row = pad_start[None, :] + jnp.cumsum(seg, axis=0) - seg
    first_slot = jnp.cumsum(seg, axis=1) - seg
    meta = jnp.stack([seg, first_slot, first_row], axis=1).reshape(-1)
    max_rows = n_assign + n_tiles * N_EXPERTS * (SEG_ALIGN - 1)
    n_blocks = -(-max_rows // MOE_BLOCK) + N_EXPERTS
    blk_start = jnp.arange(n_blocks, dtype=I32) * MOE_BLOCK
    blk_e = jnp.minimum(jnp.sum((pad_end[None, :] <= blk_start[:, None]).astype(I32), axis=1), N_EXPERTS - 1)
    n_used = (pad_end[-1] // MOE_BLOCK).astype(I32).reshape(1)
    tails = jnp.concatenate([padded - totals, pad_start + totals, n_used])
    return meta.astype(I32), tails.astype(I32), blk_e.astype(I32), n_used, n_blocks * MOE_BLOCK


def kernel(x, c, w_mod, b_mod, w_in, w_pool, pool_scale, w_gk_up, b_gk, gla_norm_g, w_out,
           w_router, b_router, w1, w3, w2, final_g):
    bn, sn, d = x.shape
    t = bn * sn
    mod = _modulation(c, w_mod, b_mod).reshape(DEPTH, bn, 6, 1, d)

    w_main = w_in[:, :, :D_MAIN].astype(BF16)
    w_r = jnp.swapaxes(w_in[:, :, D_MAIN:], 1, 2).astype(BF16)
    zero_rank = jnp.zeros_like(w_gk_up[:, 0])
    w_gk = jnp.stack([jnp.concatenate([w_gk_up[:, 0], zero_rank], axis=1),
                      jnp.concatenate([zero_rank, w_gk_up[:, 1]], axis=1)], axis=1).astype(BF16)
    b_gk3 = b_gk.reshape(DEPTH, 2, 1, D_QK)
    w_pool_b = w_pool.astype(BF16)
    w_out_b = w_out.astype(BF16)
    wr_t = w_router.T
    wr_hi = wr_t.astype(BF16)
    wr_lo = (wr_t - wr_hi.astype(F32)).astype(BF16)
    br = b_router.reshape(N_EXPERTS, 1)

    for l in range(DEPTH):
        sh1, sc1, g1, sh2, sc2, g2 = (mod[l, :, i] for i in range(6))
        u, g, v, qf, kf, df, qb, kb, db, decf, decb = _inproj(
            x, sc1, sh1, w_main[l], w_r[l], w_gk[l], b_gk3[l])
        y_gla = _gla(qf, kf, df, qb, kb, db, v, decf, decb, g, gla_norm_g[l])
        x, h, pos, route_w, tile_counts = _mix(
            u, y_gla, x, w_pool_b[l], pool_scale[l].reshape(1, D_POOL), w_out_b[l], g1, sc2, sh2,
            wr_hi, wr_lo, br)
        meta, tails, blk_e, n_used, n_rows_sorted = _dispatch_plan(tile_counts[:, :, 0], 2 * t)
        xs = _dispatch(meta, tails, h.reshape(t, d), pos, n_rows_sorted)
        ys = _ffn(l, blk_e, n_used, xs, w1, w3, w2)
        x = _combine(meta, ys, x.reshape(t, d), pos, route_w, g2, final_g, sn, l == DEPTH - 1)
        x = x.reshape(bn, sn, d)
    return x
```
